```python
import math
import jax, jax.numpy as jnp
from jax import lax
import numpy as np

D_MODEL = 4096
BATCH = 1
SEQ = 8192
DEPTH = 2

CTX_LEN = 256
GRID_W = 64
EPS = 1e-6
ROPE_BASE = 10000.0

DA_HEAD_DIM = 128
DA_HEADS = D_MODEL // 512
DA_V_DIM = 2 * DA_HEAD_DIM
DA_WIDTH = DA_HEADS * DA_V_DIM
Q_BLOCK = 128

FT_GROUPS = 4
FT_GROUP_DIM = D_MODEL // (4 * FT_GROUPS)
FT_WIDTH = FT_GROUPS * FT_GROUP_DIM

GLA_HEADS = 4
GLA_V_DIM = D_MODEL // (4 * GLA_HEADS)
GLA_K_DIM = GLA_V_DIM // 2
GLA_WIDTH = GLA_HEADS * GLA_V_DIM
GLA_GATE_RANK = 16
GLA_TAU = 16.0
GLA_CHUNK = 64

MIX_WIDTH = DA_WIDTH + FT_WIDTH + GLA_WIDTH
IN_SIZES = (
    DA_HEADS * 2 * DA_HEAD_DIM,
    DA_HEADS * 2 * DA_HEAD_DIM,
    DA_WIDTH,
    FT_WIDTH,
    GLA_HEADS * GLA_K_DIM,
    GLA_HEADS * GLA_K_DIM,
    GLA_WIDTH,
    2 * GLA_GATE_RANK,
    GLA_WIDTH,
)
IN_WIDTH = sum(IN_SIZES)
D_FF = 256 * ((8 * D_MODEL // 3 + 255) // 256)
N_MOD = 6

kernel_name = 'hymba_style_diffattn_fnet_gla_dit'


def rms_norm(x, g):
    xf = x.astype(jnp.float32)
    y = xf * lax.rsqrt(jnp.mean(xf * xf, axis=-1, keepdims=True) + EPS)
    return (y * g.astype(jnp.float32)).astype(x.dtype)


def modulate(x, g, shift, scale):
    return rms_norm(x, g) * (1 + scale) + shift


def split_columns(p):
    out, start = [], 0
    for size in IN_SIZES:
        out.append(p[..., start:start + size])
        start += size
    return out


def split_heads(z, n):
    b, t, _ = z.shape
    return z.reshape(b, t, n, -1).transpose(0, 2, 1, 3)


def merge_heads(z):
    b, h, t, d = z.shape
    return z.transpose(0, 2, 1, 3).reshape(b, t, h * d)


def da_heads_qk(z):
    b, t, _ = z.shape
    return z.reshape(b, t, DA_HEADS, 2, DA_HEAD_DIM).transpose(0, 2, 3, 1, 4)


def axial_rope_tables(row, col):
    half = DA_HEAD_DIM // 2
    freqs = ROPE_BASE ** (-jnp.arange(0, half, 2, dtype=jnp.float32) / half)
    ar = row.astype(jnp.float32)[:, None] * freqs
    ac = col.astype(jnp.float32)[:, None] * freqs
    ang = jnp.concatenate([ar, ar, ac, ac], axis=-1)
    return jnp.cos(ang), jnp.sin(ang)


def apply_axial_rope(x, cos, sin):
    x1, x2, x3, x4 = jnp.split(x, 4, axis=-1)
    rot = jnp.concatenate([-x2, x1, -x4, x3], axis=-1)
    return (x.astype(jnp.float32) * cos + rot.astype(jnp.float32) * sin).astype(x.dtype)


def diff_weighted_values(q, k, v, lam):
    s = jnp.einsum('bhmqd,bhmkd->bhmqk', q, k).astype(jnp.float32) * (DA_HEAD_DIM ** -0.5)
    p = jax.nn.softmax(s, axis=-1)
    w = p[:, :, 0] - lam * p[:, :, 1]
    return jnp.einsum('bhqk,bhke->bhqe', w.astype(v.dtype), v)


def diff_attention_latent(q, k_all, v_all, lam):
    b, h, m, t, d = q.shape
    nb = t // Q_BLOCK
    qb = q.reshape(b, h, m, nb, Q_BLOCK, d).transpose(3, 0, 1, 2, 4, 5)
    ob = lax.map(lambda blk: diff_weighted_values(blk, k_all, v_all, lam), qb)
    return ob.transpose(1, 2, 0, 3, 4).reshape(b, h, t, DA_V_DIM)


def da_post(o, g, lam_init):
    return merge_heads(rms_norm(o, g) * (1.0 - lam_init))


def fourier_mix(u):
    b, t, _ = u.shape
    z = u.astype(jnp.float32).reshape(b, t, FT_GROUPS, FT_GROUP_DIM)
    f = jnp.fft.fft2(z, axes=(1, 3), norm='ortho').real
    return f.reshape(b, t, FT_WIDTH).astype(u.dtype)


def gla_prep(q, k, v, gd, w2, b2):
    qh = split_heads(q, GLA_HEADS).astype(jnp.float32) * (GLA_K_DIM ** -0.5)
    kh = split_heads(k, GLA_HEADS).astype(jnp.float32)
    vh = split_heads(v, GLA_HEADS).astype(jnp.float32)
    log_a = []
    for i, g in enumerate(jnp.split(gd, 2, axis=-1)):
        logits = (g @ w2[i] + b2[i]).astype(jnp.float32)
        log_a.append(split_heads(jax.nn.log_sigmoid(logits) / GLA_TAU, GLA_HEADS))
    return qh, kh, vh, log_a


def gla_chunked(q, k, v, log_a, s0):
    b, h, t, dk = q.shape
    n = t // GLA_CHUNK
    to_chunks = lambda z: z.reshape(b, h, n, GLA_CHUNK, z.shape[-1])
    q, k, v, log_a = to_chunks(q), to_chunks(k), to_chunks(v), to_chunks(log_a)
    cum = jnp.cumsum(log_a, axis=3)
    cum_last = cum[:, :, :, -1:]
    q_in = q * jnp.exp(cum)
    k_in = k * jnp.exp(-cum)
    mask = jnp.tril(jnp.ones((GLA_CHUNK, GLA_CHUNK), dtype=bool))
    a = jnp.where(mask, jnp.einsum('bhnid,bhnjd->bhnij', q_in, k_in), 0.0)
    o_intra = jnp.einsum('bhnij,bhnje->bhnie', a, v)
    kv = jnp.einsum('bhncd,bhnce->bhnde', k * jnp.exp(cum_last - cum), v)
    decay = jnp.exp(cum_last[:, :, :, 0])

    def step(s, inp):
        dec, kv_n = inp
        return dec[..., None] * s + kv_n, s

    s_final, s_before = lax.scan(step, s0, (jnp.moveaxis(decay, 2, 0), jnp.moveaxis(kv, 2, 0)))
    s_before = jnp.moveaxis(s_before, 0, 2)
    o_inter = jnp.einsum('bhncd,bhnde->bhnce', q_in, s_before)
    return (o_intra + o_inter).reshape(b, h, t, -1), s_final


def gla_scan(q, k, v, log_a, s0, reverse):
    if reverse:
        q, k, v, log_a = (jnp.flip(z, axis=2) for z in (q, k, v, log_a))
    o, s = gla_chunked(q, k, v, log_a, s0)
    if reverse:
        o = jnp.flip(o, axis=2)
    return o, s


def gla_output(o, r, g):
    y = rms_norm(o, g) * jax.nn.silu(split_heads(r, GLA_HEADS).astype(jnp.float32))
    return merge_heads(y).astype(r.dtype)


def conv_ffn(h, w_up, conv_w, conv_b, w_down):
    u = h @ w_up
    up = jnp.pad(u, ((0, 0), (1, 1), (0, 0)))
    u = up[:, :-2] * conv_w[0] + up[:, 1:-1] * conv_w[1] + up[:, 2:] * conv_w[2] + conv_b
    gate, val = jnp.split(u, 2, axis=-1)
    return (jax.nn.silu(gate) * val) @ w_down


def setup_inputs(seed: int = 0) -> dict:
    key = jax.random.key(seed)
    ks = jax.random.split(key, 24)
    f32 = jnp.float32

    def nrm(k, shape, scale=1.0):
        return jax.random.normal(k, shape, f32) * scale

    def gain(k, shape):
        return 1.0 + 0.02 * jax.random.normal(k, shape, f32)

    return {
        'x': nrm(ks[0], (BATCH, SEQ, D_MODEL)),
        'c': nrm(ks[1], (BATCH, D_MODEL)),
        'ctx': nrm(ks[2], (BATCH, CTX_LEN, D_MODEL)),
        'c_ctx': nrm(ks[3], (D_MODEL,)),
        'w_ada': nrm(ks[4], (DEPTH, D_MODEL, N_MOD * D_MODEL), D_MODEL ** -0.5),
        'b_ada': nrm(ks[5], (DEPTH, N_MOD * D_MODEL), 0.01),
        'norm1_g': gain(ks[6], (DEPTH, D_MODEL)),
        'norm2_g': gain(ks[7], (DEPTH, D_MODEL)),
        'w_in': nrm(ks[8], (DEPTH, D_MODEL, IN_WIDTH), D_MODEL ** -0.5),
        'q_norm_g': gain(ks[9], (DEPTH, DA_HEAD_DIM)),
        'k_norm_g': gain(ks[10], (DEPTH, DA_HEAD_DIM)),
        'lambda_q1': nrm(ks[11], (DEPTH, DA_HEAD_DIM), 0.1),
        'lambda_k1': nrm(ks[12], (DEPTH, DA_HEAD_DIM), 0.1),
        'lambda_q2': nrm(ks[13], (DEPTH, DA_HEAD_DIM), 0.1),
        'lambda_k2': nrm(ks[14], (DEPTH, DA_HEAD_DIM), 0.1),
        'da_subln_g': gain(ks[15], (DEPTH, DA_V_DIM)),
        'gla_gate_w2': nrm(ks[16], (DEPTH, 2, GLA_GATE_RANK, GLA_HEADS * GLA_K_DIM), GLA_GATE_RANK ** -0.5),
        'gla_gate_b': nrm(ks[17], (DEPTH, 2, GLA_HEADS * GLA_K_DIM), 0.01),
        'gla_norm_g': gain(ks[18], (DEPTH, GLA_V_DIM)),
        'w_out': nrm(ks[19], (DEPTH, MIX_WIDTH, D_MODEL), MIX_WIDTH ** -0.5),
        'w_up': nrm(ks[20], (DEPTH, D_MODEL, 2 * D_FF), D_MODEL ** -0.5),
        'conv_w': nrm(ks[21], (DEPTH, 3, 2 * D_FF), 3 ** -0.5),
        'conv_b': nrm(ks[22], (DEPTH, 2 * D_FF), 0.01),
        'w_down': nrm(ks[23], (DEPTH, D_FF, D_MODEL), D_FF ** -0.5),
    }


def reference(x, c, ctx, c_ctx, w_ada, b_ada, norm1_g, norm2_g, w_in, q_norm_g, k_norm_g,
              lambda_q1, lambda_k1, lambda_q2, lambda_k2, da_subln_g, gla_gate_w2, gla_gate_b,
              gla_norm_g, w_out, w_up, conv_w, conv_b, w_down):
    rows = x.shape[1] // GRID_W
    row = jnp.repeat(jnp.arange(rows), GRID_W)
    col = jnp.tile(jnp.arange(GRID_W), rows)
    cos, sin = axial_rope_tables(row, col)
    bsz = x.shape[0]
    s_zero = jnp.zeros((bsz, GLA_HEADS, GLA_K_DIM, GLA_V_DIM), jnp.float32)
    xc = ctx

    for l in range(DEPTH):
        need_ctx = l < DEPTH - 1
        sh1_l, sc1_l, g1_l, sh2_l, sc2_l, g2_l = jnp.split(
            (jax.nn.silu(c) @ w_ada[l] + b_ada[l])[:, None, :], N_MOD, axis=-1)
        sh1_c, sc1_c, g1_c, sh2_c, sc2_c, g2_c = jnp.split(
            jax.nn.silu(c_ctx) @ w_ada[l] + b_ada[l], N_MOD, axis=-1)

        p_l = split_columns(modulate(x, norm1_g[l], sh1_l, sc1_l) @ w_in[l])
        p_c = split_columns(modulate(xc, norm1_g[l], sh1_c, sc1_c) @ w_in[l])

        lam_init = 0.8 - 0.6 * math.exp(-0.3 * l)
        lam = (jnp.exp(jnp.sum(lambda_q1[l] * lambda_k1[l])) - jnp.exp(jnp.sum(lambda_q2[l] * lambda_k2[l]))
               + lam_init).astype(jnp.float32)
        dq_l = apply_axial_rope(rms_norm(da_heads_qk(p_l[0]), q_norm_g[l]), cos, sin)
        dk_l = apply_axial_rope(rms_norm(da_heads_qk(p_l[1]), k_norm_g[l]), cos, sin)
        dv_l = split_heads(p_l[2], DA_HEADS)
        dq_c = rms_norm(da_heads_qk(p_c[0]), q_norm_g[l])
        dk_c = rms_norm(da_heads_qk(p_c[1]), k_norm_g[l])
        dv_c = split_heads(p_c[2], DA_HEADS)
        k_all = jnp.concatenate([dk_c, dk_l], axis=3)
        v_all = jnp.concatenate([dv_c, dv_l], axis=2)
        da_l = da_post(diff_attention_latent(dq_l, k_all, v_all, lam), da_subln_g[l], lam_init)

        ft_l = fourier_mix(p_l[3])

        gq_l, gk_l, gv_l, la_l = gla_prep(p_l[4], p_l[5], p_l[6], p_l[7], gla_gate_w2[l], gla_gate_b[l])
        gq_c, gk_c, gv_c, la_c = gla_prep(p_c[4], p_c[5], p_c[6], p_c[7], gla_gate_w2[l], gla_gate_b[l])
        oc_f, sc_f = gla_scan(gq_c, gk_c, gv_c, la_c[0], s_zero, False)
        oc_b, sc_b = gla_scan(gq_c, gk_c, gv_c, la_c[1], s_zero, True)
        ol_f, _ = gla_scan(gq_l, gk_l, gv_l, la_l[0], sc_f, False)
        ol_b, _ = gla_scan(gq_l, gk_l, gv_l, la_l[1], sc_b, True)
        gla_l = gla_output(ol_f + ol_b, p_l[8], gla_norm_g[l])

        y_l = jnp.concatenate([da_l, ft_l, gla_l], axis=-1) @ w_out[l]
        x_new = x + g1_l * y_l
        x_new = x_new + g2_l * conv_ffn(modulate(x_new, norm2_g[l], sh2_l, sc2_l),
                                        w_up[l], conv_w[l], conv_b[l], w_down[l])

        if need_ctx:
            da_c = da_post(diff_weighted_values(dq_c, dk_c, dv_c, lam), da_subln_g[l], lam_init)
            ft_c = fourier_mix(p_c[3])
            gla_c = gla_output(oc_f + oc_b, p_c[8], gla_norm_g[l])
            y_c = jnp.concatenate([da_c, ft_c, gla_c], axis=-1) @ w_out[l]
            xc = xc + g1_c * y_c
            xc = xc + g2_c * conv_ffn(modulate(xc, norm2_g[l], sh2_c, sc2_c),
                                      w_up[l], conv_w[l], conv_b[l], w_down[l])
        x = x_new

    return x
```

```python
import functools
import math

import numpy as np
import jax
import jax.numpy as jnp
from jax import lax
from jax.experimental import pallas as pl
from jax.experimental.pallas import tpu as pltpu

F32 = jnp.float32
BF16 = jnp.bfloat16

D_MODEL = 4096
CTX_LEN = 256
GRID_W = 64
EPS = 1e-6
ROPE_BASE = 10000.0
DA_HEAD_DIM = 128
DA_HEADS = 8
DA_V_DIM = 256
DA_WIDTH = DA_HEADS * DA_V_DIM
FT_GROUPS = 4
FT_GROUP_DIM = 256
FT_WIDTH = FT_GROUPS * FT_GROUP_DIM
GLA_HEADS = 4
GLA_V_DIM = 256
GLA_K_DIM = 128
GLA_WIDTH = GLA_HEADS * GLA_V_DIM
GLA_GATE_RANK = 16
GLA_TAU = 16.0
GLA_CHUNK = 64
D_FF = 11008
N_MOD = 6

P_Q, P_K, P_V = 0, 2048, 4096
P_GQ, P_GK, P_GV, P_GR = 6144, 6656, 7168, 8192
P_WIDTH = 9216

V7X_VMEM_LIMIT_BYTES = 56 * 1024 * 1024
LANES = 128
FFT_N2 = 128
GLA_ROWS = 256


def _cparams(sem):
    return pltpu.CompilerParams(dimension_semantics=sem, vmem_limit_bytes=V7X_VMEM_LIMIT_BYTES)


def _sigmoid(x):
    return 1.0 / (1.0 + jnp.exp(-x))


def _adaln_body(c_ref, w_ref, b_ref, o_ref):
    c = c_ref[...]
    s = (c * _sigmoid(c)).astype(BF16)
    w = w_ref[...].astype(BF16)
    o_ref[...] = jnp.dot(s, w, preferred_element_type=F32) + b_ref[...]


def adaln(cvec, w_ada, b_ada, tn=512):
    depth, d, n = w_ada.shape
    return pl.pallas_call(
        _adaln_body,
        grid=(depth, n // tn),
        in_specs=[
            pl.BlockSpec((8, d), lambda l, j: (0, 0)),
            pl.BlockSpec((None, d, tn), lambda l, j: (l, 0, j)),
            pl.BlockSpec((None, 1, tn), lambda l, j: (l, 0, j)),
        ],
        out_specs=pl.BlockSpec((None, 8, tn), lambda l, j: (l, 0, j)),
        out_shape=jax.ShapeDtypeStruct((depth, 8, n), F32),
        compiler_params=_cparams(("arbitrary", "arbitrary")),
        name="adaln",
    )(cvec, w_ada, b_ada.reshape(depth, 1, n))


def _modulate_body(x_ref, g_ref, sh_ref, sc_ref, o_ref):
    x = x_ref[...]
    ms = jnp.mean(x * x, axis=-1, keepdims=True)
    y = x * lax.rsqrt(ms + EPS) * g_ref[...]
    o_ref[...] = (y * (1.0 + sc_ref[...]) + sh_ref[...]).astype(o_ref.dtype)


def modulate(x, g, shift, scale, tm=256):
    m, d = x.shape
    vec = pl.BlockSpec((1, d), lambda i: (0, 0))
    return pl.pallas_call(
        _modulate_body,
        grid=(m // tm,),
        in_specs=[pl.BlockSpec((tm, d), lambda i: (i, 0)), vec, vec, vec],
        out_specs=pl.BlockSpec((tm, d), lambda i: (i, 0)),
        out_shape=jax.ShapeDtypeStruct((m, d), BF16),
        compiler_params=_cparams(("arbitrary",)),
        name="modulate",
    )(x, g.reshape(1, d), shift.reshape(1, d), scale.reshape(1, d))


def _mm_body(a_ref, b_ref, o_ref):
    o_ref[...] = jnp.dot(a_ref[...], b_ref[...], preferred_element_type=F32).astype(o_ref.dtype)


def _mm_res_body(a_ref, b_ref, r_ref, g_ref, o_ref):
    acc = jnp.dot(a_ref[...], b_ref[...], preferred_element_type=F32)
    o_ref[...] = r_ref[...] + g_ref[...] * acc


def matmul(a, b, *, tm, tn, out_dtype=BF16, res=None, gate=None, name="matmul"):
    m, k = a.shape
    n = b.shape[1]
    tm = min(tm, m)
    in_specs = [pl.BlockSpec((tm, k), lambda i, j: (i, 0)), pl.BlockSpec((k, tn), lambda i, j: (0, j))]
    args = [a, b]
    body = _mm_body
    if res is not None:
        in_specs += [pl.BlockSpec((tm, tn), lambda i, j: (i, j)), pl.BlockSpec((1, tn), lambda i, j: (0, j))]
        args += [res, gate.reshape(1, n)]
        body = _mm_res_body
        out_dtype = F32
    return pl.pallas_call(
        body,
        grid=(m // tm, n // tn),
        in_specs=in_specs,
        out_specs=pl.BlockSpec((tm, tn), lambda i, j: (i, j)),
        out_shape=jax.ShapeDtypeStruct((m, n), out_dtype),
        compiler_params=_cparams(("parallel", "arbitrary")),
        name=name,
    )(*args)


def _mm_ksplit_res_body(a_ref, b_ref, r_ref, g_ref, o_ref, acc_ref):
    kk = pl.program_id(2)

    @pl.when(kk == 0)
    def _():
        acc_ref[...] = jnp.zeros_like(acc_ref)

    acc_ref[...] += jnp.dot(a_ref[...], b_ref[...], preferred_element_type=F32)

    @pl.when(kk == pl.num_programs(2) - 1)
    def _():
        o_ref[...] = r_ref[...] + g_ref[...] * acc_ref[...]


def matmul_ksplit_res(a, b, res, gate, *, tm, tn, tk, name="matmul_ksplit"):
    m, k = a.shape
    n = b.shape[1]
    tm = min(tm, m)
    return pl.pallas_call(
        _mm_ksplit_res_body,
        grid=(m // tm, n // tn, k // tk),
        in_specs=[
            pl.BlockSpec((tm, tk), lambda i, j, kk: (i, kk)),
            pl.BlockSpec((tk, tn), lambda i, j, kk: (kk, j)),
            pl.BlockSpec((tm, tn), lambda i, j, kk: (i, j)),
            pl.BlockSpec((1, tn), lambda i, j, kk: (0, j)),
        ],
        out_specs=pl.BlockSpec((tm, tn), lambda i, j, kk: (i, j)),
        out_shape=jax.ShapeDtypeStruct((m, n), F32),
        scratch_shapes=[pltpu.VMEM((tm, tn), F32)],
        compiler_params=_cparams(("parallel", "arbitrary", "arbitrary")),
        name=name,
    )(a, b, res, gate.reshape(1, n))


def _qk_prep_body(use_rope, p_ref, gq_ref, gk_ref, cos_ref, sa_ref, sb_ref, q_ref, k_ref):
    n_groups = (2 * DA_HEADS * DA_HEAD_DIM) // LANES
    if use_rope:
        cos, sa, sb = cos_ref[...], sa_ref[...], sb_ref[...]

    def prep(col, gain, scale):
        x = p_ref[:, col:col + LANES].astype(F32)
        ms = jnp.mean(x * x, axis=-1, keepdims=True)
        y = x * lax.rsqrt(ms + EPS) * gain
        if use_rope:
            y = y * cos + pltpu.roll(y, 32, 1) * sa + pltpu.roll(y, 96, 1) * sb
        return y * scale if scale != 1.0 else y

    gq, gk = gq_ref[...], gk_ref[...]
    for g in range(n_groups):
        q_ref[:, g * LANES:(g + 1) * LANES] = prep(P_Q + g * LANES, gq, DA_HEAD_DIM ** -0.5).astype(q_ref.dtype)
        k_ref[:, g * LANES:(g + 1) * LANES] = prep(P_K + g * LANES, gk, 1.0).astype(k_ref.dtype)


def qk_prep(p, gq, gk, rope, tm=256):
    rows = p.shape[0]
    use_rope = rope is not None
    width = 2 * DA_HEADS * DA_HEAD_DIM
    vec = pl.BlockSpec((1, LANES), lambda i: (0, 0))
    tab = pl.BlockSpec((tm, LANES), lambda i: (i, 0))
    if use_rope:
        tabs = list(rope)
    else:
        tabs = [jnp.zeros((rows, LANES), F32)] * 3
    return pl.pallas_call(
        functools.partial(_qk_prep_body, use_rope),
        grid=(rows // tm,),
        in_specs=[pl.BlockSpec((tm, 2 * width), lambda i: (i, 0)), vec, vec, tab, tab, tab],
        out_specs=[pl.BlockSpec((tm, width), lambda i: (i, 0))] * 2,
        out_shape=[jax.ShapeDtypeStruct((rows, width), BF16)] * 2,
        compiler_params=_cparams(("arbitrary",)),
        name="qk_prep",
    )(p, gq.reshape(1, LANES), gk.reshape(1, LANES), *tabs)


def _attn_body(lam_init, n_lat_chunks, tk, has_ctx, *refs):
    if has_ctx:
        (q_ref, k_ref, v_ref, kc_ref, vc_ref, lq1_ref, lk1_ref, lq2_ref, lk2_ref, g_ref,
         o_ref, m_ref, l_ref, acc_ref) = refs
    else:
        (q_ref, k_ref, v_ref, lq1_ref, lk1_ref, lq2_ref, lk2_ref, g_ref,
         o_ref, m_ref, l_ref, acc_ref) = refs
    d = DA_HEAD_DIM
    q = (q_ref[:, :d], q_ref[:, d:])
    m_ref[...] = jnp.full_like(m_ref, -jnp.inf)
    l_ref[...] = jnp.zeros_like(l_ref)
    acc_ref[...] = jnp.zeros_like(acc_ref)

    def update(k_blk, v_blk):
        for mi in range(2):
            s = lax.dot_general(q[mi], k_blk[:, mi * d:(mi + 1) * d], (((1,), (1,)), ((), ())),
                                preferred_element_type=F32)
            m_old = m_ref[mi]
            m_new = jnp.maximum(m_old, jnp.max(s, axis=-1, keepdims=True))
            alpha = jnp.exp(m_old - m_new)
            p = jnp.exp(s - m_new)
            l_ref[mi] = alpha * l_ref[mi] + jnp.sum(p, axis=-1, keepdims=True)
            acc_ref[mi] = alpha * acc_ref[mi] + jnp.dot(p.astype(BF16), v_blk, preferred_element_type=F32)
            m_ref[mi] = m_new

    if has_ctx:
        update(kc_ref[...], vc_ref[...])

    def step(c, carry):
        off = pl.multiple_of(c * tk, tk)
        update(k_ref[pl.ds(off, tk), :], v_ref[pl.ds(off, tk), :])
        return carry

    lax.fori_loop(0, n_lat_chunks, step, 0)

    lam = (jnp.exp(jnp.sum(lq1_ref[...] * lk1_ref[...])) - jnp.exp(jnp.sum(lq2_ref[...] * lk2_ref[...]))
           + lam_init)
    o = acc_ref[0] / l_ref[0] - lam * (acc_ref[1] / l_ref[1])
    ms = jnp.mean(o * o, axis=-1, keepdims=True)
    o_ref[...] = (o * lax.rsqrt(ms + EPS) * g_ref[...] * (1.0 - lam_init)).astype(o_ref.dtype)


def diff_attention(q, k, p, kc, pc, lams, subln_g, lam_init, *, tq, tk):
    t_q = q.shape[0]
    t_k = k.shape[0]
    has_ctx = kc is not None
    tq = min(tq, t_q)
    tk = min(tk, t_k)
    hw = 2 * DA_HEAD_DIM
    v_blk0 = P_V // DA_V_DIM
    in_specs = [
        pl.BlockSpec((tq, hw), lambda h, i: (i, h)),
        pl.BlockSpec((t_k, hw), lambda h, i: (0, h)),
        pl.BlockSpec((t_k, DA_V_DIM), lambda h, i: (0, v_blk0 + h)),
    ]
    args = [q, k, p]
    if has_ctx:
        t_c = kc.shape[0]
        in_specs += [pl.BlockSpec((t_c, hw), lambda h, i: (0, h)),
                     pl.BlockSpec((t_c, DA_V_DIM), lambda h, i: (0, v_blk0 + h))]
        args += [kc, pc]
    vec = pl.BlockSpec((1, DA_HEAD_DIM), lambda h, i: (0, 0))
    in_specs += [vec] * 4 + [pl.BlockSpec((1, DA_V_DIM), lambda h, i: (0, 0))]
    args += [v.reshape(1, DA_HEAD_DIM) for v in lams] + [subln_g.reshape(1, DA_V_DIM)]
    return pl.pallas_call(
        functools.partial(_attn_body, lam_init, t_k // tk, tk, has_ctx),
        grid=(DA_HEADS, t_q // tq),
        in_specs=in_specs,
        out_specs=pl.BlockSpec((tq, DA_V_DIM), lambda h, i: (i, h)),
        out_shape=jax.ShapeDtypeStruct((t_q, DA_WIDTH), BF16),
        scratch_shapes=[pltpu.VMEM((2, tq, 1), F32), pltpu.VMEM((2, tq, 1), F32),
                        pltpu.VMEM((2, tq, DA_V_DIM), F32)],
        compiler_params=_cparams(("parallel", "arbitrary")),
        name="diff_attention",
    )(*args)


def _dft_tables_latent(t_len):
    n2 = FFT_N2
    n1 = t_len // n2
    f1 = np.arange(n1)[:, None]
    t1 = np.arange(n1)[None, :]
    ang = 2.0 * np.pi * ((f1 * t1) % n1) / n1
    stage_a = np.concatenate([np.cos(ang), np.sin(ang)], axis=0)
    f2 = np.arange(n2)[None, :, None]
    t2 = np.arange(n2)[None, None, :]
    f1b = np.arange(n1)[:, None, None]
    idx = (t2 * f2 * n1 + t2 * f1b) % t_len
    th = 2.0 * np.pi * idx / t_len
    mr, mi = np.cos(th), -np.sin(th)
    stage_b = np.concatenate([np.concatenate([mr, mi], axis=2), np.concatenate([mi, -mr], axis=2)], axis=1)
    return stage_a.astype(np.float32), stage_b.astype(np.float32)


def _dft_table_channels(t_len, sign):
    c = np.arange(FT_GROUP_DIM)
    ang = 2.0 * np.pi * ((c[:, None] * c[None, :]) % FT_GROUP_DIM) / FT_GROUP_DIM
    scale = 1.0 / math.sqrt(t_len * FT_GROUP_DIM)
    return (np.concatenate([np.cos(ang), sign * np.sin(ang)], axis=0) * scale).astype(np.float32)


def _fft_a_body(cs_ref, z_ref, p_ref, q_ref):
    n1 = z_ref.shape[0]
    r = jnp.dot(cs_ref[...], z_ref[...], preferred_element_type=F32)
    p_ref[...] = r[:n1].astype(p_ref.dtype)
    q_ref[...] = r[n1:].astype(q_ref.dtype)


def _fft_b_body(m_ref, p_ref, q_ref, ch_ref, o_ref):
    n2 = FFT_N2
    pq = jnp.concatenate([p_ref[...], q_ref[...]], axis=0)
    r = jnp.dot(m_ref[...], pq, preferred_element_type=F32)
    ch = ch_ref[...]
    for g in range(FT_GROUPS):
        c0 = g * FT_GROUP_DIM
        rg = jnp.concatenate([r[:n2, c0:c0 + FT_GROUP_DIM], r[n2:, c0:c0 + FT_GROUP_DIM]], axis=1).astype(BF16)
        o_ref[:, c0:c0 + FT_GROUP_DIM] = jnp.dot(rg, ch, preferred_element_type=F32).astype(o_ref.dtype)


def fourier_mix_latent(z, tn=8192):
    t_len, width = z.shape
    n2 = FFT_N2
    n1 = t_len // n2
    ta, tb = _dft_tables_latent(t_len)
    cs = jnp.asarray(ta).astype(BF16)
    mt = jnp.asarray(tb).astype(BF16)
    ch = jnp.asarray(_dft_table_channels(t_len, 1.0)).astype(BF16)
    cols = n2 * width
    tn = min(tn, cols)
    zv = z.reshape(n1, cols)
    pv, qv = pl.pallas_call(
        _fft_a_body,
        grid=(cols // tn,),
        in_specs=[pl.BlockSpec((2 * n1, n1), lambda j: (0, 0)), pl.BlockSpec((n1, tn), lambda j: (0, j))],
        out_specs=[pl.BlockSpec((n1, tn), lambda j: (0, j))] * 2,
        out_shape=[jax.ShapeDtypeStruct((n1, cols), BF16)] * 2,
        compiler_params=_cparams(("arbitrary",)),
        name="fft_stage_a",
    )(cs, zv)
    pm = pv.reshape(t_len, width)
    qm = qv.reshape(t_len, width)
    out = pl.pallas_call(
        _fft_b_body,
        grid=(n1,),
        in_specs=[
            pl.BlockSpec((None, 2 * n2, 2 * n2), lambda f: (f, 0, 0)),
            pl.BlockSpec((n2, width), lambda f: (f, 0)),
            pl.BlockSpec((n2, width), lambda f: (f, 0)),
            pl.BlockSpec((2 * FT_GROUP_DIM, FT_GROUP_DIM), lambda f: (0, 0)),
        ],
        out_specs=pl.BlockSpec((n2, width), lambda f: (0, f)),
        out_shape=jax.ShapeDtypeStruct((n2, n1 * width), BF16),
        compiler_params=_cparams(("arbitrary",)),
        name="fft_stage_b",
    )(mt, pm, qm, ch)
    return out.reshape(t_len, width)


def _fft_ctx_body(cs_ref, z_ref, ch_ref, o_ref):
    t_len = z_ref.shape[0]
    r = jnp.dot(cs_ref[...], z_ref[...], preferred_element_type=F32)
    ch = ch_ref[...]
    for g in range(FT_GROUPS):
        c0 = g * FT_GROUP_DIM
        rg = jnp.concatenate([r[:t_len, c0:c0 + FT_GROUP_DIM], r[t_len:, c0:c0 + FT_GROUP_DIM]], axis=1).astype(BF16)
        o_ref[:, c0:c0 + FT_GROUP_DIM] = jnp.dot(rg, ch, preferred_element_type=F32).astype(o_ref.dtype)


def fourier_mix_short(z):
    t_len, width = z.shape
    f = np.arange(t_len)
    ang = 2.0 * np.pi * ((f[:, None] * f[None, :]) % t_len) / t_len
    cs = jnp.asarray(np.concatenate([np.cos(ang), np.sin(ang)], axis=0).astype(np.float32)).astype(BF16)
    ch = jnp.asarray(_dft_table_channels(t_len, -1.0)).astype(BF16)
    return pl.pallas_call(
        _fft_ctx_body,
        out_shape=jax.ShapeDtypeStruct((t_len, width), BF16),
        compiler_params=pltpu.CompilerParams(vmem_limit_bytes=V7X_VMEM_LIMIT_BYTES),
        name="fft_short",
    )(cs, z, ch)


def _split3(x):
    hi = x.astype(BF16)
    r1 = x - hi.astype(F32)
    mid = r1.astype(BF16)
    lo = (r1 - mid.astype(F32)).astype(BF16)
    return hi, mid, lo


def _gla_body(reverse, q_ref, k_ref, v_ref, gd_ref, qc_ref, kc_ref, vc_ref, gdc_ref, w2_ref, b2_ref,
              o_ref, st_ref):
    n = pl.program_id(0)
    rows = GLA_ROWS
    ch = GLA_CHUNK
    n_chunks = rows // ch
    dk, dv = GLA_K_DIM, GLA_V_DIM

    @pl.when(n == 0)
    def _():
        st_ref[...] = jnp.zeros_like(st_ref)

    is_ctx = n == 0
    q = jnp.where(is_ctx, qc_ref[...], q_ref[...])
    k = jnp.where(is_ctx, kc_ref[...], k_ref[...])
    v = jnp.where(is_ctx, vc_ref[...], v_ref[...])
    gd = jnp.where(is_ctx, gdc_ref[...], gd_ref[...])

    logits = jnp.dot(gd, w2_ref[...], preferred_element_type=F32) + b2_ref[...]
    log_a = (jnp.minimum(logits, 0.0) - jnp.log(1.0 + jnp.exp(-jnp.abs(logits)))) / GLA_TAU

    ri = lax.broadcasted_iota(jnp.int32, (rows, rows), 0)
    ci = lax.broadcasted_iota(jnp.int32, (rows, rows), 1)
    same = (ri // ch) == (ci // ch)
    tri = jnp.logical_and(same, (ci >= ri) if reverse else (ci <= ri))
    tri_b = jnp.where(tri, 1.0, 0.0).astype(BF16)
    hi, mid, lo = _split3(log_a)
    cum = (jnp.dot(tri_b, hi, preferred_element_type=F32) + jnp.dot(tri_b, mid, preferred_element_type=F32)
           + jnp.dot(tri_b, lo, preferred_element_type=F32))
    e_pos = jnp.exp(cum)
    e_neg = jnp.exp(-cum)
    order = list(range(n_chunks))[::-1] if reverse else list(range(n_chunks))
    last_row = [(c * ch) if reverse else (c * ch + ch - 1) for c in range(n_chunks)]
    scale = dk ** -0.5

    for h in range(GLA_HEADS):
        ks = slice(h * dk, (h + 1) * dk)
        vs = slice(h * dv, (h + 1) * dv)
        qf = q[:, ks].astype(F32) * scale
        kf = k[:, ks].astype(F32)
        vh = v[:, vs]
        q_in = (qf * e_pos[:, ks]).astype(BF16)
        k_in = (kf * e_neg[:, ks]).astype(BF16)
        a = lax.dot_general(q_in, k_in, (((1,), (1,)), ((), ())), preferred_element_type=F32)
        a = jnp.where(tri, a, 0.0).astype(BF16)
        o_intra = jnp.dot(a, vh, preferred_element_type=F32)
        st = st_ref[h]
        for c in order:
            rs = slice(c * ch, (c + 1) * ch)
            cum_last = cum[last_row[c]:last_row[c] + 1, ks]
            o_inter = lax.dot_general(q_in[rs], st.astype(BF16), (((1,), (1,)), ((), ())),
                                      preferred_element_type=F32)
            o_ref[rs, vs] = o_intra[rs] + o_inter
            k_dec = (kf[rs] * jnp.exp(cum_last - cum[rs, ks])).astype(BF16)
            kv_t = lax.dot_general(vh[rs], k_dec, (((0,), (0,)), ((), ())), preferred_element_type=F32)
            st = jnp.exp(cum_last) * st + kv_t
        st_ref[h] = st


def gla_direction(p, gd, pc, gdc, w2pad, b2, reverse):
    t_len = p.shape[0]
    rows = GLA_ROWS
    nb = t_len // rows
    hk = GLA_HEADS * GLA_K_DIM

    if reverse:
        lat = lambda n: jnp.where(n == 0, nb - 1, nb - n)
    else:
        lat = lambda n: jnp.where(n == 0, 0, n - 1)
    out_blk = lambda n: jnp.where(n == 0, nb, lat(n))
    in_specs = [
        pl.BlockSpec((rows, hk), lambda n: (lat(n), P_GQ // hk)),
        pl.BlockSpec((rows, hk), lambda n: (lat(n), P_GK // hk)),
        pl.BlockSpec((rows, GLA_WIDTH), lambda n: (lat(n), P_GV // GLA_WIDTH)),
        pl.BlockSpec((rows, LANES), lambda n: (lat(n), 0)),
        pl.BlockSpec((rows, hk), lambda n: (0, P_GQ // hk)),
        pl.BlockSpec((rows, hk), lambda n: (0, P_GK // hk)),
        pl.BlockSpec((rows, GLA_WIDTH), lambda n: (0, P_GV // GLA_WIDTH)),
        pl.BlockSpec((rows, LANES), lambda n: (0, 0)),
        pl.BlockSpec((LANES, hk), lambda n: (0, 0)),
        pl.BlockSpec((1, hk), lambda n: (0, 0)),
    ]
    return pl.pallas_call(
        functools.partial(_gla_body, reverse),
        grid=(nb + 1,),
        in_specs=in_specs,
        out_specs=pl.BlockSpec((rows, GLA_WIDTH), lambda n: (out_blk(n), 0)),
        out_shape=jax.ShapeDtypeStruct((t_len + rows, GLA_WIDTH), F32),
        scratch_shapes=[pltpu.VMEM((GLA_HEADS, GLA_V_DIM, GLA_K_DIM), F32)],
        compiler_params=_cparams(("arbitrary",)),
        name="gla_bwd" if reverse else "gla_fwd",
    )(p, p, p, gd, pc, pc, pc, gdc, w2pad, b2.reshape(1, hk))


def _gla_out_body(of_ref, ob_ref, r_ref, g_ref, y_ref):
    g = g_ref[...]
    for h in range(GLA_HEADS):
        vs = slice(h * GLA_V_DIM, (h + 1) * GLA_V_DIM)
        o = of_ref[:, vs] + ob_ref[:, vs]
        ms = jnp.mean(o * o, axis=-1, keepdims=True)
        r = r_ref[:, vs].astype(F32)
        y_ref[:, vs] = (o * lax.rsqrt(ms + EPS) * g * (r * _sigmoid(r))).astype(y_ref.dtype)


def gla_output(o_f, o_b, p, g, row_block0, n_rows, tm=256):
    r_blk = P_GR // GLA_WIDTH
    o_spec = pl.BlockSpec((tm, GLA_WIDTH), lambda i: (row_block0 + i, 0))
    return pl.pallas_call(
        _gla_out_body,
        grid=(n_rows // tm,),
        in_specs=[o_spec, o_spec, pl.BlockSpec((tm, GLA_WIDTH), lambda i: (i, r_blk)),
                  pl.BlockSpec((1, GLA_V_DIM), lambda i: (0, 0))],
        out_specs=pl.BlockSpec((tm, GLA_WIDTH), lambda i: (i, 0)),
        out_shape=jax.ShapeDtypeStruct((n_rows, GLA_WIDTH), BF16),
        compiler_params=_cparams(("arbitrary",)),
        name="gla_output",
    )(o_f, o_b, p, g.reshape(1, GLA_V_DIM))


def _convgate_body(n_row_tiles, ug_ref, ugp_ref, ugn_ref, uv_ref, uvp_ref, uvn_ref,
                   wg_ref, wv_ref, bg_ref, bv_ref, o_ref):
    i = pl.program_id(0)
    tm = ug_ref.shape[0]
    rows = lax.broadcasted_iota(jnp.int32, (tm, 1), 0)
    has_prev = (i > 0).astype(F32)
    has_next = (i < n_row_tiles - 1).astype(F32)

    def conv(x_ref, p_ref, n_ref, w_ref, b_ref):
        x = x_ref[...].astype(F32)
        prev_row = p_ref[7:8, :].astype(F32) * has_prev
        next_row = n_ref[0:1, :].astype(F32) * has_next
        up = jnp.where(rows == 0, prev_row, pltpu.roll(x, 1, 0))
        dn = jnp.where(rows == tm - 1, next_row, pltpu.roll(x, tm - 1, 0))
        return up * w_ref[0:1, :] + x * w_ref[1:2, :] + dn * w_ref[2:3, :] + b_ref[...]

    gate = conv(ug_ref, ugp_ref, ugn_ref, wg_ref, bg_ref)
    val = conv(uv_ref, uvp_ref, uvn_ref, wv_ref, bv_ref)
    o_ref[...] = (gate * _sigmoid(gate) * val).astype(o_ref.dtype)


def conv_gate(u, conv_w, conv_b, tm=1024, tc=256):
    m, two_f = u.shape
    f = two_f // 2
    tm = min(tm, m)
    nt = m // tm
    nc = f // tc
    hb = tm // 8
    last8 = m // 8 - 1
    main_g = pl.BlockSpec((tm, tc), lambda i, j: (i, j))
    prev_g = pl.BlockSpec((8, tc), lambda i, j: (jnp.maximum(i * hb - 1, 0), j))
    next_g = pl.BlockSpec((8, tc), lambda i, j: (jnp.minimum((i + 1) * hb, last8), j))
    main_v = pl.BlockSpec((tm, tc), lambda i, j: (i, j + nc))
    prev_v = pl.BlockSpec((8, tc), lambda i, j: (jnp.maximum(i * hb - 1, 0), j + nc))
    next_v = pl.BlockSpec((8, tc), lambda i, j: (jnp.minimum((i + 1) * hb, last8), j + nc))
    w_g = pl.BlockSpec((3, tc), lambda i, j: (0, j))
    w_v = pl.BlockSpec((3, tc), lambda i, j: (0, j + nc))
    b_g = pl.BlockSpec((1, tc), lambda i, j: (0, j))
    b_v = pl.BlockSpec((1, tc), lambda i, j: (0, j + nc))
    cb = conv_b.reshape(1, two_f)
    return pl.pallas_call(
        functools.partial(_convgate_body, nt),
        grid=(nt, nc),
        in_specs=[main_g, prev_g, next_g, main_v, prev_v, next_v, w_g, w_v, b_g, b_v],
        out_specs=pl.BlockSpec((tm, tc), lambda i, j: (i, j)),
        out_shape=jax.ShapeDtypeStruct((m, f), BF16),
        compiler_params=_cparams(("arbitrary", "arbitrary")),
        name="conv_gate",
    )(u, u, u, u, u, u, conv_w, conv_w, cb, cb)


def _rope_tables(t_len):
    rows = t_len // GRID_W
    row = jnp.repeat(jnp.arange(rows), GRID_W).astype(F32)
    col = jnp.tile(jnp.arange(GRID_W), rows).astype(F32)
    half = DA_HEAD_DIM // 2
    freqs = ROPE_BASE ** (-jnp.arange(0, half, 2, dtype=F32) / half)
    ar = row[:, None] * freqs
    ac = col[:, None] * freqs
    ang = jnp.concatenate([ar, ar, ac, ac], axis=-1)
    cos, sin = jnp.cos(ang), jnp.sin(ang)
    hi = (jnp.arange(DA_HEAD_DIM) % 64) >= 32
    return cos, jnp.where(hi, sin, 0.0), jnp.where(hi, 0.0, -sin)


def _ffn(x, h, w_up, conv_w, conv_b, w_down, gate):
    u = matmul(h, w_up, tm=1024, tn=512, name="ffn_up")
    act = conv_gate(u, conv_w, conv_b)
    return matmul_ksplit_res(act, w_down, x, gate, tm=512, tn=1024, tk=D_FF // 2, name="ffn_down")


def kernel(x, c, ctx, c_ctx, w_ada, b_ada, norm1_g, norm2_g, w_in, q_norm_g, k_norm_g, lambda_q1, lambda_k1,
           lambda_q2, lambda_k2, da_subln_g, gla_gate_w2, gla_gate_b, gla_norm_g, w_out, w_up, conv_w, conv_b,
           w_down):
    depth = w_ada.shape[0]
    t_len = x.shape[1]
    d = D_MODEL
    xl = x[0]
    xc = ctx[0]
    rope = _rope_tables(t_len)

    cvec = jnp.zeros((8, d), F32).at[0].set(c[0]).at[1].set(c_ctx)
    mod = adaln(cvec, w_ada, b_ada)

    for l in range(depth):
        need_ctx = l < depth - 1
        lam_init = 0.8 - 0.6 * math.exp(-0.3 * l)
        mods_l = [mod[l, 0, i * d:(i + 1) * d] for i in range(N_MOD)]
        mods_c = [mod[l, 1, i * d:(i + 1) * d] for i in range(N_MOD)]

        wi = w_in[l]
        w_pack = jnp.concatenate([wi[:, 0:6144], wi[:, 7168:9216], wi[:, 9248:10272]], axis=1).astype(BF16)
        w_ft = wi[:, 6144:7168].astype(BF16)
        w_gd = jnp.pad(wi[:, 9216:9248], ((0, 0), (0, LANES - 2 * GLA_GATE_RANK))).astype(BF16)

        h_l = modulate(xl, norm1_g[l], mods_l[0], mods_l[1])
        h_c = modulate(xc, norm1_g[l], mods_c[0], mods_c[1])
        p_l = matmul(h_l, w_pack, tm=1024, tn=1024, name="in_proj")
        p_c = matmul(h_c, w_pack, tm=1024, tn=1024, name="in_proj_ctx")
        z_l = matmul(h_l, w_ft, tm=1024, tn=1024, name="in_proj_ft")
        gd_l = matmul(h_l, w_gd, tm=1024, tn=LANES, name="in_proj_gd")
        gd_c = matmul(h_c, w_gd, tm=1024, tn=LANES, name="in_proj_gd_ctx")

        q_l, k_l = qk_prep(p_l, q_norm_g[l], k_norm_g[l], rope)
        q_c, k_c = qk_prep(p_c, q_norm_g[l], k_norm_g[l], None)
        lams = (lambda_q1[l], lambda_k1[l], lambda_q2[l], lambda_k2[l])
        da_l = diff_attention(q_l, k_l, p_l, k_c, p_c, lams, da_subln_g[l], lam_init, tq=256, tk=512)

        ft_l = fourier_mix_latent(z_l)

        w2 = gla_gate_w2[l]
        zpad = jnp.zeros((LANES - 2 * GLA_GATE_RANK, w2.shape[-1]), F32)
        zr = jnp.zeros((GLA_GATE_RANK, w2.shape[-1]), F32)
        w2_f = jnp.concatenate([w2[0], zr, zpad], axis=0).astype(BF16)
        w2_b = jnp.concatenate([zr, w2[1], zpad], axis=0).astype(BF16)
        o_f = gla_direction(p_l, gd_l, p_c, gd_c, w2_f, gla_gate_b[l, 0], False)
        o_b = gla_direction(p_l, gd_l, p_c, gd_c, w2_b, gla_gate_b[l, 1], True)
        gla_l = gla_output(o_f, o_b, p_l, gla_norm_g[l], 0, t_len)

        wo = w_out[l].astype(BF16)
        wu = w_up[l].astype(BF16)
        wd = w_down[l].astype(BF16)

        mix_l = jnp.concatenate([da_l, ft_l, gla_l], axis=-1)
        x_new = matmul(mix_l, wo, tm=1024, tn=512, res=xl, gate=mods_l[2], name="out_proj")
        h2_l = modulate(x_new, norm2_g[l], mods_l[3], mods_l[4])
        x_new = _ffn(x_new, h2_l, wu, conv_w[l], conv_b[l], wd, mods_l[5])

        if need_ctx:
            z_c = matmul(h_c, w_ft, tm=1024, tn=1024, name="in_proj_ft_ctx")
            da_c = diff_attention(q_c, k_c, p_c, None, None, lams, da_subln_g[l], lam_init, tq=256, tk=256)
            ft_c = fourier_mix_short(z_c)
            gla_c = gla_output(o_f, o_b, p_c, gla_norm_g[l], t_len // 256, xc.shape[0])
            mix_c = jnp.concatenate([da_c, ft_c, gla_c], axis=-1)
            xc_new = matmul(mix_c, wo, tm=1024, tn=512, res=xc, gate=mods_c[2], name="out_proj_ctx")
            h2_c = modulate(xc_new, norm2_g[l], mods_c[3], mods_c[4])
            xc = _ffn(xc_new, h2_c, wu, conv_w[l], conv_b[l], wd, mods_c[5])
        xl = x_new

    return xl[None]
```

```python
import functools
import math

import numpy as np
import jax
import jax.numpy as jnp
from jax import lax
from jax.experimental import pallas as pl
from jax.experimental.pallas import tpu as pltpu

F32 = jnp.float32
BF16 = jnp.bfloat16

D_MODEL = 4096
CTX_LEN = 256
GRID_W = 64
EPS = 1e-6
ROPE_BASE = 10000.0
DA_HEAD_DIM = 128
DA_HEADS = 8
DA_V_DIM = 256
DA_WIDTH = DA_HEADS * DA_V_DIM
FT_GROUPS = 4
FT_GROUP_DIM = 256
FT_WIDTH = FT_GROUPS * FT_GROUP_DIM
GLA_HEADS = 4
GLA_V_DIM = 256
GLA_K_DIM = 128
GLA_WIDTH = GLA_HEADS * GLA_V_DIM
GLA_GATE_RANK = 16
GLA_TAU = 16.0
GLA_CHUNK = 64
D_FF = 11008
N_MOD = 6

P_Q, P_K, P_V = 0, 2048, 4096
P_GQ, P_GK, P_GV, P_GR = 6144, 6656, 7168, 8192
P_WIDTH = 9216

V7X_VMEM_LIMIT_BYTES = 56 * 1024 * 1024
LANES = 128
FFT_N2 = 128
GLA_ROWS = 256
ATTN_TK = 512
Q_SCALE = DA_HEAD_DIM ** -0.5 * math.log2(math.e)


def _cparams(sem):
    return pltpu.CompilerParams(dimension_semantics=sem, vmem_limit_bytes=V7X_VMEM_LIMIT_BYTES)


def _sigmoid(x):
    return 1.0 / (1.0 + jnp.exp(-x))


def _adaln_body(c_ref, w_ref, b_ref, o_ref):
    c = c_ref[...]
    s = (c * _sigmoid(c)).astype(BF16)
    w = w_ref[...].astype(BF16)
    o_ref[...] = jnp.dot(s, w, preferred_element_type=F32) + b_ref[...]


def adaln(cvec, w_ada, b_ada, tn=512):
    depth, d, n = w_ada.shape
    return pl.pallas_call(
        _adaln_body,
        grid=(depth, n // tn),
        in_specs=[
            pl.BlockSpec((8, d), lambda l, j: (0, 0)),
            pl.BlockSpec((None, d, tn), lambda l, j: (l, 0, j)),
            pl.BlockSpec((None, 1, tn), lambda l, j: (l, 0, j)),
        ],
        out_specs=pl.BlockSpec((None, 8, tn), lambda l, j: (l, 0, j)),
        out_shape=jax.ShapeDtypeStruct((depth, 8, n), F32),
        compiler_params=_cparams(("arbitrary", "arbitrary")),
        name="adaln",
    )(cvec, w_ada, b_ada.reshape(depth, 1, n))


def _modulate_body(x_ref, g_ref, sh_ref, sc_ref, o_ref):
    x = x_ref[...]
    ms = jnp.mean(x * x, axis=-1, keepdims=True)
    y = x * lax.rsqrt(ms + EPS) * g_ref[...]
    o_ref[...] = (y * (1.0 + sc_ref[...]) + sh_ref[...]).astype(o_ref.dtype)


def modulate(x, g, shift, scale, tm=256):
    m, d = x.shape
    vec = pl.BlockSpec((1, d), lambda i: (0, 0))
    return pl.pallas_call(
        _modulate_body,
        grid=(m // tm,),
        in_specs=[pl.BlockSpec((tm, d), lambda i: (i, 0)), vec, vec, vec],
        out_specs=pl.BlockSpec((tm, d), lambda i: (i, 0)),
        out_shape=jax.ShapeDtypeStruct((m, d), BF16),
        compiler_params=_cparams(("arbitrary",)),
        name="modulate",
    )(x, g.reshape(1, d), shift.reshape(1, d), scale.reshape(1, d))


def _mm_body(a_ref, b_ref, o_ref):
    o_ref[...] = jnp.dot(a_ref[...], b_ref[...], preferred_element_type=F32).astype(o_ref.dtype)


def _mm_res_body(a_ref, b_ref, r_ref, g_ref, o_ref):
    acc = jnp.dot(a_ref[...], b_ref[...], preferred_element_type=F32)
    o_ref[...] = r_ref[...] + g_ref[...] * acc


def matmul(a, b, *, tm, tn, out_dtype=BF16, res=None, gate=None, name="matmul"):
    m, k = a.shape
    n = b.shape[1]
    tm = min(tm, m)
    in_specs = [pl.BlockSpec((tm, k), lambda i, j: (i, 0)), pl.BlockSpec((k, tn), lambda i, j: (0, j))]
    args = [a, b]
    body = _mm_body
    if res is not None:
        in_specs += [pl.BlockSpec((tm, tn), lambda i, j: (i, j)), pl.BlockSpec((1, tn), lambda i, j: (0, j))]
        args += [res, gate.reshape(1, n)]
        body = _mm_res_body
        out_dtype = F32
    return pl.pallas_call(
        body,
        grid=(m // tm, n // tn),
        in_specs=in_specs,
        out_specs=pl.BlockSpec((tm, tn), lambda i, j: (i, j)),
        out_shape=jax.ShapeDtypeStruct((m, n), out_dtype),
        compiler_params=_cparams(("parallel", "arbitrary")),
        name=name,
    )(*args)


def _mm_ksplit_res_body(a_ref, b_ref, r_ref, g_ref, o_ref, acc_ref):
    kk = pl.program_id(2)

    @pl.when(kk == 0)
    def _():
        acc_ref[...] = jnp.zeros_like(acc_ref)

    acc_ref[...] += jnp.dot(a_ref[...], b_ref[...], preferred_element_type=F32)

    @pl.when(kk == pl.num_programs(2) - 1)
    def _():
        o_ref[...] = r_ref[...] + g_ref[...] * acc_ref[...]


def matmul_ksplit_res(a, b, res, gate, *, tm, tn, tk, name="matmul_ksplit"):
    m, k = a.shape
    n = b.shape[1]
    tm = min(tm, m)
    return pl.pallas_call(
        _mm_ksplit_res_body,
        grid=(m // tm, n // tn, k // tk),
        in_specs=[
            pl.BlockSpec((tm, tk), lambda i, j, kk: (i, kk)),
            pl.BlockSpec((tk, tn), lambda i, j, kk: (kk, j)),
            pl.BlockSpec((tm, tn), lambda i, j, kk: (i, j)),
            pl.BlockSpec((1, tn), lambda i, j, kk: (0, j)),
        ],
        out_specs=pl.BlockSpec((tm, tn), lambda i, j, kk: (i, j)),
        out_shape=jax.ShapeDtypeStruct((m, n), F32),
        scratch_shapes=[pltpu.VMEM((tm, tn), F32)],
        compiler_params=_cparams(("parallel", "arbitrary", "arbitrary")),
        name=name,
    )(a, b, res, gate.reshape(1, n))


def _qk_prep_body(use_rope, p_ref, gq_ref, gk_ref, cos_ref, sa_ref, sb_ref, qt_ref, k_ref, vt_ref):
    n_groups = (2 * DA_HEADS * DA_HEAD_DIM) // LANES
    if use_rope:
        cos, sa, sb = cos_ref[...], sa_ref[...], sb_ref[...]

    def prep(col, gain, scale):
        x = p_ref[:, col:col + LANES].astype(F32)
        ms = jnp.mean(x * x, axis=-1, keepdims=True)
        y = x * lax.rsqrt(ms + EPS) * gain
        if use_rope:
            y = y * cos + pltpu.roll(y, 32, 1) * sa + pltpu.roll(y, 96, 1) * sb
        return y * scale if scale != 1.0 else y

    gq, gk = gq_ref[...], gk_ref[...]
    for g in range(n_groups):
        qt_ref[g * LANES:(g + 1) * LANES, :] = prep(P_Q + g * LANES, gq, Q_SCALE).T.astype(qt_ref.dtype)
        k_ref[:, g * LANES:(g + 1) * LANES] = prep(P_K + g * LANES, gk, 1.0).astype(k_ref.dtype)
    for h in range(DA_HEADS):
        v = p_ref[:, P_V + h * DA_V_DIM:P_V + (h + 1) * DA_V_DIM].astype(F32)
        vt_ref[h] = v.T.astype(vt_ref.dtype)


def qk_prep(p, gq, gk, rope, tm):
    rows = p.shape[0]
    use_rope = rope is not None
    width = 2 * DA_HEADS * DA_HEAD_DIM
    vec = pl.BlockSpec((1, LANES), lambda i: (0, 0))
    tab = pl.BlockSpec((tm, LANES), lambda i: (i, 0))
    if use_rope:
        tabs = list(rope)
    else:
        tabs = [jnp.zeros((rows, LANES), F32)] * 3
    return pl.pallas_call(
        functools.partial(_qk_prep_body, use_rope),
        grid=(rows // tm,),
        in_specs=[pl.BlockSpec((tm, P_V + DA_WIDTH), lambda i: (i, 0)), vec, vec, tab, tab, tab],
        out_specs=[pl.BlockSpec((width, tm), lambda i: (0, i)),
                   pl.BlockSpec((tm, width), lambda i: (i, 0)),
                   pl.BlockSpec((DA_HEADS, None, DA_V_DIM, tm), lambda i: (0, i, 0, 0))],
        out_shape=[jax.ShapeDtypeStruct((width, rows), BF16),
                   jax.ShapeDtypeStruct((rows, width), BF16),
                   jax.ShapeDtypeStruct((DA_HEADS, rows // tm, DA_V_DIM, tm), BF16)],
        compiler_params=_cparams(("arbitrary",)),
        name="qk_prep",
    )(p, gq.reshape(1, LANES), gk.reshape(1, LANES), *tabs)


def _attn_body(lam_init, n_chunks, tk, has_ctx, *refs):
    if has_ctx:
        (qt_ref, k_ref, vt_ref, kc_ref, vtc_ref, lq1_ref, lk1_ref, lq2_ref, lk2_ref, g_ref,
         o_ref, m_ref, l_ref, acc_ref) = refs
    else:
        (qt_ref, k_ref, vt_ref, lq1_ref, lk1_ref, lq2_ref, lk2_ref, g_ref,
         o_ref, m_ref, l_ref, acc_ref) = refs
    d = DA_HEAD_DIM
    qt = (qt_ref[:d, :], qt_ref[d:, :])
    m_ref[...] = jnp.full_like(m_ref, -jnp.inf)
    l_ref[...] = jnp.zeros_like(l_ref)
    acc_ref[...] = jnp.zeros_like(acc_ref)

    def update(k_blk, vt_blk):
        for mi in range(2):
            s = jnp.dot(k_blk[:, mi * d:(mi + 1) * d], qt[mi], preferred_element_type=F32)
            m_old = m_ref[mi]
            m_new = jnp.maximum(m_old, jnp.max(s, axis=0, keepdims=True))
            alpha = jnp.exp2(m_old - m_new)
            p = jnp.exp2(s - m_new)
            l_ref[mi] = alpha * l_ref[mi] + jnp.sum(p, axis=0, keepdims=True)
            acc_ref[mi] = alpha * acc_ref[mi] + jnp.dot(vt_blk, p.astype(BF16), preferred_element_type=F32)
            m_ref[mi] = m_new

    if has_ctx:
        update(kc_ref[...], vtc_ref[...])

    def step(c, carry):
        off = pl.multiple_of(c * tk, tk)
        update(k_ref[pl.ds(off, tk), :], vt_ref[c])
        return carry

    lax.fori_loop(0, n_chunks, step, 0, unroll=math.gcd(n_chunks, 8))

    lam =(jnp.exp(jnp.sum(lq1_ref[...] * lk1_ref[...])) - jnp.exp(jnp.sum(lq2_ref[...] * lk2_ref[...]))
           + lam_init)
    ot = acc_ref[0] * (1.0 / l_ref[0]) - lam * (acc_ref[1] * (1.0 / l_ref[1]))
    ms = jnp.mean(ot * ot, axis=0, keepdims=True)
    o = (ot * lax.rsqrt(ms + EPS)).T
    o_ref[...] = (o * g_ref[...] * (1.0 - lam_init)).astype(o_ref.dtype)


def diff_attention(qt, k, vt, kc, vtc, lams, subln_g, lam_init, *, tq):
    t_q = qt.shape[1]
    t_k = k.shape[0]
    n_chunks, tk = vt.shape[1], vt.shape[3]
    has_ctx = kc is not None
    tq = min(tq, t_q)
    hw = 2 * DA_HEAD_DIM
    in_specs = [
        pl.BlockSpec((hw, tq), lambda h, i: (h, i)),
        pl.BlockSpec((t_k, hw), lambda h, i: (0, h)),
        pl.BlockSpec((None, n_chunks, DA_V_DIM, tk), lambda h, i: (h, 0, 0, 0)),
    ]
    args = [qt, k, vt]
    if has_ctx:
        t_c = kc.shape[0]
        in_specs += [pl.BlockSpec((t_c, hw), lambda h, i: (0, h)),
                     pl.BlockSpec((None, None, DA_V_DIM, t_c), lambda h, i: (h, 0, 0, 0))]
        args += [kc, vtc]
    vec = pl.BlockSpec((1, DA_HEAD_DIM), lambda h, i: (0, 0))
    in_specs += [vec] * 4 + [pl.BlockSpec((1, DA_V_DIM), lambda h, i: (0, 0))]
    args += [v.reshape(1, DA_HEAD_DIM) for v in lams] + [subln_g.reshape(1, DA_V_DIM)]
    return pl.pallas_call(
        functools.partial(_attn_body, lam_init, n_chunks, tk, has_ctx),
        grid=(DA_HEADS, t_q // tq),
        in_specs=in_specs,
        out_specs=pl.BlockSpec((tq, DA_V_DIM), lambda h, i: (i, h)),
        out_shape=jax.ShapeDtypeStruct((t_q, DA_WIDTH), BF16),
        scratch_shapes=[pltpu.VMEM((2, 1, tq), F32), pltpu.VMEM((2, 1, tq), F32),
                        pltpu.VMEM((2, DA_V_DIM, tq), F32)],
        compiler_params=_cparams(("parallel", "arbitrary")),
        name="diff_attention",
    )(*args)


def _dft_tables_latent(t_len):
    n2 = FFT_N2
    n1 = t_len // n2
    f1 = np.arange(n1)[:, None]
    t1 = np.arange(n1)[None, :]
    ang = 2.0 * np.pi * ((f1 * t1) % n1) / n1
    stage_a = np.concatenate([np.cos(ang), np.sin(ang)], axis=0)
    f2 = np.arange(n2)[None, :, None]
    t2 = np.arange(n2)[None, None, :]
    f1b = np.arange(n1)[:, None, None]
    idx = (t2 * f2 * n1 + t2 * f1b) % t_len
    th = 2.0 * np.pi * idx / t_len
    mr, mi = np.cos(th), -np.sin(th)
    stage_b = np.concatenate([np.concatenate([mr, mi], axis=2), np.concatenate([mi, -mr], axis=2)], axis=1)
    return stage_a.astype(np.float32), stage_b.astype(np.float32)


def _dft_table_channels(t_len, sign):
    c = np.arange(FT_GROUP_DIM)
    ang = 2.0 * np.pi * ((c[:, None] * c[None, :]) % FT_GROUP_DIM) / FT_GROUP_DIM
    scale = 1.0 / math.sqrt(t_len * FT_GROUP_DIM)
    return (np.concatenate([np.cos(ang), sign * np.sin(ang)], axis=0) * scale).astype(np.float32)


def _fft_a_body(cs_ref, z_ref, p_ref, q_ref):
    n1 = z_ref.shape[0]
    r = jnp.dot(cs_ref[...], z_ref[...], preferred_element_type=F32)
    p_ref[...] = r[:n1].astype(p_ref.dtype)
    q_ref[...] = r[n1:].astype(q_ref.dtype)


def _fft_b_body(m_ref, p_ref, q_ref, ch_ref, o_ref):
    n2 = FFT_N2
    pq = jnp.concatenate([p_ref[...], q_ref[...]], axis=0)
    r = jnp.dot(m_ref[...], pq, preferred_element_type=F32)
    ch = ch_ref[...]
    for g in range(FT_GROUPS):
        c0 = g * FT_GROUP_DIM
        rg = jnp.concatenate([r[:n2, c0:c0 + FT_GROUP_DIM], r[n2:, c0:c0 + FT_GROUP_DIM]], axis=1).astype(BF16)
        o_ref[:, c0:c0 + FT_GROUP_DIM] = jnp.dot(rg, ch, preferred_element_type=F32).astype(o_ref.dtype)


def fourier_mix_latent(z, tn=8192):
    t_len, width = z.shape
    n2 = FFT_N2
    n1 = t_len // n2
    ta, tb = _dft_tables_latent(t_len)
    cs = jnp.asarray(ta).astype(BF16)
    mt = jnp.asarray(tb).astype(BF16)
    ch = jnp.asarray(_dft_table_channels(t_len, 1.0)).astype(BF16)
    cols = n2 * width
    tn = min(tn, cols)
    zv = z.reshape(n1, cols)
    pv, qv = pl.pallas_call(
        _fft_a_body,
        grid=(cols // tn,),
        in_specs=[pl.BlockSpec((2 * n1, n1), lambda j: (0, 0)), pl.BlockSpec((n1, tn), lambda j: (0, j))],
        out_specs=[pl.BlockSpec((n1, tn), lambda j: (0, j))] * 2,
        out_shape=[jax.ShapeDtypeStruct((n1, cols), BF16)] * 2,
        compiler_params=_cparams(("arbitrary",)),
        name="fft_stage_a",
    )(cs, zv)
    pm = pv.reshape(t_len, width)
    qm = qv.reshape(t_len, width)
    out = pl.pallas_call(
        _fft_b_body,
        grid=(n1,),
        in_specs=[
            pl.BlockSpec((None, 2 * n2, 2 * n2), lambda f: (f, 0, 0)),
            pl.BlockSpec((n2, width), lambda f: (f, 0)),
            pl.BlockSpec((n2, width), lambda f: (f, 0)),
            pl.BlockSpec((2 * FT_GROUP_DIM, FT_GROUP_DIM), lambda f: (0, 0)),
        ],
        out_specs=pl.BlockSpec((n2, width), lambda f: (0, f)),
        out_shape=jax.ShapeDtypeStruct((n2, n1 * width), BF16),
        compiler_params=_cparams(("arbitrary",)),
        name="fft_stage_b",
    )(mt, pm, qm, ch)
    return out.reshape(t_len, width)


def _fft_ctx_body(cs_ref, z_ref, ch_ref, o_ref):
    t_len = z_ref.shape[0]
    r = jnp.dot(cs_ref[...], z_ref[...], preferred_element_type=F32)
    ch = ch_ref[...]
    for g in range(FT_GROUPS):
        c0 = g * FT_GROUP_DIM
        rg = jnp.concatenate([r[:t_len, c0:c0 + FT_GROUP_DIM], r[t_len:, c0:c0 + FT_GROUP_DIM]], axis=1).astype(BF16)
        o_ref[:, c0:c0 + FT_GROUP_DIM] = jnp.dot(rg, ch, preferred_element_type=F32).astype(o_ref.dtype)


def fourier_mix_short(z):
    t_len, width = z.shape
    f = np.arange(t_len)
    ang = 2.0 * np.pi * ((f[:, None] * f[None, :]) % t_len) / t_len
    cs = jnp.asarray(np.concatenate([np.cos(ang), np.sin(ang)], axis=0).astype(np.float32)).astype(BF16)
    ch = jnp.asarray(_dft_table_channels(t_len, -1.0)).astype(BF16)
    return pl.pallas_call(
        _fft_ctx_body,
        out_shape=jax.ShapeDtypeStruct((t_len, width), BF16),
        compiler_params=pltpu.CompilerParams(vmem_limit_bytes=V7X_VMEM_LIMIT_BYTES),
        name="fft_short",
    )(cs, z, ch)


def _split3(x):
    hi = x.astype(BF16)
    r1 = x - hi.astype(F32)
    mid = r1.astype(BF16)
    lo = (r1 - mid.astype(F32)).astype(BF16)
    return hi, mid, lo


def _gla_body(reverse, q_ref, k_ref, v_ref, gd_ref, qc_ref, kc_ref, vc_ref, gdc_ref, w2_ref, b2_ref,
              o_ref, st_ref):
    n = pl.program_id(0)
    rows = GLA_ROWS
    ch = GLA_CHUNK
    n_chunks = rows // ch
    dk, dv = GLA_K_DIM, GLA_V_DIM

    @pl.when(n == 0)
    def _():
        st_ref[...] = jnp.zeros_like(st_ref)

    is_ctx = n == 0
    q = jnp.where(is_ctx, qc_ref[...], q_ref[...])
    k = jnp.where(is_ctx, kc_ref[...], k_ref[...])
    v = jnp.where(is_ctx, vc_ref[...], v_ref[...])
    gd = jnp.where(is_ctx, gdc_ref[...], gd_ref[...])

    logits = jnp.dot(gd, w2_ref[...], preferred_element_type=F32) + b2_ref[...]
    log_a = (jnp.minimum(logits, 0.0) - jnp.log(1.0 + jnp.exp(-jnp.abs(logits)))) / GLA_TAU

    ri = lax.broadcasted_iota(jnp.int32, (rows, rows), 0)
    ci = lax.broadcasted_iota(jnp.int32, (rows, rows), 1)
    same = (ri // ch) == (ci // ch)
    tri = jnp.logical_and(same, (ci >= ri) if reverse else (ci <= ri))
    tri_b = jnp.where(tri, 1.0, 0.0).astype(BF16)
    hi, mid, lo = _split3(log_a)
    cum = (jnp.dot(tri_b, hi, preferred_element_type=F32) + jnp.dot(tri_b, mid, preferred_element_type=F32)
           + jnp.dot(tri_b, lo, preferred_element_type=F32))
    e_pos = jnp.exp(cum)
    e_neg = jnp.exp(-cum)
    order = list(range(n_chunks))[::-1] if reverse else list(range(n_chunks))
    last_row = [(c * ch) if reverse else (c * ch + ch - 1) for c in range(n_chunks)]
    scale = dk ** -0.5

    for h in range(GLA_HEADS):
        ks = slice(h * dk, (h + 1) * dk)
        vs = slice(h * dv, (h + 1) * dv)
        qf = q[:, ks].astype(F32) * scale
        kf = k[:, ks].astype(F32)
        vh = v[:, vs]
        q_in = (qf * e_pos[:, ks]).astype(BF16)
        k_in = (kf * e_neg[:, ks]).astype(BF16)
        a = lax.dot_general(q_in, k_in, (((1,), (1,)), ((), ())), preferred_element_type=F32)
        a = jnp.where(tri, a, 0.0).astype(BF16)
        o_intra = jnp.dot(a, vh, preferred_element_type=F32)
        st = st_ref[h]
        for c in order:
            rs = slice(c * ch, (c + 1) * ch)
            cum_last = cum[last_row[c]:last_row[c] + 1, ks]
            o_inter = lax.dot_general(q_in[rs], st.astype(BF16), (((1,), (1,)), ((), ())),
                                      preferred_element_type=F32)
            o_ref[rs, vs] = o_intra[rs] + o_inter
            k_dec = (kf[rs] * jnp.exp(cum_last - cum[rs, ks])).astype(BF16)
            kv_t = lax.dot_general(vh[rs], k_dec, (((0,), (0,)), ((), ())), preferred_element_type=F32)
            st = jnp.exp(cum_last) * st + kv_t
        st_ref[h] = st


def gla_direction(p, gd, pc, gdc, w2pad, b2, reverse):
    t_len = p.shape[0]
    rows = GLA_ROWS
    nb = t_len // rows
    hk = GLA_HEADS * GLA_K_DIM

    if reverse:
        lat = lambda n: jnp.where(n == 0, nb - 1, nb - n)
    else:
        lat = lambda n: jnp.where(n == 0, 0, n - 1)
    out_blk = lambda n: jnp.where(n == 0, nb, lat(n))
    in_specs = [
        pl.BlockSpec((rows, hk), lambda n: (lat(n), P_GQ // hk)),
        pl.BlockSpec((rows, hk), lambda n: (lat(n), P_GK // hk)),
        pl.BlockSpec((rows, GLA_WIDTH), lambda n: (lat(n), P_GV // GLA_WIDTH)),
        pl.BlockSpec((rows, LANES), lambda n: (lat(n), 0)),
        pl.BlockSpec((rows, hk), lambda n: (0, P_GQ // hk)),
        pl.BlockSpec((rows, hk), lambda n: (0, P_GK // hk)),
        pl.BlockSpec((rows, GLA_WIDTH), lambda n: (0, P_GV // GLA_WIDTH)),
        pl.BlockSpec((rows, LANES), lambda n: (0, 0)),
        pl.BlockSpec((LANES, hk), lambda n: (0, 0)),
        pl.BlockSpec((1, hk), lambda n: (0, 0)),
    ]
    return pl.pallas_call(
        functools.partial(_gla_body, reverse),
        grid=(nb + 1,),
        in_specs=in_specs,
        out_specs=pl.BlockSpec((rows, GLA_WIDTH), lambda n: (out_blk(n), 0)),
        out_shape=jax.ShapeDtypeStruct((t_len + rows, GLA_WIDTH), F32),
        scratch_shapes=[pltpu.VMEM((GLA_HEADS, GLA_V_DIM, GLA_K_DIM), F32)],
        compiler_params=_cparams(("arbitrary",)),
        name="gla_bwd" if reverse else "gla_fwd",
    )(p, p, p, gd, pc, pc, pc, gdc, w2pad, b2.reshape(1, hk))


def _gla_out_body(of_ref, ob_ref, r_ref, g_ref, y_ref):
    g = g_ref[...]
    for h in range(GLA_HEADS):
        vs = slice(h * GLA_V_DIM, (h + 1) * GLA_V_DIM)
        o = of_ref[:, vs] + ob_ref[:, vs]
        ms = jnp.mean(o * o, axis=-1, keepdims=True)
        r = r_ref[:, vs].astype(F32)
        y_ref[:, vs] = (o * lax.rsqrt(ms + EPS) * g * (r * _sigmoid(r))).astype(y_ref.dtype)


def gla_output(o_f, o_b, p, g, row_block0, n_rows, tm=256):
    r_blk = P_GR // GLA_WIDTH
    o_spec = pl.BlockSpec((tm, GLA_WIDTH), lambda i: (row_block0 + i, 0))
    return pl.pallas_call(
        _gla_out_body,
        grid=(n_rows // tm,),
        in_specs=[o_spec, o_spec, pl.BlockSpec((tm, GLA_WIDTH), lambda i: (i, r_blk)),
                  pl.BlockSpec((1, GLA_V_DIM), lambda i: (0, 0))],
        out_specs=pl.BlockSpec((tm, GLA_WIDTH), lambda i: (i, 0)),
        out_shape=jax.ShapeDtypeStruct((n_rows, GLA_WIDTH), BF16),
        compiler_params=_cparams(("arbitrary",)),
        name="gla_output",
    )(o_f, o_b, p, g.reshape(1, GLA_V_DIM))


def _convgate_body(n_row_tiles, ug_ref, ugp_ref, ugn_ref, uv_ref, uvp_ref, uvn_ref,
                   wg_ref, wv_ref, bg_ref, bv_ref, o_ref):
    i = pl.program_id(0)
    tm = ug_ref.shape[0]
    rows = lax.broadcasted_iota(jnp.int32, (tm, 1), 0)
    has_prev = (i > 0).astype(F32)
    has_next = (i < n_row_tiles - 1).astype(F32)

    def conv(x_ref, p_ref, n_ref, w_ref, b_ref):
        x = x_ref[...].astype(F32)
        prev_row = p_ref[7:8, :].astype(F32) * has_prev
        next_row = n_ref[0:1, :].astype(F32) * has_next
        up = jnp.where(rows == 0, prev_row, pltpu.roll(x, 1, 0))
        dn = jnp.where(rows == tm - 1, next_row, pltpu.roll(x, tm - 1, 0))
        return up * w_ref[0:1, :] + x * w_ref[1:2, :] + dn * w_ref[2:3, :] + b_ref[...]

    gate = conv(ug_ref, ugp_ref, ugn_ref, wg_ref, bg_ref)
    val = conv(uv_ref, uvp_ref, uvn_ref, wv_ref, bv_ref)
    o_ref[...] = (gate * _sigmoid(gate) * val).astype(o_ref.dtype)


def conv_gate(u, conv_w, conv_b, tm=1024, tc=256):
    m, two_f = u.shape
    f = two_f // 2
    tm = min(tm, m)
    nt = m // tm
    nc = f // tc
    hb = tm // 8
    last8 = m // 8 - 1
    main_g = pl.BlockSpec((tm, tc), lambda i, j: (i, j))
    prev_g = pl.BlockSpec((8, tc), lambda i, j: (jnp.maximum(i * hb - 1, 0), j))
    next_g = pl.BlockSpec((8, tc), lambda i, j: (jnp.minimum((i + 1) * hb, last8), j))
    main_v = pl.BlockSpec((tm, tc), lambda i, j: (i, j + nc))
    prev_v = pl.BlockSpec((8, tc), lambda i, j: (jnp.maximum(i * hb - 1, 0), j + nc))
    next_v = pl.BlockSpec((8, tc), lambda i, j: (jnp.minimum((i + 1) * hb, last8), j + nc))
    w_g = pl.BlockSpec((3, tc), lambda i, j: (0, j))
    w_v = pl.BlockSpec((3, tc), lambda i, j: (0, j + nc))
    b_g = pl.BlockSpec((1, tc), lambda i, j: (0, j))
    b_v = pl.BlockSpec((1, tc), lambda i, j: (0, j + nc))
    cb = conv_b.reshape(1, two_f)
    return pl.pallas_call(
        functools.partial(_convgate_body, nt),
        grid=(nt, nc),
        in_specs=[main_g, prev_g, next_g, main_v, prev_v, next_v, w_g, w_v, b_g, b_v],
        out_specs=pl.BlockSpec((tm, tc), lambda i, j: (i, j)),
        out_shape=jax.ShapeDtypeStruct((m, f), BF16),
        compiler_params=_cparams(("arbitrary", "arbitrary")),
        name="conv_gate",
    )(u, u, u, u, u, u, conv_w, conv_w, cb, cb)


def _rope_tables(t_len):
    rows = t_len // GRID_W
    row = jnp.repeat(jnp.arange(rows), GRID_W).astype(F32)
    col = jnp.tile(jnp.arange(GRID_W), rows).astype(F32)
    half = DA_HEAD_DIM // 2
    freqs = ROPE_BASE ** (-jnp.arange(0, half, 2, dtype=F32) / half)
    ar = row[:, None] * freqs
    ac = col[:, None] * freqs
    ang = jnp.concatenate([ar, ar, ac, ac], axis=-1)
    cos, sin = jnp.cos(ang), jnp.sin(ang)
    hi = (jnp.arange(DA_HEAD_DIM) % 64) >= 32
    return cos, jnp.where(hi, sin, 0.0), jnp.where(hi, 0.0, -sin)


def _ffn(x, h, w_up, conv_w, conv_b, w_down, gate):
    u = matmul(h, w_up, tm=1024, tn=512, name="ffn_up")
    act = conv_gate(u, conv_w, conv_b)
    return matmul_ksplit_res(act, w_down, x, gate, tm=512, tn=1024, tk=D_FF // 2, name="ffn_down")


def kernel(x, c, ctx, c_ctx, w_ada, b_ada, norm1_g, norm2_g, w_in, q_norm_g, k_norm_g, lambda_q1, lambda_k1,
           lambda_q2, lambda_k2, da_subln_g, gla_gate_w2, gla_gate_b, gla_norm_g, w_out, w_up, conv_w, conv_b,
           w_down):
    depth = w_ada.shape[0]
    t_len = x.shape[1]
    d = D_MODEL
    xl = x[0]
    xc = ctx[0]
    rope = _rope_tables(t_len)

    cvec = jnp.zeros((8, d), F32).at[0].set(c[0]).at[1].set(c_ctx)
    mod = adaln(cvec, w_ada, b_ada)

    for l in range(depth):
        need_ctx = l < depth - 1
        lam_init = 0.8 - 0.6 * math.exp(-0.3 * l)
        mods_l = [mod[l, 0, i * d:(i + 1) * d] for i in range(N_MOD)]
        mods_c = [mod[l, 1, i * d:(i + 1) * d] for i in range(N_MOD)]

        wi = w_in[l]
        w_pack = jnp.concatenate([wi[:, 0:6144], wi[:, 7168:9216], wi[:, 9248:10272]], axis=1).astype(BF16)
        w_ft = wi[:, 6144:7168].astype(BF16)
        w_gd = jnp.pad(wi[:, 9216:9248], ((0, 0), (0, LANES - 2 * GLA_GATE_RANK))).astype(BF16)

        h_l = modulate(xl, norm1_g[l], mods_l[0], mods_l[1])
        h_c = modulate(xc, norm1_g[l], mods_c[0], mods_c[1])
        p_l = matmul(h_l, w_pack, tm=1024, tn=1024, name="in_proj")
        p_c = matmul(h_c, w_pack, tm=1024, tn=1024, name="in_proj_ctx")
        z_l = matmul(h_l, w_ft, tm=1024, tn=1024, name="in_proj_ft")
        gd_l = matmul(h_l, w_gd, tm=1024, tn=LANES, name="in_proj_gd")
        gd_c = matmul(h_c, w_gd, tm=1024, tn=LANES, name="in_proj_gd_ctx")

        qt_l, k_l, vt_l = qk_prep(p_l, q_norm_g[l], k_norm_g[l], rope, ATTN_TK)
        qt_c, k_c, vt_c = qk_prep(p_c, q_norm_g[l], k_norm_g[l], None, xc.shape[0])
        lams = (lambda_q1[l], lambda_k1[l], lambda_q2[l], lambda_k2[l])
        da_l = diff_attention(qt_l, k_l, vt_l, k_c, vt_c, lams, da_subln_g[l], lam_init, tq=256)

        ft_l = fourier_mix_latent(z_l)

        w2 = gla_gate_w2[l]
        zpad = jnp.zeros((LANES - 2 * GLA_GATE_RANK, w2.shape[-1]), F32)
        zr = jnp.zeros((GLA_GATE_RANK, w2.shape[-1]), F32)
        w2_f = jnp.concatenate([w2[0], zr, zpad], axis=0).astype(BF16)
        w2_b = jnp.concatenate([zr, w2[1], zpad], axis=0).astype(BF16)
        o_f = gla_direction(p_l, gd_l, p_c, gd_c, w2_f, gla_gate_b[l, 0], False)
        o_b = gla_direction(p_l, gd_l, p_c, gd_c, w2_b, gla_gate_b[l, 1], True)
        gla_l = gla_output(o_f, o_b, p_l, gla_norm_g[l], 0, t_len)

        wo = w_out[l].astype(BF16)
        wu = w_up[l].astype(BF16)
        wd = w_down[l].astype(BF16)

        mix_l = jnp.concatenate([da_l, ft_l, gla_l], axis=-1)
        x_new = matmul(mix_l, wo, tm=1024, tn=512, res=xl, gate=mods_l[2], name="out_proj")
        h2_l = modulate(x_new, norm2_g[l], mods_l[3], mods_l[4])
        x_new = _ffn(x_new, h2_l, wu, conv_w[l], conv_b[l], wd, mods_l[5])

        if need_ctx:
            z_c = matmul(h_c, w_ft, tm=1024, tn=1024, name="in_proj_ft_ctx")
            da_c = diff_attention(qt_c, k_c, vt_c, None, None, lams, da_subln_g[l], lam_init, tq=256)
            ft_c = fourier_mix_short(z_c)
            gla_c = gla_output(o_f, o_b, p_c, gla_norm_g[l], t_len // 256, xc.shape[0])
            mix_c = jnp.concatenate([da_c, ft_c, gla_c], axis=-1)
            xc_new = matmul(mix_c, wo, tm=1024, tn=512, res=xc, gate=mods_c[2], name="out_proj_ctx")
            h2_c = modulate(xc_new, norm2_g[l], mods_c[3], mods_c[4])
            xc = _ffn(xc_new, h2_c, wu, conv_w[l], conv_b[l], wd, mods_c[5])
        xl = x_new

    return xl[None]
```

```python
import functools
import math

import numpy as np
import jax
import jax.numpy as jnp
from jax import lax
from jax.experimental import pallas as pl
from jax.experimental.pallas import tpu as pltpu

F32 = jnp.float32
BF16 = jnp.bfloat16

D_MODEL = 4096
CTX_LEN = 256
GRID_W = 64
EPS = 1e-6
ROPE_BASE = 10000.0
DA_HEAD_DIM = 128
DA_HEADS = 8
DA_V_DIM = 256
DA_WIDTH = DA_HEADS * DA_V_DIM
FT_GROUPS = 4
FT_GROUP_DIM = 256
FT_WIDTH = FT_GROUPS * FT_GROUP_DIM
GLA_HEADS = 4
GLA_V_DIM = 256
GLA_K_DIM = 128
GLA_WIDTH = GLA_HEADS * GLA_V_DIM
GLA_GATE_RANK = 16
GLA_TAU = 16.0
GLA_CHUNK = 64
D_FF = 11008
D_FF_PAD = 11264
N_MOD = 6

W_FT, W_GQ, W_GD, W_GR = 6144, 7168, 9216, 9248
P_Q, P_K, P_V = 0, 2048, 4096
P_GQ, P_GK, P_GV = 6144, 6656, 7168
P_WIDTH = 8192

V7X_VMEM_LIMIT_BYTES = 56 * 1024 * 1024
LANES = 128
FFT_N2 = 128
GLA_ROWS = 256
ATTN_TK = 512
Q_SCALE = DA_HEAD_DIM ** -0.5 * math.log2(math.e)


def _cparams(sem):
    return pltpu.CompilerParams(dimension_semantics=sem, vmem_limit_bytes=V7X_VMEM_LIMIT_BYTES)


def _sigmoid(x):
    return 1.0 / (1.0 + jnp.exp(-x))


def _adaln_body(c_ref, w_ref, b_ref, o_ref):
    c = c_ref[...]
    s = (c * _sigmoid(c)).astype(BF16)
    w = w_ref[...].astype(BF16)
    o_ref[...] = jnp.dot(s, w, preferred_element_type=F32) + b_ref[...]


def adaln(cvec, w_ada, b_ada, tn=512):
    depth, d, n = w_ada.shape
    return pl.pallas_call(
        _adaln_body,
        grid=(depth, n // tn),
        in_specs=[
            pl.BlockSpec((8, d), lambda l, j: (0, 0)),
            pl.BlockSpec((None, d, tn), lambda l, j: (l, 0, j)),
            pl.BlockSpec((None, 1, tn), lambda l, j: (l, 0, j)),
        ],
        out_specs=pl.BlockSpec((None, 8, tn), lambda l, j: (l, 0, j)),
        out_shape=jax.ShapeDtypeStruct((depth, 8, n), F32),
        compiler_params=_cparams(("arbitrary", "arbitrary")),
        name="adaln",
    )(cvec, w_ada, b_ada.reshape(depth, 1, n))


def _modulate_body(x_ref, g_ref, sh_ref, sc_ref, o_ref):
    x = x_ref[...]
    ms = jnp.mean(x * x, axis=-1, keepdims=True)
    y = x * lax.rsqrt(ms + EPS) * g_ref[...]
    o_ref[...] = (y * (1.0 + sc_ref[...]) + sh_ref[...]).astype(o_ref.dtype)


def modulate(x, g, shift, scale, tm=256):
    m, d = x.shape
    vec = pl.BlockSpec((1, d), lambda i: (0, 0))
    return pl.pallas_call(
        _modulate_body,
        grid=(m // tm,),
        in_specs=[pl.BlockSpec((tm, d), lambda i: (i, 0)), vec, vec, vec],
        out_specs=pl.BlockSpec((tm, d), lambda i: (i, 0)),
        out_shape=jax.ShapeDtypeStruct((m, d), BF16),
        compiler_params=_cparams(("arbitrary",)),
        name="modulate",
    )(x, g.reshape(1, d), shift.reshape(1, d), scale.reshape(1, d))


def _mm_body(a_ref, b_ref, o_ref):
    o_ref[...] = jnp.dot(a_ref[...], b_ref[...].astype(BF16), preferred_element_type=F32).astype(o_ref.dtype)


def _mm_res_body(a_ref, b_ref, r_ref, g_ref, o_ref):
    acc = jnp.dot(a_ref[...], b_ref[...].astype(BF16), preferred_element_type=F32)
    o_ref[...] = r_ref[...] + g_ref[...] * acc


def matmul(a, b, *, tm, tn, out_dtype=BF16, res=None, gate=None, layer=None, n_out=None, col_map=None,
           name="matmul"):
    m, k = a.shape
    n = n_out if n_out is not None else b.shape[-1]
    tm = min(tm, m)
    cm = col_map if col_map is not None else (lambda j: j)
    if b.ndim == 3:
        b_spec = pl.BlockSpec((None, k, tn), lambda i, j: (layer, 0, cm(j)))
    else:
        b_spec = pl.BlockSpec((k, tn), lambda i, j: (0, cm(j)))
    in_specs = [pl.BlockSpec((tm, k), lambda i, j: (i, 0), pipeline_mode=pl.Buffered(1)), b_spec]
    args = [a, b]
    body = _mm_body
    if res is not None:
        in_specs += [pl.BlockSpec((tm, tn), lambda i, j: (i, j)), pl.BlockSpec((1, tn), lambda i, j: (0, j))]
        args += [res, gate.reshape(1, n)]
        body = _mm_res_body
        out_dtype = F32
    return pl.pallas_call(
        body,
        grid=(m // tm, n // tn),
        in_specs=in_specs,
        out_specs=pl.BlockSpec((tm, tn), lambda i, j: (i, j)),
        out_shape=jax.ShapeDtypeStruct((m, n), out_dtype),
        compiler_params=_cparams(("parallel", "arbitrary")),
        name=name,
    )(*args)


def _mm_ksplit_res_body(a_ref, b_ref, r_ref, g_ref, o_ref, acc_ref):
    kk = pl.program_id(2)

    @pl.when(kk == 0)
    def _():
        acc_ref[...] = jnp.zeros_like(acc_ref)

    acc_ref[...] += jnp.dot(a_ref[...], b_ref[...], preferred_element_type=F32)

    @pl.when(kk == pl.num_programs(2) - 1)
    def _():
        o_ref[...] = r_ref[...] + g_ref[...] * acc_ref[...]


def matmul_ksplit_res(a, b, res, gate, *, tm, tn, tk, name="matmul_ksplit"):
    m, k = a.shape
    n = b.shape[1]
    tm = min(tm, m)
    return pl.pallas_call(
        _mm_ksplit_res_body,
        grid=(m // tm, n // tn, k // tk),
        in_specs=[
            pl.BlockSpec((tm, tk), lambda i, j, kk: (i, kk)),
            pl.BlockSpec((tk, tn), lambda i, j, kk: (kk, j)),
            pl.BlockSpec((tm, tn), lambda i, j, kk: (i, j)),
            pl.BlockSpec((1, tn), lambda i, j, kk: (0, j)),
        ],
        out_specs=pl.BlockSpec((tm, tn), lambda i, j, kk: (i, j)),
        out_shape=jax.ShapeDtypeStruct((m, n), F32),
        scratch_shapes=[pltpu.VMEM((tm, tn), F32)],
        compiler_params=_cparams(("parallel", "arbitrary", "arbitrary")),
        name=name,
    )(a, b, res, gate.reshape(1, n))


def _qk_prep_body(use_rope, p_ref, gq_ref, gk_ref, cos_ref, sa_ref, sb_ref, qt_ref, k_ref, vt_ref):
    n_groups = (2 * DA_HEADS * DA_HEAD_DIM) // LANES
    if use_rope:
        cos, sa, sb = cos_ref[...], sa_ref[...], sb_ref[...]

    def prep(col, gain, scale):
        x = p_ref[:, col:col + LANES].astype(F32)
        ms = jnp.mean(x * x, axis=-1, keepdims=True)
        y = x * lax.rsqrt(ms + EPS) * gain
        if use_rope:
            y = y * cos + pltpu.roll(y, 32, 1) * sa + pltpu.roll(y, 96, 1) * sb
        return y * scale if scale != 1.0 else y

    gq, gk = gq_ref[...], gk_ref[...]
    for g in range(n_groups):
        qt_ref[g * LANES:(g + 1) * LANES, :] = prep(P_Q + g * LANES, gq, Q_SCALE).T.astype(qt_ref.dtype)
        k_ref[:, g * LANES:(g + 1) * LANES] = prep(P_K + g * LANES, gk, 1.0).astype(k_ref.dtype)
    for h in range(DA_HEADS):
        v = p_ref[:, P_V + h * DA_V_DIM:P_V + (h + 1) * DA_V_DIM].astype(F32)
        vt_ref[h] = v.T.astype(vt_ref.dtype)


def qk_prep(p, gq, gk, rope, tm):
    rows = p.shape[0]
    use_rope = rope is not None
    width = 2 * DA_HEADS * DA_HEAD_DIM
    vec = pl.BlockSpec((1, LANES), lambda i: (0, 0))
    tab = pl.BlockSpec((tm, LANES), lambda i: (i, 0))
    if use_rope:
        tabs = list(rope)
    else:
        tabs = [jnp.zeros((rows, LANES), F32)] * 3
    return pl.pallas_call(
        functools.partial(_qk_prep_body, use_rope),
        grid=(rows // tm,),
        in_specs=[pl.BlockSpec((tm, P_V + DA_WIDTH), lambda i: (i, 0)), vec, vec, tab, tab, tab],
        out_specs=[pl.BlockSpec((width, tm), lambda i: (0, i)),
                   pl.BlockSpec((tm, width), lambda i: (i, 0)),
                   pl.BlockSpec((DA_HEADS, None, DA_V_DIM, tm), lambda i: (0, i, 0, 0))],
        out_shape=[jax.ShapeDtypeStruct((width, rows), BF16),
                   jax.ShapeDtypeStruct((rows, width), BF16),
                   jax.ShapeDtypeStruct((DA_HEADS, rows // tm, DA_V_DIM, tm), BF16)],
        compiler_params=_cparams(("arbitrary",)),
        name="qk_prep",
    )(p, gq.reshape(1, LANES), gk.reshape(1, LANES), *tabs)


def _attn_body(lam_init, n_chunks, tk, has_ctx, *refs):
    if has_ctx:
        (qt_ref, k_ref, vt_ref, kc_ref, vtc_ref, lq1_ref, lk1_ref, lq2_ref, lk2_ref, g_ref,
         o_ref, m_ref, l_ref, acc_ref) = refs
    else:
        (qt_ref, k_ref, vt_ref, lq1_ref, lk1_ref, lq2_ref, lk2_ref, g_ref,
         o_ref, m_ref, l_ref, acc_ref) = refs
    d = DA_HEAD_DIM
    qt = (qt_ref[:d, :], qt_ref[d:, :])
    m_ref[...] = jnp.full_like(m_ref, -jnp.inf)
    l_ref[...] = jnp.zeros_like(l_ref)
    acc_ref[...] = jnp.zeros_like(acc_ref)

    def update(k_blk, vt_blk):
        for mi in range(2):
            s = jnp.dot(k_blk[:, mi * d:(mi + 1) * d], qt[mi], preferred_element_type=F32)
            m_old = m_ref[mi]
            m_new = jnp.maximum(m_old, jnp.max(s, axis=0, keepdims=True))
            alpha = jnp.exp2(m_old - m_new)
            p = jnp.exp2(s - m_new)
            l_ref[mi] = alpha * l_ref[mi] + jnp.sum(p, axis=0, keepdims=True)
            acc_ref[mi] = alpha * acc_ref[mi] + jnp.dot(vt_blk, p.astype(BF16), preferred_element_type=F32)
            m_ref[mi] = m_new

    if has_ctx:
        update(kc_ref[...], vtc_ref[...])

    def step(c, carry):
        off = pl.multiple_of(c * tk, tk)
        update(k_ref[pl.ds(off, tk), :], vt_ref[c])
        return carry

    lax.fori_loop(0, n_chunks, step, 0, unroll=math.gcd(n_chunks, 8))

    lam =(jnp.exp(jnp.sum(lq1_ref[...] * lk1_ref[...])) - jnp.exp(jnp.sum(lq2_ref[...] * lk2_ref[...]))
           + lam_init)
    ot = acc_ref[0] * (1.0 / l_ref[0]) - lam * (acc_ref[1] * (1.0 / l_ref[1]))
    ms = jnp.mean(ot * ot, axis=0, keepdims=True)
    o = (ot * lax.rsqrt(ms + EPS)).T
    o_ref[...] = (o * g_ref[...] * (1.0 - lam_init)).astype(o_ref.dtype)


def diff_attention(qt, k, vt, kc, vtc, lams, subln_g, lam_init, *, tq):
    t_q = qt.shape[1]
    t_k = k.shape[0]
    n_chunks, tk = vt.shape[1], vt.shape[3]
    has_ctx = kc is not None
    tq = min(tq, t_q)
    hw = 2 * DA_HEAD_DIM
    in_specs = [
        pl.BlockSpec((hw, tq), lambda h, i: (h, i)),
        pl.BlockSpec((t_k, hw), lambda h, i: (0, h)),
        pl.BlockSpec((None, n_chunks, DA_V_DIM, tk), lambda h, i: (h, 0, 0, 0)),
    ]
    args = [qt, k, vt]
    if has_ctx:
        t_c = kc.shape[0]
        in_specs += [pl.BlockSpec((t_c, hw), lambda h, i: (0, h)),
                     pl.BlockSpec((None, None, DA_V_DIM, t_c), lambda h, i: (h, 0, 0, 0))]
        args += [kc, vtc]
    vec = pl.BlockSpec((1, DA_HEAD_DIM), lambda h, i: (0, 0))
    in_specs += [vec] * 4 + [pl.BlockSpec((1, DA_V_DIM), lambda h, i: (0, 0))]
    args += [v.reshape(1, DA_HEAD_DIM) for v in lams] + [subln_g.reshape(1, DA_V_DIM)]
    return pl.pallas_call(
        functools.partial(_attn_body, lam_init, n_chunks, tk, has_ctx),
        grid=(DA_HEADS, t_q // tq),
        in_specs=in_specs,
        out_specs=pl.BlockSpec((tq, DA_V_DIM), lambda h, i: (i, h)),
        out_shape=jax.ShapeDtypeStruct((t_q, DA_WIDTH), BF16),
        scratch_shapes=[pltpu.VMEM((2, 1, tq), F32), pltpu.VMEM((2, 1, tq), F32),
                        pltpu.VMEM((2, DA_V_DIM, tq), F32)],
        compiler_params=_cparams(("parallel", "arbitrary")),
        name="diff_attention",
    )(*args)


def _dft_tables_latent(t_len):
    n2 = FFT_N2
    n1 = t_len // n2
    f1 = np.arange(n1)[:, None]
    t1 = np.arange(n1)[None, :]
    ang = 2.0 * np.pi * ((f1 * t1) % n1) / n1
    stage_a = np.concatenate([np.cos(ang), np.sin(ang)], axis=0)
    f2 = np.arange(n2)[None, :, None]
    t2 = np.arange(n2)[None, None, :]
    f1b = np.arange(n1)[:, None, None]
    idx = (t2 * f2 * n1 + t2 * f1b) % t_len
    th = 2.0 * np.pi * idx / t_len
    mr, mi = np.cos(th), -np.sin(th)
    stage_b = np.concatenate([np.concatenate([mr, mi], axis=2), np.concatenate([mi, -mr], axis=2)], axis=1)
    return stage_a.astype(np.float32), stage_b.astype(np.float32)


def _dft_table_channels(t_len, sign):
    c = np.arange(FT_GROUP_DIM)
    ang = 2.0 * np.pi * ((c[:, None] * c[None, :]) % FT_GROUP_DIM) / FT_GROUP_DIM
    scale = 1.0 / math.sqrt(t_len * FT_GROUP_DIM)
    return (np.concatenate([np.cos(ang), sign * np.sin(ang)], axis=0) * scale).astype(np.float32)


def _fft_a_body(cs_ref, z_ref, p_ref, q_ref):
    n1 = z_ref.shape[0]
    r = jnp.dot(cs_ref[...], z_ref[...], preferred_element_type=F32)
    p_ref[...] = r[:n1].astype(p_ref.dtype)
    q_ref[...] = r[n1:].astype(q_ref.dtype)


def _fft_b_body(m_ref, p_ref, q_ref, ch_ref, o_ref):
    n2 = FFT_N2
    pq = jnp.concatenate([p_ref[...], q_ref[...]], axis=0)
    r = jnp.dot(m_ref[...], pq, preferred_element_type=F32)
    ch = ch_ref[...]
    for g in range(FT_GROUPS):
        c0 = g * FT_GROUP_DIM
        rg = jnp.concatenate([r[:n2, c0:c0 + FT_GROUP_DIM], r[n2:, c0:c0 + FT_GROUP_DIM]], axis=1).astype(BF16)
        o_ref[:, c0:c0 + FT_GROUP_DIM] = jnp.dot(rg, ch, preferred_element_type=F32).astype(o_ref.dtype)


def fourier_mix_latent(z, tn=8192):
    t_len, width = z.shape
    n2 = FFT_N2
    n1 = t_len // n2
    ta, tb = _dft_tables_latent(t_len)
    cs = jnp.asarray(ta).astype(BF16)
    mt = jnp.asarray(tb).astype(BF16)
    ch = jnp.asarray(_dft_table_channels(t_len, 1.0)).astype(BF16)
    cols = n2 * width
    tn = min(tn, cols)
    zv = z.reshape(n1, cols)
    pv, qv = pl.pallas_call(
        _fft_a_body,
        grid=(cols // tn,),
        in_specs=[pl.BlockSpec((2 * n1, n1), lambda j: (0, 0)), pl.BlockSpec((n1, tn), lambda j: (0, j))],
        out_specs=[pl.BlockSpec((n1, tn), lambda j: (0, j))] * 2,
        out_shape=[jax.ShapeDtypeStruct((n1, cols), BF16)] * 2,
        compiler_params=_cparams(("arbitrary",)),
        name="fft_stage_a",
    )(cs, zv)
    pm = pv.reshape(t_len, width)
    qm = qv.reshape(t_len, width)
    out = pl.pallas_call(
        _fft_b_body,
        grid=(n1,),
        in_specs=[
            pl.BlockSpec((None, 2 * n2, 2 * n2), lambda f: (f, 0, 0)),
            pl.BlockSpec((n2, width), lambda f: (f, 0)),
            pl.BlockSpec((n2, width), lambda f: (f, 0)),
            pl.BlockSpec((2 * FT_GROUP_DIM, FT_GROUP_DIM), lambda f: (0, 0)),
        ],
        out_specs=pl.BlockSpec((n2, width), lambda f: (0, f)),
        out_shape=jax.ShapeDtypeStruct((n2, n1 * width), BF16),
        compiler_params=_cparams(("arbitrary",)),
        name="fft_stage_b",
    )(mt, pm, qm, ch)
    return out.reshape(t_len, width)


def _fft_ctx_body(cs_ref, z_ref, ch_ref, o_ref):
    t_len = z_ref.shape[0]
    r = jnp.dot(cs_ref[...], z_ref[...], preferred_element_type=F32)
    ch = ch_ref[...]
    for g in range(FT_GROUPS):
        c0 = g * FT_GROUP_DIM
        rg = jnp.concatenate([r[:t_len, c0:c0 + FT_GROUP_DIM], r[t_len:, c0:c0 + FT_GROUP_DIM]], axis=1).astype(BF16)
        o_ref[:, c0:c0 + FT_GROUP_DIM] = jnp.dot(rg, ch, preferred_element_type=F32).astype(o_ref.dtype)


def fourier_mix_short(z):
    t_len, width = z.shape
    f = np.arange(t_len)
    ang = 2.0 * np.pi * ((f[:, None] * f[None, :]) % t_len) / t_len
    cs = jnp.asarray(np.concatenate([np.cos(ang), np.sin(ang)], axis=0).astype(np.float32)).astype(BF16)
    ch = jnp.asarray(_dft_table_channels(t_len, -1.0)).astype(BF16)
    return pl.pallas_call(
        _fft_ctx_body,
        out_shape=jax.ShapeDtypeStruct((t_len, width), BF16),
        compiler_params=pltpu.CompilerParams(vmem_limit_bytes=V7X_VMEM_LIMIT_BYTES),
        name="fft_short",
    )(cs, z, ch)


def _split3(x):
    hi = x.astype(BF16)
    r1 = x - hi.astype(F32)
    mid = r1.astype(BF16)
    lo = (r1 - mid.astype(F32)).astype(BF16)
    return hi, mid, lo


def _gla_body(reverse, q_ref, k_ref, v_ref, gd_ref, qc_ref, kc_ref, vc_ref, gdc_ref, w2_ref, b2_ref,
              o_ref, st_ref):
    n = pl.program_id(0)
    rows = GLA_ROWS
    ch = GLA_CHUNK
    n_chunks = rows // ch
    dk, dv = GLA_K_DIM, GLA_V_DIM

    @pl.when(n == 0)
    def _():
        st_ref[...] = jnp.zeros_like(st_ref)

    is_ctx = n == 0
    q = jnp.where(is_ctx, qc_ref[...], q_ref[...])
    k = jnp.where(is_ctx, kc_ref[...], k_ref[...])
    v = jnp.where(is_ctx, vc_ref[...], v_ref[...])
    gd = jnp.where(is_ctx, gdc_ref[...], gd_ref[...])

    logits = jnp.dot(gd, w2_ref[...], preferred_element_type=F32) + b2_ref[...]
    log_a = (jnp.minimum(logits, 0.0) - jnp.log(1.0 + jnp.exp(-jnp.abs(logits)))) / GLA_TAU

    ri = lax.broadcasted_iota(jnp.int32, (rows, rows), 0)
    ci = lax.broadcasted_iota(jnp.int32, (rows, rows), 1)
    same = (ri // ch) == (ci // ch)
    tri = jnp.logical_and(same, (ci >= ri) if reverse else (ci <= ri))
    tri_b = jnp.where(tri, 1.0, 0.0).astype(BF16)
    hi, mid, lo = _split3(log_a)
    cum = (jnp.dot(tri_b, hi, preferred_element_type=F32) + jnp.dot(tri_b, mid, preferred_element_type=F32)
           + jnp.dot(tri_b, lo, preferred_element_type=F32))
    e_pos = jnp.exp(cum)
    e_neg = jnp.exp(-cum)
    order = list(range(n_chunks))[::-1] if reverse else list(range(n_chunks))
    last_row = [(c * ch) if reverse else (c * ch + ch - 1) for c in range(n_chunks)]
    scale = dk ** -0.5

    for h in range(GLA_HEADS):
        ks = slice(h * dk, (h + 1) * dk)
        vs = slice(h * dv, (h + 1) * dv)
        qf = q[:, ks].astype(F32) * scale
        kf = k[:, ks].astype(F32)
        vh = v[:, vs]
        q_in = (qf * e_pos[:, ks]).astype(BF16)
        k_in = (kf * e_neg[:, ks]).astype(BF16)
        a = lax.dot_general(q_in, k_in, (((1,), (1,)), ((), ())), preferred_element_type=F32)
        a = jnp.where(tri, a, 0.0).astype(BF16)
        o_intra = jnp.dot(a, vh, preferred_element_type=F32)
        st = st_ref[h]
        for c in order:
            rs = slice(c * ch, (c + 1) * ch)
            cum_last = cum[last_row[c]:last_row[c] + 1, ks]
            o_inter = lax.dot_general(q_in[rs], st.astype(BF16), (((1,), (1,)), ((), ())),
                                      preferred_element_type=F32)
            o_ref[rs, vs] = o_intra[rs] + o_inter
            k_dec = (kf[rs] * jnp.exp(cum_last - cum[rs, ks])).astype(BF16)
            kv_t = lax.dot_general(vh[rs], k_dec, (((0,), (0,)), ((), ())), preferred_element_type=F32)
            st = jnp.exp(cum_last) * st + kv_t
        st_ref[h] = st


def gla_direction(p, gd, pc, gdc, w2pad, b2, reverse):
    t_len = p.shape[0]
    rows = GLA_ROWS
    nb = t_len // rows
    hk = GLA_HEADS * GLA_K_DIM

    if reverse:
        lat = lambda n: jnp.where(n == 0, nb - 1, nb - n)
    else:
        lat = lambda n: jnp.where(n == 0, 0, n - 1)
    out_blk = lambda n: jnp.where(n == 0, nb, lat(n))
    in_specs = [
        pl.BlockSpec((rows, hk), lambda n: (lat(n), P_GQ // hk)),
        pl.BlockSpec((rows, hk), lambda n: (lat(n), P_GK // hk)),
        pl.BlockSpec((rows, GLA_WIDTH), lambda n: (lat(n), P_GV // GLA_WIDTH)),
        pl.BlockSpec((rows, LANES), lambda n: (lat(n), 0)),
        pl.BlockSpec((rows, hk), lambda n: (0, P_GQ // hk)),
        pl.BlockSpec((rows, hk), lambda n: (0, P_GK // hk)),
        pl.BlockSpec((rows, GLA_WIDTH), lambda n: (0, P_GV // GLA_WIDTH)),
        pl.BlockSpec((rows, LANES), lambda n: (0, 0)),
        pl.BlockSpec((LANES, hk), lambda n: (0, 0)),
        pl.BlockSpec((1, hk), lambda n: (0, 0)),
    ]
    return pl.pallas_call(
        functools.partial(_gla_body, reverse),
        grid=(nb + 1,),
        in_specs=in_specs,
        out_specs=pl.BlockSpec((rows, GLA_WIDTH), lambda n: (out_blk(n), 0)),
        out_shape=jax.ShapeDtypeStruct((t_len + rows, GLA_WIDTH), F32),
        scratch_shapes=[pltpu.VMEM((GLA_HEADS, GLA_V_DIM, GLA_K_DIM), F32)],
        compiler_params=_cparams(("arbitrary",)),
        name="gla_bwd" if reverse else "gla_fwd",
    )(p, p, p, gd, pc, pc, pc, gdc, w2pad, b2.reshape(1, hk))


def _gla_out_body(of_ref, ob_ref, r_ref, g_ref, y_ref):
    g = g_ref[...]
    for h in range(GLA_HEADS):
        vs = slice(h * GLA_V_DIM, (h + 1) * GLA_V_DIM)
        o = of_ref[:, vs] + ob_ref[:, vs]
        ms = jnp.mean(o * o, axis=-1, keepdims=True)
        r = r_ref[:, vs].astype(F32)
        y_ref[:, vs] = (o * lax.rsqrt(ms + EPS) * g * (r * _sigmoid(r))).astype(y_ref.dtype)


def gla_output(o_f, o_b, r, g, row_block0, tm=256):
    n_rows = r.shape[0]
    o_spec = pl.BlockSpec((tm, GLA_WIDTH), lambda i: (row_block0 + i, 0))
    return pl.pallas_call(
        _gla_out_body,
        grid=(n_rows // tm,),
        in_specs=[o_spec, o_spec, pl.BlockSpec((tm, GLA_WIDTH), lambda i: (i, 0)),
                  pl.BlockSpec((1, GLA_V_DIM), lambda i: (0, 0))],
        out_specs=pl.BlockSpec((tm, GLA_WIDTH), lambda i: (i, 0)),
        out_shape=jax.ShapeDtypeStruct((n_rows, GLA_WIDTH), BF16),
        compiler_params=_cparams(("arbitrary",)),
        name="gla_output",
    )(o_f, o_b, r, g.reshape(1, GLA_V_DIM))


FFN_HALO = 16


def _ffn_up_body(n_row_tiles, a_ref, ap_ref, an_ref, wg_ref, wv_ref, cwg_ref, cwv_ref, cbg_ref, cbv_ref,
                 o_ref, ext_ref):
    i = pl.program_id(0)
    tm = a_ref.shape[0]
    hl = FFN_HALO

    @pl.when(pl.program_id(1) == 0)
    def _():
        ext_ref[hl:hl + tm, :] = a_ref[...]
        zero = jnp.zeros_like(ap_ref)
        ext_ref[0:hl, :] = jnp.where(i > 0, ap_ref[...], zero)
        ext_ref[hl + tm:, :] = jnp.where(i < n_row_tiles - 1, an_ref[...], zero)

    ext = ext_ref[...]

    def conv(w_ref, cw_ref, cb_ref):
        u = jnp.dot(ext, w_ref[...], preferred_element_type=F32)
        c = (pltpu.roll(u, 1, 0) * cw_ref[0:1, :] + u * cw_ref[1:2, :]
             + pltpu.roll(u, tm + 2 * hl - 1, 0) * cw_ref[2:3, :])
        return c[hl:hl + tm] + cb_ref[...]

    gate = conv(wg_ref, cwg_ref, cbg_ref)
    val = conv(wv_ref, cwv_ref, cbv_ref)
    o_ref[...] = (gate * _sigmoid(gate) * val).astype(o_ref.dtype)


def ffn_up_gated(h, wg, wv, cwg, cwv, cbg, cbv, *, tm, tn):
    m, k = h.shape
    f = wg.shape[1]
    tm = min(tm, m)
    nt = m // tm
    hb = tm // FFN_HALO
    last = m // FFN_HALO - 1
    w_spec = pl.BlockSpec((k, tn), lambda i, j: (0, j))
    cw_spec = pl.BlockSpec((3, tn), lambda i, j: (0, j))
    cb_spec = pl.BlockSpec((1, tn), lambda i, j: (0, j))
    return pl.pallas_call(
        functools.partial(_ffn_up_body, nt),
        grid=(nt, f // tn),
        in_specs=[
            pl.BlockSpec((tm, k), lambda i, j: (i, 0), pipeline_mode=pl.Buffered(1)),
            pl.BlockSpec((FFN_HALO, k), lambda i, j: (jnp.maximum(i * hb - 1, 0), 0)),
            pl.BlockSpec((FFN_HALO, k), lambda i, j: (jnp.minimum((i + 1) * hb, last), 0)),
            w_spec, w_spec, cw_spec, cw_spec, cb_spec, cb_spec,
        ],
        out_specs=pl.BlockSpec((tm, tn), lambda i, j: (i, j)),
        out_shape=jax.ShapeDtypeStruct((m, f), BF16),
        scratch_shapes=[pltpu.VMEM((tm + 2 * FFN_HALO, k), BF16)],
        compiler_params=_cparams(("parallel", "arbitrary")),
        name="ffn_up_gated",
    )(h, h, h, wg, wv, cwg, cwv, cbg, cbv)


def _rope_tables(t_len):
    rows = t_len // GRID_W
    row = jnp.repeat(jnp.arange(rows), GRID_W).astype(F32)
    col = jnp.tile(jnp.arange(GRID_W), rows).astype(F32)
    half = DA_HEAD_DIM // 2
    freqs = ROPE_BASE ** (-jnp.arange(0, half, 2, dtype=F32) / half)
    ar = row[:, None] * freqs
    ac = col[:, None] * freqs
    ang = jnp.concatenate([ar, ar, ac, ac], axis=-1)
    cos, sin = jnp.cos(ang), jnp.sin(ang)
    hi = (jnp.arange(DA_HEAD_DIM) % 64) >= 32
    return cos, jnp.where(hi, sin, 0.0), jnp.where(hi, 0.0, -sin)


def _ffn_weights(w_up, conv_w, conv_b, w_down):
    pad = D_FF_PAD - D_FF
    cols = lambda z: jnp.pad(z, ((0, 0), (0, pad)))
    cb = conv_b.reshape(1, 2 * D_FF)
    return dict(
        wg=cols(w_up[:, :D_FF]).astype(BF16), wv=cols(w_up[:, D_FF:]).astype(BF16),
        cwg=cols(conv_w[:, :D_FF]), cwv=cols(conv_w[:, D_FF:]),
        cbg=cols(cb[:, :D_FF]), cbv=cols(cb[:, D_FF:]),
        wd=jnp.pad(w_down, ((0, pad), (0, 0))).astype(BF16))


def _ffn(x, h, fw, gate):
    act = ffn_up_gated(h, fw["wg"], fw["wv"], fw["cwg"], fw["cwv"], fw["cbg"], fw["cbv"], tm=1024, tn=512)
    return matmul_ksplit_res(act, fw["wd"], x, gate, tm=512, tn=1024, tk=D_FF_PAD // 2, name="ffn_down")


def _in_proj(h, w_in, l, w_r, w_gd, suffix):
    skip_ft = (W_GQ - W_FT) // 512
    p = matmul(h, w_in, layer=l, tm=2048, tn=512, n_out=P_WIDTH,
               col_map=lambda j: jnp.where(j < W_FT // 512, j, j + skip_ft), name="in_proj" + suffix)
    z = matmul(h, w_in, layer=l, tm=2048, tn=512, n_out=FT_WIDTH, col_map=lambda j: j + W_FT // 512,
               name="in_proj_ft" + suffix)
    r = matmul(h, w_r, tm=2048, tn=512, name="in_proj_r" + suffix)
    gd = matmul(h, w_gd, tm=2048, tn=LANES, name="in_proj_gd" + suffix)
    return p, z, r, gd


def kernel(x, c, ctx, c_ctx, w_ada, b_ada, norm1_g, norm2_g, w_in, q_norm_g, k_norm_g, lambda_q1, lambda_k1,
           lambda_q2, lambda_k2, da_subln_g, gla_gate_w2, gla_gate_b, gla_norm_g, w_out, w_up, conv_w, conv_b,
           w_down):
    depth = w_ada.shape[0]
    t_len = x.shape[1]
    d = D_MODEL
    xl = x[0]
    xc = ctx[0]
    rope = _rope_tables(t_len)

    cvec = jnp.zeros((8, d), F32).at[0].set(c[0]).at[1].set(c_ctx)
    mod = adaln(cvec, w_ada, b_ada)

    for l in range(depth):
        need_ctx = l < depth - 1
        lam_init = 0.8 - 0.6 * math.exp(-0.3 * l)
        mods_l = [mod[l, 0, i * d:(i + 1) * d] for i in range(N_MOD)]
        mods_c = [mod[l, 1, i * d:(i + 1) * d] for i in range(N_MOD)]

        w_r = w_in[l, :, W_GR:W_GR + GLA_WIDTH]
        w_gd = jnp.pad(w_in[l, :, W_GD:W_GR], ((0, 0), (0, LANES - 2 * GLA_GATE_RANK)))

        h_l = modulate(xl, norm1_g[l], mods_l[0], mods_l[1])
        h_c = modulate(xc, norm1_g[l], mods_c[0], mods_c[1])
        p_l, z_l, r_l, gd_l = _in_proj(h_l, w_in, l, w_r, w_gd, "")
        p_c, z_c, r_c, gd_c = _in_proj(h_c, w_in, l, w_r, w_gd, "_ctx")

        qt_l, k_l, vt_l = qk_prep(p_l, q_norm_g[l], k_norm_g[l], rope, ATTN_TK)
        qt_c, k_c, vt_c = qk_prep(p_c, q_norm_g[l], k_norm_g[l], None, xc.shape[0])
        lams = (lambda_q1[l], lambda_k1[l], lambda_q2[l], lambda_k2[l])
        da_l = diff_attention(qt_l, k_l, vt_l, k_c, vt_c, lams, da_subln_g[l], lam_init, tq=256)

        ft_l = fourier_mix_latent(z_l)

        w2 = gla_gate_w2[l]
        zpad = jnp.zeros((LANES - 2 * GLA_GATE_RANK, w2.shape[-1]), F32)
        zr = jnp.zeros((GLA_GATE_RANK, w2.shape[-1]), F32)
        w2_f = jnp.concatenate([w2[0], zr, zpad], axis=0).astype(BF16)
        w2_b = jnp.concatenate([zr, w2[1], zpad], axis=0).astype(BF16)
        o_f = gla_direction(p_l, gd_l, p_c, gd_c, w2_f, gla_gate_b[l, 0], False)
        o_b = gla_direction(p_l, gd_l, p_c, gd_c, w2_b, gla_gate_b[l, 1], True)
        gla_l = gla_output(o_f, o_b, r_l, gla_norm_g[l], 0)

        fw = _ffn_weights(w_up[l], conv_w[l], conv_b[l], w_down[l])

        mix_l = jnp.concatenate([da_l, ft_l, gla_l], axis=-1)
        x_new = matmul(mix_l, w_out, layer=l, tm=1024, tn=512, res=xl, gate=mods_l[2], name="out_proj")
        h2_l = modulate(x_new, norm2_g[l], mods_l[3], mods_l[4])
        x_new = _ffn(x_new, h2_l, fw, mods_l[5])

        if need_ctx:
            da_c = diff_attention(qt_c, k_c, vt_c, None, None, lams, da_subln_g[l], lam_init, tq=256)
            ft_c = fourier_mix_short(z_c)
            gla_c = gla_output(o_f, o_b, r_c, gla_norm_g[l], t_len // 256)
            mix_c = jnp.concatenate([da_c, ft_c, gla_c], axis=-1)
            xc_new = matmul(mix_c, w_out, layer=l, tm=1024, tn=512, res=xc, gate=mods_c[2], name="out_proj_ctx")
            h2_c = modulate(xc_new, norm2_g[l], mods_c[3], mods_c[4])
            xc = _ffn(xc_new, h2_c, fw, mods_c[5])
        xl = x_new

    return xl[None]
```

```python
import functools
import math

import numpy as np
import jax
import jax.numpy as jnp
from jax import lax
from jax.experimental import pallas as pl
from jax.experimental.pallas import tpu as pltpu

F32 = jnp.float32
BF16 = jnp.bfloat16

D_MODEL = 4096
CTX_LEN = 256
GRID_W = 64
EPS = 1e-6
ROPE_BASE = 10000.0
DA_HEAD_DIM = 128
DA_HEADS = 8
DA_V_DIM = 256
DA_WIDTH = DA_HEADS * DA_V_DIM
FT_GROUPS = 4
FT_GROUP_DIM = 256
FT_WIDTH = FT_GROUPS * FT_GROUP_DIM
GLA_HEADS = 4
GLA_V_DIM = 256
GLA_K_DIM = 128
GLA_WIDTH = GLA_HEADS * GLA_V_DIM
GLA_GATE_RANK = 16
GLA_TAU = 16.0
GLA_CHUNK = 64
D_FF = 11008
N_MOD = 6

W_FT, W_GQ, W_GD, W_GR = 6144, 7168, 9216, 9248
P_Q, P_K, P_V = 0, 2048, 4096
P_GQ, P_GK, P_GV = 6144, 6656, 7168
P_WIDTH = 8192

V7X_VMEM_LIMIT_BYTES = 56 * 1024 * 1024
LANES = 128
FFT_N2 = 128
GLA_ROWS = 256
ATTN_TK = 512
Q_SCALE = DA_HEAD_DIM ** -0.5 * math.log2(math.e)


def _cparams(sem):
    return pltpu.CompilerParams(dimension_semantics=sem, vmem_limit_bytes=V7X_VMEM_LIMIT_BYTES)


def _sigmoid(x):
    return 1.0 / (1.0 + jnp.exp(-x))


def _adaln_body(c_ref, w_ref, b_ref, o_ref):
    c = c_ref[...]
    s = (c * _sigmoid(c)).astype(BF16)
    w = w_ref[...].astype(BF16)
    o_ref[...] = jnp.dot(s, w, preferred_element_type=F32) + b_ref[...]


def adaln(cvec, w_ada, b_ada, tn=512):
    depth, d, n = w_ada.shape
    return pl.pallas_call(
        _adaln_body,
        grid=(depth, n // tn),
        in_specs=[
            pl.BlockSpec((8, d), lambda l, j: (0, 0)),
            pl.BlockSpec((None, d, tn), lambda l, j: (l, 0, j)),
            pl.BlockSpec((None, 1, tn), lambda l, j: (l, 0, j)),
        ],
        out_specs=pl.BlockSpec((None, 8, tn), lambda l, j: (l, 0, j)),
        out_shape=jax.ShapeDtypeStruct((depth, 8, n), F32),
        compiler_params=_cparams(("arbitrary", "arbitrary")),
        name="adaln",
    )(cvec, w_ada, b_ada.reshape(depth, 1, n))


def _modulate_body(x_ref, g_ref, sh_ref, sc_ref, o_ref):
    x = x_ref[...]
    ms = jnp.mean(x * x, axis=-1, keepdims=True)
    y = x * lax.rsqrt(ms + EPS) * g_ref[...]
    o_ref[...] = (y * (1.0 + sc_ref[...]) + sh_ref[...]).astype(o_ref.dtype)


def modulate(x, g, shift, scale, tm=256):
    m, d = x.shape
    vec = pl.BlockSpec((1, d), lambda i: (0, 0))
    return pl.pallas_call(
        _modulate_body,
        grid=(m // tm,),
        in_specs=[pl.BlockSpec((tm, d), lambda i: (i, 0)), vec, vec, vec],
        out_specs=pl.BlockSpec((tm, d), lambda i: (i, 0)),
        out_shape=jax.ShapeDtypeStruct((m, d), BF16),
        compiler_params=_cparams(("arbitrary",)),
        name="modulate",
    )(x, g.reshape(1, d), shift.reshape(1, d), scale.reshape(1, d))


def _mm_nt_body(a_ref, w_ref, o_ref):
    w = w_ref[...].astype(BF16)
    acc = lax.dot_general(a_ref[...], w, (((1,), (1,)), ((), ())), preferred_element_type=F32)
    o_ref[...] = acc.astype(o_ref.dtype)


def matmul_nt(a, wt, *, tm, tn, out_dtype=BF16, layer=None, n_out=None, row_map=None, name="matmul_nt"):
    m, k = a.shape
    n = n_out if n_out is not None else wt.shape[-2]
    tm = min(tm, m)
    rm = row_map if row_map is not None else (lambda j: j)
    if wt.ndim == 3:
        w_spec = pl.BlockSpec((None, tn, k), lambda i, j: (layer, rm(j), 0))
    else:
        w_spec = pl.BlockSpec((tn, k), lambda i, j: (rm(j), 0))
    return pl.pallas_call(
        _mm_nt_body,
        grid=(m // tm, n // tn),
        in_specs=[pl.BlockSpec((tm, k), lambda i, j: (i, 0), pipeline_mode=pl.Buffered(1)), w_spec],
        out_specs=pl.BlockSpec((tm, tn), lambda i, j: (i, j)),
        out_shape=jax.ShapeDtypeStruct((m, n), out_dtype),
        compiler_params=_cparams(("arbitrary", "arbitrary")),
        name=name,
    )(a, wt)


def _out_proj_body(da_ref, ft_ref, gla_ref, w_ref, r_ref, g_ref, o_ref, mix_ref):
    @pl.when(pl.program_id(1) == 0)
    def _():
        mix_ref[:, :DA_WIDTH] = da_ref[...]
        mix_ref[:, DA_WIDTH:DA_WIDTH + FT_WIDTH] = ft_ref[...].astype(BF16)
        mix_ref[:, DA_WIDTH + FT_WIDTH:] = gla_ref[...]

    acc = jnp.dot(mix_ref[...], w_ref[...].astype(BF16), preferred_element_type=F32)
    o_ref[...] = r_ref[...] + g_ref[...] * acc


def out_proj(da, ft, gla, w_out, layer, res, gate, *, tm, tn, name="out_proj"):
    m = da.shape[0]
    k, n = w_out.shape[1:]
    tm = min(tm, m)
    once = pl.Buffered(1)
    return pl.pallas_call(
        _out_proj_body,
        grid=(m // tm, n // tn),
        in_specs=[
            pl.BlockSpec((tm, DA_WIDTH), lambda i, j: (i, 0), pipeline_mode=once),
            pl.BlockSpec((tm, FT_WIDTH), lambda i, j: (i, 0), pipeline_mode=once),
            pl.BlockSpec((tm, GLA_WIDTH), lambda i, j: (i, 0), pipeline_mode=once),
            pl.BlockSpec((None, k, tn), lambda i, j: (layer, 0, j)),
            pl.BlockSpec((tm, tn), lambda i, j: (i, j)),
            pl.BlockSpec((1, tn), lambda i, j: (0, j)),
        ],
        out_specs=pl.BlockSpec((tm, tn), lambda i, j: (i, j)),
        out_shape=jax.ShapeDtypeStruct((m, n), F32),
        scratch_shapes=[pltpu.VMEM((tm, k), BF16)],
        compiler_params=_cparams(("arbitrary", "arbitrary")),
        name=name,
    )(da, ft, gla, w_out, res, gate.reshape(1, n))


def _mm_ksplit_res_body(a_ref, b_ref, r_ref, g_ref, o_ref, acc_ref):
    kk = pl.program_id(2)

    @pl.when(kk == 0)
    def _():
        acc_ref[...] = jnp.zeros_like(acc_ref)

    acc_ref[...] += jnp.dot(a_ref[...], b_ref[...], preferred_element_type=F32)

    @pl.when(kk == pl.num_programs(2) - 1)
    def _():
        o_ref[...] = r_ref[...] + g_ref[...] * acc_ref[...]


def matmul_ksplit_res(a, b, layer, res, gate, *, tm, tn, tk, name="matmul_ksplit"):
    m, k = a.shape
    n = b.shape[2]
    tm = min(tm, m)
    return pl.pallas_call(
        _mm_ksplit_res_body,
        grid=(m // tm, n // tn, k // tk),
        in_specs=[
            pl.BlockSpec((tm, tk), lambda i, j, kk: (i, kk)),
            pl.BlockSpec((None, tk, tn), lambda i, j, kk: (layer, kk, j)),
            pl.BlockSpec((tm, tn), lambda i, j, kk: (i, j)),
            pl.BlockSpec((1, tn), lambda i, j, kk: (0, j)),
        ],
        out_specs=pl.BlockSpec((tm, tn), lambda i, j, kk: (i, j)),
        out_shape=jax.ShapeDtypeStruct((m, n), F32),
        scratch_shapes=[pltpu.VMEM((tm, tn), F32)],
        compiler_params=_cparams(("parallel", "arbitrary", "arbitrary")),
        name=name,
    )(a, b, res, gate.reshape(1, n))


def _qk_prep_body(use_rope, p_ref, gq_ref, gk_ref, cos_ref, sa_ref, sb_ref, qt_ref, k_ref, vt_ref):
    n_groups = (2 * DA_HEADS * DA_HEAD_DIM) // LANES
    if use_rope:
        cos, sa, sb = cos_ref[...], sa_ref[...], sb_ref[...]

    def prep(col, gain, scale):
        x = p_ref[:, col:col + LANES].astype(F32)
        ms = jnp.mean(x * x, axis=-1, keepdims=True)
        y = x * lax.rsqrt(ms + EPS) * gain
        if use_rope:
            y = y * cos + pltpu.roll(y, 32, 1) * sa + pltpu.roll(y, 96, 1) * sb
        return y * scale if scale != 1.0 else y

    gq, gk = gq_ref[...], gk_ref[...]
    for g in range(n_groups):
        qt_ref[g * LANES:(g + 1) * LANES, :] = prep(P_Q + g * LANES, gq, Q_SCALE).T.astype(qt_ref.dtype)
        k_ref[:, g * LANES:(g + 1) * LANES] = prep(P_K + g * LANES, gk, 1.0).astype(k_ref.dtype)
    for h in range(DA_HEADS):
        v = p_ref[:, P_V + h * DA_V_DIM:P_V + (h + 1) * DA_V_DIM].astype(F32)
        vt_ref[h] = v.T.astype(vt_ref.dtype)


def qk_prep(p, gq, gk, rope, tm):
    rows = p.shape[0]
    use_rope = rope is not None
    width = 2 * DA_HEADS * DA_HEAD_DIM
    vec = pl.BlockSpec((1, LANES), lambda i: (0, 0))
    tab = pl.BlockSpec((tm, LANES), lambda i: (i, 0))
    if use_rope:
        tabs = list(rope)
    else:
        tabs = [jnp.zeros((rows, LANES), F32)] * 3
    return pl.pallas_call(
        functools.partial(_qk_prep_body, use_rope),
        grid=(rows // tm,),
        in_specs=[pl.BlockSpec((tm, P_V + DA_WIDTH), lambda i: (i, 0)), vec, vec, tab, tab, tab],
        out_specs=[pl.BlockSpec((width, tm), lambda i: (0, i)),
                   pl.BlockSpec((tm, width), lambda i: (i, 0)),
                   pl.BlockSpec((DA_HEADS, None, DA_V_DIM, tm), lambda i: (0, i, 0, 0))],
        out_shape=[jax.ShapeDtypeStruct((width, rows), BF16),
                   jax.ShapeDtypeStruct((rows, width), BF16),
                   jax.ShapeDtypeStruct((DA_HEADS, rows // tm, DA_V_DIM, tm), BF16)],
        compiler_params=_cparams(("arbitrary",)),
        name="qk_prep",
    )(p, gq.reshape(1, LANES), gk.reshape(1, LANES), *tabs)


def _attn_body(lam_init, n_chunks, tk, has_ctx, *refs):
    if has_ctx:
        (qt_ref, k_ref, vt_ref, kc_ref, vtc_ref, lq1_ref, lk1_ref, lq2_ref, lk2_ref, g_ref,
         o_ref, m_ref, l_ref, acc_ref) = refs
    else:
        (qt_ref, k_ref, vt_ref, lq1_ref, lk1_ref, lq2_ref, lk2_ref, g_ref,
         o_ref, m_ref, l_ref, acc_ref) = refs
    d = DA_HEAD_DIM
    qt = (qt_ref[:d, :], qt_ref[d:, :])
    m_ref[...] = jnp.full_like(m_ref, -jnp.inf)
    l_ref[...] = jnp.zeros_like(l_ref)
    acc_ref[...] = jnp.zeros_like(acc_ref)

    def update(k_blk, vt_blk):
        for mi in range(2):
            s = jnp.dot(k_blk[:, mi * d:(mi + 1) * d], qt[mi], preferred_element_type=F32)
            m_old = m_ref[mi]
            m_new = jnp.maximum(m_old, jnp.max(s, axis=0, keepdims=True))
            alpha = jnp.exp2(m_old - m_new)
            p = jnp.exp2(s - m_new)
            l_ref[mi] = alpha * l_ref[mi] + jnp.sum(p, axis=0, keepdims=True)
            acc_ref[mi] = alpha * acc_ref[mi] + jnp.dot(vt_blk, p.astype(BF16), preferred_element_type=F32)
            m_ref[mi] = m_new

    if has_ctx:
        update(kc_ref[...], vtc_ref[...])

    def step(c, carry):
        off = pl.multiple_of(c * tk, tk)
        update(k_ref[pl.ds(off, tk), :], vt_ref[c])
        return carry

    lax.fori_loop(0, n_chunks, step, 0, unroll=math.gcd(n_chunks, 8))

    lam =(jnp.exp(jnp.sum(lq1_ref[...] * lk1_ref[...])) - jnp.exp(jnp.sum(lq2_ref[...] * lk2_ref[...]))
           + lam_init)
    ot = acc_ref[0] * (1.0 / l_ref[0]) - lam * (acc_ref[1] * (1.0 / l_ref[1]))
    ms = jnp.mean(ot * ot, axis=0, keepdims=True)
    o = (ot * lax.rsqrt(ms + EPS)).T
    o_ref[...] = (o * g_ref[...] * (1.0 - lam_init)).astype(o_ref.dtype)


def diff_attention(qt, k, vt, kc, vtc, lams, subln_g, lam_init, *, tq):
    t_q = qt.shape[1]
    t_k = k.shape[0]
    n_chunks, tk = vt.shape[1], vt.shape[3]
    has_ctx = kc is not None
    tq = min(tq, t_q)
    hw = 2 * DA_HEAD_DIM
    in_specs = [
        pl.BlockSpec((hw, tq), lambda h, i: (h, i)),
        pl.BlockSpec((t_k, hw), lambda h, i: (0, h)),
        pl.BlockSpec((None, n_chunks, DA_V_DIM, tk), lambda h, i: (h, 0, 0, 0)),
    ]
    args = [qt, k, vt]
    if has_ctx:
        t_c = kc.shape[0]
        in_specs += [pl.BlockSpec((t_c, hw), lambda h, i: (0, h)),
                     pl.BlockSpec((None, None, DA_V_DIM, t_c), lambda h, i: (h, 0, 0, 0))]
        args += [kc, vtc]
    vec = pl.BlockSpec((1, DA_HEAD_DIM), lambda h, i: (0, 0))
    in_specs += [vec] * 4 + [pl.BlockSpec((1, DA_V_DIM), lambda h, i: (0, 0))]
    args += [v.reshape(1, DA_HEAD_DIM) for v in lams] + [subln_g.reshape(1, DA_V_DIM)]
    return pl.pallas_call(
        functools.partial(_attn_body, lam_init, n_chunks, tk, has_ctx),
        grid=(DA_HEADS, t_q // tq),
        in_specs=in_specs,
        out_specs=pl.BlockSpec((tq, DA_V_DIM), lambda h, i: (i, h)),
        out_shape=jax.ShapeDtypeStruct((t_q, DA_WIDTH), BF16),
        scratch_shapes=[pltpu.VMEM((2, 1, tq), F32), pltpu.VMEM((2, 1, tq), F32),
                        pltpu.VMEM((2, DA_V_DIM, tq), F32)],
        compiler_params=_cparams(("parallel", "arbitrary")),
        name="diff_attention",
    )(*args)


def _dft_tables_latent(t_len):
    n2 = FFT_N2
    n1 = t_len // n2
    f1 = np.arange(n1)[:, None]
    t1 = np.arange(n1)[None, :]
    ang = 2.0 * np.pi * ((f1 * t1) % n1) / n1
    stage_a = np.concatenate([np.cos(ang), np.sin(ang)], axis=0)
    f2 = np.arange(n2)[None, :, None]
    t2 = np.arange(n2)[None, None, :]
    f1b = np.arange(n1)[:, None, None]
    idx = (t2 * f2 * n1 + t2 * f1b) % t_len
    th = 2.0 * np.pi * idx / t_len
    mr, mi = np.cos(th), -np.sin(th)
    stage_b = np.concatenate([np.concatenate([mr, mi], axis=2), np.concatenate([mi, -mr], axis=2)], axis=1)
    return stage_a.astype(np.float32), stage_b.astype(np.float32)


def _dft_table_channels(t_len, sign):
    c = np.arange(FT_GROUP_DIM)
    ang = 2.0 * np.pi * ((c[:, None] * c[None, :]) % FT_GROUP_DIM) / FT_GROUP_DIM
    scale = 1.0 / math.sqrt(t_len * FT_GROUP_DIM)
    return (np.concatenate([np.cos(ang), sign * np.sin(ang)], axis=0) * scale).astype(np.float32)


def _fft_a_body(cs_ref, z_ref, p_ref, q_ref):
    n1 = z_ref.shape[0]
    r = jnp.dot(cs_ref[...], z_ref[...], preferred_element_type=F32)
    p_ref[...] = r[:n1].astype(p_ref.dtype)
    q_ref[...] = r[n1:].astype(q_ref.dtype)


def _fft_b_body(m_ref, p_ref, q_ref, ch_ref, o_ref):
    n2 = FFT_N2
    pq = jnp.concatenate([p_ref[...], q_ref[...]], axis=0)
    r = jnp.dot(m_ref[...], pq, preferred_element_type=F32)
    ch = ch_ref[...]
    for g in range(FT_GROUPS):
        c0 = g * FT_GROUP_DIM
        rg = jnp.concatenate([r[:n2, c0:c0 + FT_GROUP_DIM], r[n2:, c0:c0 + FT_GROUP_DIM]], axis=1).astype(BF16)
        o_ref[:, c0:c0 + FT_GROUP_DIM] = jnp.dot(rg, ch, preferred_element_type=F32).astype(o_ref.dtype)


def fourier_mix_latent(z, tn=8192):
    t_len, width = z.shape
    n2 = FFT_N2
    n1 = t_len // n2
    ta, tb = _dft_tables_latent(t_len)
    cs = jnp.asarray(ta).astype(BF16)
    mt = jnp.asarray(tb).astype(BF16)
    ch = jnp.asarray(_dft_table_channels(t_len, 1.0)).astype(BF16)
    cols = n2 * width
    tn = min(tn, cols)
    zv = z.reshape(n1, cols)
    pv, qv = pl.pallas_call(
        _fft_a_body,
        grid=(cols // tn,),
        in_specs=[pl.BlockSpec((2 * n1, n1), lambda j: (0, 0)), pl.BlockSpec((n1, tn), lambda j: (0, j))],
        out_specs=[pl.BlockSpec((n1, tn), lambda j: (0, j))] * 2,
        out_shape=[jax.ShapeDtypeStruct((n1, cols), BF16)] * 2,
        compiler_params=_cparams(("arbitrary",)),
        name="fft_stage_a",
    )(cs, zv)
    pm = pv.reshape(t_len, width)
    qm = qv.reshape(t_len, width)
    out = pl.pallas_call(
        _fft_b_body,
        grid=(n1,),
        in_specs=[
            pl.BlockSpec((None, 2 * n2, 2 * n2), lambda f: (f, 0, 0)),
            pl.BlockSpec((n2, width), lambda f: (f, 0)),
            pl.BlockSpec((n2, width), lambda f: (f, 0)),
            pl.BlockSpec((2 * FT_GROUP_DIM, FT_GROUP_DIM), lambda f: (0, 0)),
        ],
        out_specs=pl.BlockSpec((n2, width), lambda f: (0, f)),
        out_shape=jax.ShapeDtypeStruct((n2, n1 * width), BF16),
        compiler_params=_cparams(("arbitrary",)),
        name="fft_stage_b",
    )(mt, pm, qm, ch)
    return out.reshape(t_len, width)


def _fft_ctx_body(cs_ref, z_ref, ch_ref, o_ref):
    t_len = z_ref.shape[0]
    r = jnp.dot(cs_ref[...], z_ref[...], preferred_element_type=F32)
    ch = ch_ref[...]
    for g in range(FT_GROUPS):
        c0 = g * FT_GROUP_DIM
        rg = jnp.concatenate([r[:t_len, c0:c0 + FT_GROUP_DIM], r[t_len:, c0:c0 + FT_GROUP_DIM]], axis=1).astype(BF16)
        o_ref[:, c0:c0 + FT_GROUP_DIM] = jnp.dot(rg, ch, preferred_element_type=F32).astype(o_ref.dtype)


def fourier_mix_short(z):
    t_len, width = z.shape
    f = np.arange(t_len)
    ang = 2.0 * np.pi * ((f[:, None] * f[None, :]) % t_len) / t_len
    cs = jnp.asarray(np.concatenate([np.cos(ang), np.sin(ang)], axis=0).astype(np.float32)).astype(BF16)
    ch = jnp.asarray(_dft_table_channels(t_len, -1.0)).astype(BF16)
    return pl.pallas_call(
        _fft_ctx_body,
        out_shape=jax.ShapeDtypeStruct((t_len, width), BF16),
        compiler_params=pltpu.CompilerParams(vmem_limit_bytes=V7X_VMEM_LIMIT_BYTES),
        name="fft_short",
    )(cs, z, ch)


def _split3(x):
    hi = x.astype(BF16)
    r1 = x - hi.astype(F32)
    mid = r1.astype(BF16)
    lo = (r1 - mid.astype(F32)).astype(BF16)
    return hi, mid, lo


def _gla_body(reverse, q_ref, k_ref, v_ref, gd_ref, qc_ref, kc_ref, vc_ref, gdc_ref, w2_ref, b2_ref,
              o_ref, st_ref):
    n = pl.program_id(0)
    rows = GLA_ROWS
    ch = GLA_CHUNK
    n_chunks = rows // ch
    dk, dv = GLA_K_DIM, GLA_V_DIM

    @pl.when(n == 0)
    def _():
        st_ref[...] = jnp.zeros_like(st_ref)

    is_ctx = n == 0
    q = jnp.where(is_ctx, qc_ref[...], q_ref[...])
    k = jnp.where(is_ctx, kc_ref[...], k_ref[...])
    v = jnp.where(is_ctx, vc_ref[...], v_ref[...])
    gd = jnp.where(is_ctx, gdc_ref[...], gd_ref[...])

    logits = jnp.dot(gd, w2_ref[...], preferred_element_type=F32) + b2_ref[...]
    log_a = (jnp.minimum(logits, 0.0) - jnp.log(1.0 + jnp.exp(-jnp.abs(logits)))) / GLA_TAU

    ri = lax.broadcasted_iota(jnp.int32, (rows, rows), 0)
    ci = lax.broadcasted_iota(jnp.int32, (rows, rows), 1)
    same = (ri // ch) == (ci // ch)
    tri = jnp.logical_and(same, (ci >= ri) if reverse else (ci <= ri))
    tri_b = jnp.where(tri, 1.0, 0.0).astype(BF16)
    hi, mid, lo = _split3(log_a)
    cum = (jnp.dot(tri_b, hi, preferred_element_type=F32) + jnp.dot(tri_b, mid, preferred_element_type=F32)
           + jnp.dot(tri_b, lo, preferred_element_type=F32))
    e_pos = jnp.exp(cum)
    e_neg = jnp.exp(-cum)
    order = list(range(n_chunks))[::-1] if reverse else list(range(n_chunks))
    last_row = [(c * ch) if reverse else (c * ch + ch - 1) for c in range(n_chunks)]
    scale = dk ** -0.5

    for h in range(GLA_HEADS):
        ks = slice(h * dk, (h + 1) * dk)
        vs = slice(h * dv, (h + 1) * dv)
        qf = q[:, ks].astype(F32) * scale
        kf = k[:, ks].astype(F32)
        vh = v[:, vs]
        q_in = (qf * e_pos[:, ks]).astype(BF16)
        k_in = (kf * e_neg[:, ks]).astype(BF16)
        a = lax.dot_general(q_in, k_in, (((1,), (1,)), ((), ())), preferred_element_type=F32)
        a = jnp.where(tri, a, 0.0).astype(BF16)
        o_intra = jnp.dot(a, vh, preferred_element_type=F32)
        st = st_ref[h]
        for c in order:
            rs = slice(c * ch, (c + 1) * ch)
            cum_last = cum[last_row[c]:last_row[c] + 1, ks]
            o_inter = lax.dot_general(q_in[rs], st.astype(BF16), (((1,), (1,)), ((), ())),
                                      preferred_element_type=F32)
            o_ref[rs, vs] = o_intra[rs] + o_inter
            k_dec = (kf[rs] * jnp.exp(cum_last - cum[rs, ks])).astype(BF16)
            kv_t = lax.dot_general(vh[rs], k_dec, (((0,), (0,)), ((), ())), preferred_element_type=F32)
            st = jnp.exp(cum_last) * st + kv_t
        st_ref[h] = st


def gla_direction(p, gd, pc, gdc, w2pad, b2, reverse):
    t_len = p.shape[0]
    rows = GLA_ROWS
    nb = t_len // rows
    hk = GLA_HEADS * GLA_K_DIM

    if reverse:
        lat = lambda n: jnp.where(n == 0, nb - 1, nb - n)
    else:
        lat = lambda n: jnp.where(n == 0, 0, n - 1)
    out_blk = lambda n: jnp.where(n == 0, nb, lat(n))
    in_specs = [
        pl.BlockSpec((rows, hk), lambda n: (lat(n), P_GQ // hk)),
        pl.BlockSpec((rows, hk), lambda n: (lat(n), P_GK // hk)),
        pl.BlockSpec((rows, GLA_WIDTH), lambda n: (lat(n), P_GV // GLA_WIDTH)),
        pl.BlockSpec((rows, LANES), lambda n: (lat(n), 0)),
        pl.BlockSpec((rows, hk), lambda n: (0, P_GQ // hk)),
        pl.BlockSpec((rows, hk), lambda n: (0, P_GK // hk)),
        pl.BlockSpec((rows, GLA_WIDTH), lambda n: (0, P_GV // GLA_WIDTH)),
        pl.BlockSpec((rows, LANES), lambda n: (0, 0)),
        pl.BlockSpec((LANES, hk), lambda n: (0, 0)),
        pl.BlockSpec((1, hk), lambda n: (0, 0)),
    ]
    return pl.pallas_call(
        functools.partial(_gla_body, reverse),
        grid=(nb + 1,),
        in_specs=in_specs,
        out_specs=pl.BlockSpec((rows, GLA_WIDTH), lambda n: (out_blk(n), 0)),
        out_shape=jax.ShapeDtypeStruct((t_len + rows, GLA_WIDTH), F32),
        scratch_shapes=[pltpu.VMEM((GLA_HEADS, GLA_V_DIM, GLA_K_DIM), F32)],
        compiler_params=_cparams(("arbitrary",)),
        name="gla_bwd" if reverse else "gla_fwd",
    )(p, p, p, gd, pc, pc, pc, gdc, w2pad, b2.reshape(1, hk))


def _gla_out_body(of_ref, ob_ref, r_ref, g_ref, y_ref):
    g = g_ref[...]
    for h in range(GLA_HEADS):
        vs = slice(h * GLA_V_DIM, (h + 1) * GLA_V_DIM)
        o = of_ref[:, vs] + ob_ref[:, vs]
        ms = jnp.mean(o * o, axis=-1, keepdims=True)
        r = r_ref[:, vs].astype(F32)
        y_ref[:, vs] = (o * lax.rsqrt(ms + EPS) * g * (r * _sigmoid(r))).astype(y_ref.dtype)


def gla_output(o_f, o_b, r, g, row_block0, tm=256):
    n_rows = r.shape[0]
    o_spec = pl.BlockSpec((tm, GLA_WIDTH), lambda i: (row_block0 + i, 0))
    return pl.pallas_call(
        _gla_out_body,
        grid=(n_rows // tm,),
        in_specs=[o_spec, o_spec, pl.BlockSpec((tm, GLA_WIDTH), lambda i: (i, 0)),
                  pl.BlockSpec((1, GLA_V_DIM), lambda i: (0, 0))],
        out_specs=pl.BlockSpec((tm, GLA_WIDTH), lambda i: (i, 0)),
        out_shape=jax.ShapeDtypeStruct((n_rows, GLA_WIDTH), BF16),
        compiler_params=_cparams(("arbitrary",)),
        name="gla_output",
    )(o_f, o_b, r, g.reshape(1, GLA_V_DIM))


FFN_HALO = 16


def _ffn_up_body(n_row_tiles, a_ref, ap_ref, an_ref, wg_ref, wv_ref, cwg_ref, cwv_ref, cbg_ref, cbv_ref,
                 o_ref, ext_ref):
    i = pl.program_id(0)
    tm = a_ref.shape[0]
    hl = FFN_HALO

    @pl.when(pl.program_id(1) == 0)
    def _():
        ext_ref[hl:hl + tm, :] = a_ref[...]
        zero = jnp.zeros_like(ap_ref)
        ext_ref[0:hl, :] = jnp.where(i > 0, ap_ref[...], zero)
        ext_ref[hl + tm:, :] = jnp.where(i < n_row_tiles - 1, an_ref[...], zero)

    ext = ext_ref[...]

    def conv(w_ref, cw_ref, cb_ref):
        u = jnp.dot(ext, w_ref[...].astype(BF16), preferred_element_type=F32)
        c = (pltpu.roll(u, 1, 0) * cw_ref[0:1, :] + u * cw_ref[1:2, :]
             + pltpu.roll(u, tm + 2 * hl - 1, 0) * cw_ref[2:3, :])
        return c[hl:hl + tm] + cb_ref[...]

    gate = conv(wg_ref, cwg_ref, cbg_ref)
    val = conv(wv_ref, cwv_ref, cbv_ref)
    o_ref[...] = (gate * _sigmoid(gate) * val).astype(o_ref.dtype)


def ffn_up_gated(h, w_up, conv_w, conv_b, layer, *, tm, tn):
    m, k = h.shape
    f = w_up.shape[2] // 2
    tm = min(tm, m)
    nt = m // tm
    nc = f // tn
    hb = tm // FFN_HALO
    last = m // FFN_HALO - 1
    gate_half = lambda rows: pl.BlockSpec((None, rows, tn), lambda i, j: (layer, 0, j))
    val_half = lambda rows: pl.BlockSpec((None, rows, tn), lambda i, j: (layer, 0, j + nc))
    cb = conv_b.reshape(conv_b.shape[0], 1, 2 * f)
    return pl.pallas_call(
        functools.partial(_ffn_up_body, nt),
        grid=(nt, nc),
        in_specs=[
            pl.BlockSpec((tm, k), lambda i, j: (i, 0), pipeline_mode=pl.Buffered(1)),
            pl.BlockSpec((FFN_HALO, k), lambda i, j: (jnp.maximum(i * hb - 1, 0), 0)),
            pl.BlockSpec((FFN_HALO, k), lambda i, j: (jnp.minimum((i + 1) * hb, last), 0)),
            gate_half(k), val_half(k), gate_half(3), val_half(3), gate_half(1), val_half(1),
        ],
        out_specs=pl.BlockSpec((tm, tn), lambda i, j: (i, j)),
        out_shape=jax.ShapeDtypeStruct((m, f), BF16),
        scratch_shapes=[pltpu.VMEM((tm + 2 * FFN_HALO, k), BF16)],
        compiler_params=_cparams(("arbitrary", "arbitrary")),
        name="ffn_up_gated",
    )(h, h, h, w_up, w_up, conv_w, conv_w, cb, cb)


def _rope_tables(t_len):
    rows = t_len // GRID_W
    row = jnp.repeat(jnp.arange(rows), GRID_W).astype(F32)
    col = jnp.tile(jnp.arange(GRID_W), rows).astype(F32)
    half = DA_HEAD_DIM // 2
    freqs = ROPE_BASE ** (-jnp.arange(0, half, 2, dtype=F32) / half)
    ar = row[:, None] * freqs
    ac = col[:, None] * freqs
    ang = jnp.concatenate([ar, ar, ac, ac], axis=-1)
    cos, sin = jnp.cos(ang), jnp.sin(ang)
    hi = (jnp.arange(DA_HEAD_DIM) % 64) >= 32
    return cos, jnp.where(hi, sin, 0.0), jnp.where(hi, 0.0, -sin)


def _ffn(x, h, w_up, conv_w, conv_b, w_down_bf16, l, gate):
    act = ffn_up_gated(h, w_up, conv_w, conv_b, l, tm=1024, tn=256)
    return matmul_ksplit_res(act, w_down_bf16, l, x, gate, tm=512, tn=1024, tk=D_FF // 2, name="ffn_down")


def _in_proj(h, w_in_t, l, w_r, w_gd, suffix):
    skip_ft = (W_GQ - W_FT) // 512
    p = matmul_nt(h, w_in_t, layer=l, tm=2048, tn=512, n_out=P_WIDTH,
                  row_map=lambda j: jnp.where(j < W_FT // 512, j, j + skip_ft), name="in_proj" + suffix)
    z = matmul_nt(h, w_in_t, layer=l, tm=2048, tn=512, n_out=FT_WIDTH, row_map=lambda j: j + W_FT // 512,
                  name="in_proj_ft" + suffix)
    r = matmul_nt(h, w_r, tm=2048, tn=512, name="in_proj_r" + suffix)
    gd = matmul_nt(h, w_gd, tm=2048, tn=LANES, name="in_proj_gd" + suffix)
    return p, z, r, gd


def kernel(x, c, ctx, c_ctx, w_ada, b_ada, norm1_g, norm2_g, w_in, q_norm_g, k_norm_g, lambda_q1, lambda_k1,
           lambda_q2, lambda_k2, da_subln_g, gla_gate_w2, gla_gate_b, gla_norm_g, w_out, w_up, conv_w, conv_b,
           w_down):
    depth = w_ada.shape[0]
    t_len = x.shape[1]
    d = D_MODEL
    xl = x[0]
    xc = ctx[0]
    rope = _rope_tables(t_len)
    w_in_t = jnp.swapaxes(w_in, 1, 2)
    w_down_bf16 = w_down.astype(BF16)

    cvec = jnp.zeros((8, d), F32).at[0].set(c[0]).at[1].set(c_ctx)
    mod = adaln(cvec, w_ada, b_ada)

    for l in range(depth):
        need_ctx = l < depth - 1
        lam_init = 0.8 - 0.6 * math.exp(-0.3 * l)
        mods_l = [mod[l, 0, i * d:(i + 1) * d] for i in range(N_MOD)]
        mods_c = [mod[l, 1, i * d:(i + 1) * d] for i in range(N_MOD)]

        w_r = w_in_t[l, W_GR:W_GR + GLA_WIDTH, :]
        w_gd = jnp.pad(w_in_t[l, W_GD:W_GR, :], ((0, LANES - 2 * GLA_GATE_RANK), (0, 0)))

        h_l = modulate(xl, norm1_g[l], mods_l[0], mods_l[1])
        h_c = modulate(xc, norm1_g[l], mods_c[0], mods_c[1])
        p_l, z_l, r_l, gd_l = _in_proj(h_l, w_in_t, l, w_r, w_gd, "")
        p_c, z_c, r_c, gd_c = _in_proj(h_c, w_in_t, l, w_r, w_gd, "_ctx")

        qt_l, k_l, vt_l = qk_prep(p_l, q_norm_g[l], k_norm_g[l], rope, ATTN_TK)
        qt_c, k_c, vt_c = qk_prep(p_c, q_norm_g[l], k_norm_g[l], None, xc.shape[0])
        lams = (lambda_q1[l], lambda_k1[l], lambda_q2[l], lambda_k2[l])
        da_l = diff_attention(qt_l, k_l, vt_l, k_c, vt_c, lams, da_subln_g[l], lam_init, tq=256)

        ft_l = fourier_mix_latent(z_l)

        w2 = gla_gate_w2[l]
        zpad = jnp.zeros((LANES - 2 * GLA_GATE_RANK, w2.shape[-1]), F32)
        zr = jnp.zeros((GLA_GATE_RANK, w2.shape[-1]), F32)
        w2_f = jnp.concatenate([w2[0], zr, zpad], axis=0).astype(BF16)
        w2_b = jnp.concatenate([zr, w2[1], zpad], axis=0).astype(BF16)
        o_f = gla_direction(p_l, gd_l, p_c, gd_c, w2_f, gla_gate_b[l, 0], False)
        o_b = gla_direction(p_l, gd_l, p_c, gd_c, w2_b, gla_gate_b[l, 1], True)
        gla_l = gla_output(o_f, o_b, r_l, gla_norm_g[l], 0)

        x_new = out_proj(da_l, ft_l, gla_l, w_out, l, xl, mods_l[2], tm=1024, tn=512)
        h2_l = modulate(x_new, norm2_g[l], mods_l[3], mods_l[4])
        x_new = _ffn(x_new, h2_l, w_up, conv_w, conv_b, w_down_bf16, l, mods_l[5])

        if need_ctx:
            da_c = diff_attention(qt_c, k_c, vt_c, None, None, lams, da_subln_g[l], lam_init, tq=256)
            ft_c = fourier_mix_short(z_c)
            gla_c = gla_output(o_f, o_b, r_c, gla_norm_g[l], t_len // 256)
            xc_new = out_proj(da_c, ft_c, gla_c, w_out, l, xc, mods_c[2], tm=1024, tn=512, name="out_proj_ctx")
            h2_c = modulate(xc_new, norm2_g[l], mods_c[3], mods_c[4])
            xc = _ffn(xc_new, h2_c, w_up, conv_w, conv_b, w_down_bf16, l, mods_c[5])
        xl = x_new

    return xl[None]
```

```python
import functools
import math

import numpy as np
import jax
import jax.numpy as jnp
from jax import lax
from jax.experimental import pallas as pl
from jax.experimental.pallas import tpu as pltpu

F32 = jnp.float32
BF16 = jnp.bfloat16

D_MODEL = 4096
CTX_LEN = 256
GRID_W = 64
EPS = 1e-6
ROPE_BASE = 10000.0
DA_HEAD_DIM = 128
DA_HEADS = 8
DA_V_DIM = 256
DA_WIDTH = DA_HEADS * DA_V_DIM
FT_GROUPS = 4
FT_GROUP_DIM = 256
FT_WIDTH = FT_GROUPS * FT_GROUP_DIM
GLA_HEADS = 4
GLA_V_DIM = 256
GLA_K_DIM = 128
GLA_WIDTH = GLA_HEADS * GLA_V_DIM
GLA_GATE_RANK = 16
GLA_TAU = 16.0
GLA_CHUNK = 64
D_FF = 11008
N_MOD = 6

W_FT, W_GQ, W_GD, W_GR = 6144, 7168, 9216, 9248
P_Q, P_K, P_V = 0, 2048, 4096
P_GQ, P_GK, P_GV = 6144, 6656, 7168
P_WIDTH = 8192

V7X_VMEM_LIMIT_BYTES = 56 * 1024 * 1024
LANES = 128
FFT_N2 = 128
GLA_ROWS = 256
ATTN_TK = 512
Q_SCALE = DA_HEAD_DIM ** -0.5 * math.log2(math.e)


def _cparams(sem):
    return pltpu.CompilerParams(dimension_semantics=sem, vmem_limit_bytes=V7X_VMEM_LIMIT_BYTES)


def _sigmoid(x):
    return 1.0 / (1.0 + jnp.exp(-x))


def _adaln_body(c_ref, w_ref, b_ref, o_ref):
    c = c_ref[...]
    s = (c * _sigmoid(c)).astype(BF16)
    w = w_ref[...].astype(BF16)
    o_ref[...] = jnp.dot(s, w, preferred_element_type=F32) + b_ref[...]


def adaln(cvec, w_ada, b_ada, tn=512):
    depth, d, n = w_ada.shape
    return pl.pallas_call(
        _adaln_body,
        grid=(depth, n // tn),
        in_specs=[
            pl.BlockSpec((8, d), lambda l, j: (0, 0)),
            pl.BlockSpec((None, d, tn), lambda l, j: (l, 0, j)),
            pl.BlockSpec((None, 1, tn), lambda l, j: (l, 0, j)),
        ],
        out_specs=pl.BlockSpec((None, 8, tn), lambda l, j: (l, 0, j)),
        out_shape=jax.ShapeDtypeStruct((depth, 8, n), F32),
        compiler_params=_cparams(("arbitrary", "arbitrary")),
        name="adaln",
    )(cvec, w_ada, b_ada.reshape(depth, 1, n))


def _modulate_body(x_ref, g_ref, sh_ref, sc_ref, o_ref):
    x = x_ref[...]
    ms = jnp.mean(x * x, axis=-1, keepdims=True)
    y = x * lax.rsqrt(ms + EPS) * g_ref[...]
    o_ref[...] = (y * (1.0 + sc_ref[...]) + sh_ref[...]).astype(o_ref.dtype)


def modulate(x, g, shift, scale, tm=256):
    m, d = x.shape
    vec = pl.BlockSpec((1, d), lambda i: (0, 0))
    return pl.pallas_call(
        _modulate_body,
        grid=(m // tm,),
        in_specs=[pl.BlockSpec((tm, d), lambda i: (i, 0)), vec, vec, vec],
        out_specs=pl.BlockSpec((tm, d), lambda i: (i, 0)),
        out_shape=jax.ShapeDtypeStruct((m, d), BF16),
        compiler_params=_cparams(("arbitrary",)),
        name="modulate",
    )(x, g.reshape(1, d), shift.reshape(1, d), scale.reshape(1, d))


def _mm_nt_body(a_ref, w_ref, o_ref):
    w = w_ref[...].astype(BF16)
    acc = lax.dot_general(a_ref[...], w, (((1,), (1,)), ((), ())), preferred_element_type=F32)
    o_ref[...] = acc.astype(o_ref.dtype)


def matmul_nt(a, wt, *, tm, tn, out_dtype=BF16, layer=None, n_out=None, row_map=None, name="matmul_nt"):
    m, k = a.shape
    n = n_out if n_out is not None else wt.shape[-2]
    tm = min(tm, m)
    rm = row_map if row_map is not None else (lambda j: j)
    if wt.ndim == 3:
        w_spec = pl.BlockSpec((None, tn, k), lambda i, j: (layer, rm(j), 0))
    else:
        w_spec = pl.BlockSpec((tn, k), lambda i, j: (rm(j), 0))
    return pl.pallas_call(
        _mm_nt_body,
        grid=(m // tm, n // tn),
        in_specs=[pl.BlockSpec((tm, k), lambda i, j: (i, 0), pipeline_mode=pl.Buffered(1)), w_spec],
        out_specs=pl.BlockSpec((tm, tn), lambda i, j: (i, j)),
        out_shape=jax.ShapeDtypeStruct((m, n), out_dtype),
        compiler_params=_cparams(("arbitrary", "arbitrary")),
        name=name,
    )(a, wt)


def _out_proj_body(da_ref, ft_ref, gla_ref, w_ref, r_ref, g_ref, o_ref, mix_ref):
    @pl.when(pl.program_id(1) == 0)
    def _():
        mix_ref[:, :DA_WIDTH] = da_ref[...]
        mix_ref[:, DA_WIDTH:DA_WIDTH + FT_WIDTH] = ft_ref[...].astype(BF16)
        mix_ref[:, DA_WIDTH + FT_WIDTH:] = gla_ref[...]

    acc = jnp.dot(mix_ref[...], w_ref[...].astype(BF16), preferred_element_type=F32)
    o_ref[...] = r_ref[...] + g_ref[...] * acc


def out_proj(da, ft, gla, w_out, layer, res, gate, *, tm, tn, name="out_proj"):
    m = da.shape[0]
    k, n = w_out.shape[1:]
    tm = min(tm, m)
    once = pl.Buffered(1)
    return pl.pallas_call(
        _out_proj_body,
        grid=(m // tm, n // tn),
        in_specs=[
            pl.BlockSpec((tm, DA_WIDTH), lambda i, j: (i, 0), pipeline_mode=once),
            pl.BlockSpec((tm, FT_WIDTH), lambda i, j: (i, 0), pipeline_mode=once),
            pl.BlockSpec((tm, GLA_WIDTH), lambda i, j: (i, 0), pipeline_mode=once),
            pl.BlockSpec((None, k, tn), lambda i, j: (layer, 0, j)),
            pl.BlockSpec((tm, tn), lambda i, j: (i, j)),
            pl.BlockSpec((1, tn), lambda i, j: (0, j)),
        ],
        out_specs=pl.BlockSpec((tm, tn), lambda i, j: (i, j)),
        out_shape=jax.ShapeDtypeStruct((m, n), F32),
        scratch_shapes=[pltpu.VMEM((tm, k), BF16)],
        compiler_params=_cparams(("arbitrary", "arbitrary")),
        name=name,
    )(da, ft, gla, w_out, res, gate.reshape(1, n))


def _mm_ksplit_res_body(a_ref, b_ref, r_ref, g_ref, o_ref, acc_ref):
    kk = pl.program_id(2)

    @pl.when(kk == 0)
    def _():
        acc_ref[...] = jnp.zeros_like(acc_ref)

    acc_ref[...] += jnp.dot(a_ref[...], b_ref[...], preferred_element_type=F32)

    @pl.when(kk == pl.num_programs(2) - 1)
    def _():
        o_ref[...] = r_ref[...] + g_ref[...] * acc_ref[...]


def matmul_ksplit_res(a, b, layer, res, gate, *, tm, tn, tk, name="matmul_ksplit"):
    m, k = a.shape
    n = b.shape[2]
    tm = min(tm, m)
    return pl.pallas_call(
        _mm_ksplit_res_body,
        grid=(m // tm, n // tn, k // tk),
        in_specs=[
            pl.BlockSpec((tm, tk), lambda i, j, kk: (i, kk)),
            pl.BlockSpec((None, tk, tn), lambda i, j, kk: (layer, kk, j)),
            pl.BlockSpec((tm, tn), lambda i, j, kk: (i, j)),
            pl.BlockSpec((1, tn), lambda i, j, kk: (0, j)),
        ],
        out_specs=pl.BlockSpec((tm, tn), lambda i, j, kk: (i, j)),
        out_shape=jax.ShapeDtypeStruct((m, n), F32),
        scratch_shapes=[pltpu.VMEM((tm, tn), F32)],
        compiler_params=_cparams(("parallel", "arbitrary", "arbitrary")),
        name=name,
    )(a, b, res, gate.reshape(1, n))


def _qk_prep_body(use_rope, p_ref, gq_ref, gk_ref, cos_ref, sa_ref, sb_ref, qt_ref, k_ref, vt_ref):
    n_groups = (2 * DA_HEADS * DA_HEAD_DIM) // LANES
    if use_rope:
        cos, sa, sb = cos_ref[...], sa_ref[...], sb_ref[...]

    def prep(col, gain, scale):
        x = p_ref[:, col:col + LANES].astype(F32)
        ms = jnp.mean(x * x, axis=-1, keepdims=True)
        y = x * lax.rsqrt(ms + EPS) * gain
        if use_rope:
            y = y * cos + pltpu.roll(y, 32, 1) * sa + pltpu.roll(y, 96, 1) * sb
        return y * scale if scale != 1.0 else y

    gq, gk = gq_ref[...], gk_ref[...]
    for g in range(n_groups):
        qt_ref[g * LANES:(g + 1) * LANES, :] = prep(P_Q + g * LANES, gq, Q_SCALE).T.astype(qt_ref.dtype)
        k_ref[:, g * LANES:(g + 1) * LANES] = prep(P_K + g * LANES, gk, 1.0).astype(k_ref.dtype)
    for h in range(DA_HEADS):
        v = p_ref[:, P_V + h * DA_V_DIM:P_V + (h + 1) * DA_V_DIM].astype(F32)
        vt_ref[h] = v.T.astype(vt_ref.dtype)


def qk_prep(p, gq, gk, rope, tm):
    rows = p.shape[0]
    use_rope = rope is not None
    width = 2 * DA_HEADS * DA_HEAD_DIM
    vec = pl.BlockSpec((1, LANES), lambda i: (0, 0))
    tab = pl.BlockSpec((tm, LANES), lambda i: (i, 0))
    if use_rope:
        tabs = list(rope)
    else:
        tabs = [jnp.zeros((rows, LANES), F32)] * 3
    return pl.pallas_call(
        functools.partial(_qk_prep_body, use_rope),
        grid=(rows // tm,),
        in_specs=[pl.BlockSpec((tm, P_V + DA_WIDTH), lambda i: (i, 0)), vec, vec, tab, tab, tab],
        out_specs=[pl.BlockSpec((width, tm), lambda i: (0, i)),
                   pl.BlockSpec((tm, width), lambda i: (i, 0)),
                   pl.BlockSpec((DA_HEADS, None, DA_V_DIM, tm), lambda i: (0, i, 0, 0))],
        out_shape=[jax.ShapeDtypeStruct((width, rows), BF16),
                   jax.ShapeDtypeStruct((rows, width), BF16),
                   jax.ShapeDtypeStruct((DA_HEADS, rows // tm, DA_V_DIM, tm), BF16)],
        compiler_params=_cparams(("arbitrary",)),
        name="qk_prep",
    )(p, gq.reshape(1, LANES), gk.reshape(1, LANES), *tabs)


def _attn_body(lam_init, n_chunks, tk, has_ctx, *refs):
    if has_ctx:
        (qt_ref, k_ref, vt_ref, kc_ref, vtc_ref, lq1_ref, lk1_ref, lq2_ref, lk2_ref, g_ref,
         o_ref, m_ref, l_ref, acc_ref) = refs
    else:
        (qt_ref, k_ref, vt_ref, lq1_ref, lk1_ref, lq2_ref, lk2_ref, g_ref,
         o_ref, m_ref, l_ref, acc_ref) = refs
    d = DA_HEAD_DIM
    qt = (qt_ref[:d, :], qt_ref[d:, :])
    m_ref[...] = jnp.full_like(m_ref, -jnp.inf)
    l_ref[...] = jnp.zeros_like(l_ref)
    acc_ref[...] = jnp.zeros_like(acc_ref)

    def update(k_blk, vt_blk):
        for mi in range(2):
            s = jnp.dot(k_blk[:, mi * d:(mi + 1) * d], qt[mi], preferred_element_type=F32)
            m_old = m_ref[mi]
            m_new = jnp.maximum(m_old, jnp.max(s, axis=0, keepdims=True))
            alpha = jnp.exp2(m_old - m_new)
            p = jnp.exp2(s - m_new)
            l_ref[mi] = alpha * l_ref[mi] + jnp.sum(p, axis=0, keepdims=True)
            acc_ref[mi] = alpha * acc_ref[mi] + jnp.dot(vt_blk, p.astype(BF16), preferred_element_type=F32)
            m_ref[mi] = m_new

    if has_ctx:
        update(kc_ref[...], vtc_ref[...])

    def step(c, carry):
        off = pl.multiple_of(c * tk, tk)
        update(k_ref[pl.ds(off, tk), :], vt_ref[c])
        return carry

    lax.fori_loop(0, n_chunks, step, 0, unroll=math.gcd(n_chunks, 16))

    lam =(jnp.exp(jnp.sum(lq1_ref[...] * lk1_ref[...])) - jnp.exp(jnp.sum(lq2_ref[...] * lk2_ref[...]))
           + lam_init)
    ot = acc_ref[0] * (1.0 / l_ref[0]) - lam * (acc_ref[1] * (1.0 / l_ref[1]))
    ms = jnp.mean(ot * ot, axis=0, keepdims=True)
    o = (ot * lax.rsqrt(ms + EPS)).T
    o_ref[...] = (o * g_ref[...] * (1.0 - lam_init)).astype(o_ref.dtype)


def diff_attention(qt, k, vt, kc, vtc, lams, subln_g, lam_init, *, tq):
    t_q = qt.shape[1]
    t_k = k.shape[0]
    n_chunks, tk = vt.shape[1], vt.shape[3]
    has_ctx = kc is not None
    tq = min(tq, t_q)
    hw = 2 * DA_HEAD_DIM
    in_specs = [
        pl.BlockSpec((hw, tq), lambda h, i: (h, i)),
        pl.BlockSpec((t_k, hw), lambda h, i: (0, h)),
        pl.BlockSpec((None, n_chunks, DA_V_DIM, tk), lambda h, i: (h, 0, 0, 0)),
    ]
    args = [qt, k, vt]
    if has_ctx:
        t_c = kc.shape[0]
        in_specs += [pl.BlockSpec((t_c, hw), lambda h, i: (0, h)),
                     pl.BlockSpec((None, None, DA_V_DIM, t_c), lambda h, i: (h, 0, 0, 0))]
        args += [kc, vtc]
    vec = pl.BlockSpec((1, DA_HEAD_DIM), lambda h, i: (0, 0))
    in_specs += [vec] * 4 + [pl.BlockSpec((1, DA_V_DIM), lambda h, i: (0, 0))]
    args += [v.reshape(1, DA_HEAD_DIM) for v in lams] + [subln_g.reshape(1, DA_V_DIM)]
    return pl.pallas_call(
        functools.partial(_attn_body, lam_init, n_chunks, tk, has_ctx),
        grid=(DA_HEADS, t_q // tq),
        in_specs=in_specs,
        out_specs=pl.BlockSpec((tq, DA_V_DIM), lambda h, i: (i, h)),
        out_shape=jax.ShapeDtypeStruct((t_q, DA_WIDTH), BF16),
        scratch_shapes=[pltpu.VMEM((2, 1, tq), F32), pltpu.VMEM((2, 1, tq), F32),
                        pltpu.VMEM((2, DA_V_DIM, tq), F32)],
        compiler_params=_cparams(("parallel", "arbitrary")),
        name="diff_attention",
    )(*args)


def _dft_tables_latent(t_len):
    n2 = FFT_N2
    n1 = t_len // n2
    f1 = np.arange(n1)[:, None]
    t1 = np.arange(n1)[None, :]
    ang = 2.0 * np.pi * ((f1 * t1) % n1) / n1
    stage_a = np.concatenate([np.cos(ang), np.sin(ang)], axis=0)
    f2 = np.arange(n2)[None, :, None]
    t2 = np.arange(n2)[None, None, :]
    f1b = np.arange(n1)[:, None, None]
    idx = (t2 * f2 * n1 + t2 * f1b) % t_len
    th = 2.0 * np.pi * idx / t_len
    mr, mi = np.cos(th), -np.sin(th)
    stage_b = np.concatenate([np.concatenate([mr, mi], axis=2), np.concatenate([mi, -mr], axis=2)], axis=1)
    return stage_a.astype(np.float32), stage_b.astype(np.float32)


def _dft_table_channels(t_len, sign):
    c = np.arange(FT_GROUP_DIM)
    ang = 2.0 * np.pi * ((c[:, None] * c[None, :]) % FT_GROUP_DIM) / FT_GROUP_DIM
    scale = 1.0 / math.sqrt(t_len * FT_GROUP_DIM)
    return (np.concatenate([np.cos(ang), sign * np.sin(ang)], axis=0) * scale).astype(np.float32)


def _fft_a_body(cs_ref, z_ref, p_ref, q_ref):
    n1 = z_ref.shape[0]
    r = jnp.dot(cs_ref[...], z_ref[...], preferred_element_type=F32)
    p_ref[...] = r[:n1].astype(p_ref.dtype)
    q_ref[...] = r[n1:].astype(q_ref.dtype)


def _fft_b_body(m_ref, p_ref, q_ref, ch_ref, o_ref):
    n2 = FFT_N2
    pq = jnp.concatenate([p_ref[...], q_ref[...]], axis=0)
    r = jnp.dot(m_ref[...], pq, preferred_element_type=F32)
    ch = ch_ref[...]
    for g in range(FT_GROUPS):
        c0 = g * FT_GROUP_DIM
        rg = jnp.concatenate([r[:n2, c0:c0 + FT_GROUP_DIM], r[n2:, c0:c0 + FT_GROUP_DIM]], axis=1).astype(BF16)
        o_ref[:, c0:c0 + FT_GROUP_DIM] = jnp.dot(rg, ch, preferred_element_type=F32).astype(o_ref.dtype)


def fourier_mix_latent(z, tn=8192):
    t_len, width = z.shape
    n2 = FFT_N2
    n1 = t_len // n2
    ta, tb = _dft_tables_latent(t_len)
    cs = jnp.asarray(ta).astype(BF16)
    mt = jnp.asarray(tb).astype(BF16)
    ch = jnp.asarray(_dft_table_channels(t_len, 1.0)).astype(BF16)
    cols = n2 * width
    tn = min(tn, cols)
    zv = z.reshape(n1, cols)
    pv, qv = pl.pallas_call(
        _fft_a_body,
        grid=(cols // tn,),
        in_specs=[pl.BlockSpec((2 * n1, n1), lambda j: (0, 0)), pl.BlockSpec((n1, tn), lambda j: (0, j))],
        out_specs=[pl.BlockSpec((n1, tn), lambda j: (0, j))] * 2,
        out_shape=[jax.ShapeDtypeStruct((n1, cols), BF16)] * 2,
        compiler_params=_cparams(("arbitrary",)),
        name="fft_stage_a",
    )(cs, zv)
    pm = pv.reshape(t_len, width)
    qm = qv.reshape(t_len, width)
    out = pl.pallas_call(
        _fft_b_body,
        grid=(n1,),
        in_specs=[
            pl.BlockSpec((None, 2 * n2, 2 * n2), lambda f: (f, 0, 0)),
            pl.BlockSpec((n2, width), lambda f: (f, 0)),
            pl.BlockSpec((n2, width), lambda f: (f, 0)),
            pl.BlockSpec((2 * FT_GROUP_DIM, FT_GROUP_DIM), lambda f: (0, 0)),
        ],
        out_specs=pl.BlockSpec((n2, width), lambda f: (0, f)),
        out_shape=jax.ShapeDtypeStruct((n2, n1 * width), BF16),
        compiler_params=_cparams(("arbitrary",)),
        name="fft_stage_b",
    )(mt, pm, qm, ch)
    return out.reshape(t_len, width)


def _fft_ctx_body(cs_ref, z_ref, ch_ref, o_ref):
    t_len = z_ref.shape[0]
    r = jnp.dot(cs_ref[...], z_ref[...], preferred_element_type=F32)
    ch = ch_ref[...]
    for g in range(FT_GROUPS):
        c0 = g * FT_GROUP_DIM
        rg = jnp.concatenate([r[:t_len, c0:c0 + FT_GROUP_DIM], r[t_len:, c0:c0 + FT_GROUP_DIM]], axis=1).astype(BF16)
        o_ref[:, c0:c0 + FT_GROUP_DIM] = jnp.dot(rg, ch, preferred_element_type=F32).astype(o_ref.dtype)


def fourier_mix_short(z):
    t_len, width = z.shape
    f = np.arange(t_len)
    ang = 2.0 * np.pi * ((f[:, None] * f[None, :]) % t_len) / t_len
    cs = jnp.asarray(np.concatenate([np.cos(ang), np.sin(ang)], axis=0).astype(np.float32)).astype(BF16)
    ch = jnp.asarray(_dft_table_channels(t_len, -1.0)).astype(BF16)
    return pl.pallas_call(
        _fft_ctx_body,
        out_shape=jax.ShapeDtypeStruct((t_len, width), BF16),
        compiler_params=pltpu.CompilerParams(vmem_limit_bytes=V7X_VMEM_LIMIT_BYTES),
        name="fft_short",
    )(cs, z, ch)


def _split3(x):
    hi = x.astype(BF16)
    r1 = x - hi.astype(F32)
    mid = r1.astype(BF16)
    lo = (r1 - mid.astype(F32)).astype(BF16)
    return hi, mid, lo


def _gla_body(reverse, q_ref, k_ref, v_ref, gd_ref, qc_ref, kc_ref, vc_ref, gdc_ref, w2_ref, b2_ref,
              o_ref, st_ref):
    n = pl.program_id(0)
    rows = GLA_ROWS
    ch = GLA_CHUNK
    n_chunks = rows // ch
    dk, dv = GLA_K_DIM, GLA_V_DIM

    @pl.when(n == 0)
    def _():
        st_ref[...] = jnp.zeros_like(st_ref)

    is_ctx = n == 0
    q = jnp.where(is_ctx, qc_ref[...], q_ref[...])
    k = jnp.where(is_ctx, kc_ref[...], k_ref[...])
    v = jnp.where(is_ctx, vc_ref[...], v_ref[...])
    gd = jnp.where(is_ctx, gdc_ref[...], gd_ref[...])

    logits = jnp.dot(gd, w2_ref[...], preferred_element_type=F32) + b2_ref[...]
    log_a = (jnp.minimum(logits, 0.0) - jnp.log(1.0 + jnp.exp(-jnp.abs(logits)))) / GLA_TAU

    ri = lax.broadcasted_iota(jnp.int32, (rows, rows), 0)
    ci = lax.broadcasted_iota(jnp.int32, (rows, rows), 1)
    same = (ri // ch) == (ci // ch)
    tri = jnp.logical_and(same, (ci >= ri) if reverse else (ci <= ri))
    tri_b = jnp.where(tri, 1.0, 0.0).astype(BF16)
    hi, mid, lo = _split3(log_a)
    cum = (jnp.dot(tri_b, hi, preferred_element_type=F32) + jnp.dot(tri_b, mid, preferred_element_type=F32)
           + jnp.dot(tri_b, lo, preferred_element_type=F32))
    e_pos = jnp.exp(cum)
    e_neg = jnp.exp(-cum)
    order = list(range(n_chunks))[::-1] if reverse else list(range(n_chunks))
    last_row = [(c * ch) if reverse else (c * ch + ch - 1) for c in range(n_chunks)]
    scale = dk ** -0.5

    for h in range(GLA_HEADS):
        ks = slice(h * dk, (h + 1) * dk)
        vs = slice(h * dv, (h + 1) * dv)
        qf = q[:, ks].astype(F32) * scale
        kf = k[:, ks].astype(F32)
        vh = v[:, vs]
        q_in = (qf * e_pos[:, ks]).astype(BF16)
        k_in = (kf * e_neg[:, ks]).astype(BF16)
        a = lax.dot_general(q_in, k_in, (((1,), (1,)), ((), ())), preferred_element_type=F32)
        a = jnp.where(tri, a, 0.0).astype(BF16)
        o_intra = jnp.dot(a, vh, preferred_element_type=F32)
        st = st_ref[h]
        for c in order:
            rs = slice(c * ch, (c + 1) * ch)
            cum_last = cum[last_row[c]:last_row[c] + 1, ks]
            o_inter = lax.dot_general(q_in[rs], st.astype(BF16), (((1,), (1,)), ((), ())),
                                      preferred_element_type=F32)
            o_ref[rs, vs] = o_intra[rs] + o_inter
            k_dec = (kf[rs] * jnp.exp(cum_last - cum[rs, ks])).astype(BF16)
            kv_t = lax.dot_general(vh[rs], k_dec, (((0,), (0,)), ((), ())), preferred_element_type=F32)
            st = jnp.exp(cum_last) * st + kv_t
        st_ref[h] = st


def gla_direction(p, gd, pc, gdc, w2pad, b2, reverse):
    t_len = p.shape[0]
    rows = GLA_ROWS
    nb = t_len // rows
    hk = GLA_HEADS * GLA_K_DIM

    if reverse:
        lat = lambda n: jnp.where(n == 0, nb - 1, nb - n)
    else:
        lat = lambda n: jnp.where(n == 0, 0, n - 1)
    out_blk = lambda n: jnp.where(n == 0, nb, lat(n))
    in_specs = [
        pl.BlockSpec((rows, hk), lambda n: (lat(n), P_GQ // hk)),
        pl.BlockSpec((rows, hk), lambda n: (lat(n), P_GK // hk)),
        pl.BlockSpec((rows, GLA_WIDTH), lambda n: (lat(n), P_GV // GLA_WIDTH)),
        pl.BlockSpec((rows, LANES), lambda n: (lat(n), 0)),
        pl.BlockSpec((rows, hk), lambda n: (0, P_GQ // hk)),
        pl.BlockSpec((rows, hk), lambda n: (0, P_GK // hk)),
        pl.BlockSpec((rows, GLA_WIDTH), lambda n: (0, P_GV // GLA_WIDTH)),
        pl.BlockSpec((rows, LANES), lambda n: (0, 0)),
        pl.BlockSpec((LANES, hk), lambda n: (0, 0)),
        pl.BlockSpec((1, hk), lambda n: (0, 0)),
    ]
    return pl.pallas_call(
        functools.partial(_gla_body, reverse),
        grid=(nb + 1,),
        in_specs=in_specs,
        out_specs=pl.BlockSpec((rows, GLA_WIDTH), lambda n: (out_blk(n), 0)),
        out_shape=jax.ShapeDtypeStruct((t_len + rows, GLA_WIDTH), F32),
        scratch_shapes=[pltpu.VMEM((GLA_HEADS, GLA_V_DIM, GLA_K_DIM), F32)],
        compiler_params=_cparams(("arbitrary",)),
        name="gla_bwd" if reverse else "gla_fwd",
    )(p, p, p, gd, pc, pc, pc, gdc, w2pad, b2.reshape(1, hk))


def _gla_out_body(of_ref, ob_ref, r_ref, g_ref, y_ref):
    g = g_ref[...]
    for h in range(GLA_HEADS):
        vs = slice(h * GLA_V_DIM, (h + 1) * GLA_V_DIM)
        o = of_ref[:, vs] + ob_ref[:, vs]
        ms = jnp.mean(o * o, axis=-1, keepdims=True)
        r = r_ref[:, vs].astype(F32)
        y_ref[:, vs] = (o * lax.rsqrt(ms + EPS) * g * (r * _sigmoid(r))).astype(y_ref.dtype)


def gla_output(o_f, o_b, r, g, row_block0, tm=256):
    n_rows = r.shape[0]
    o_spec = pl.BlockSpec((tm, GLA_WIDTH), lambda i: (row_block0 + i, 0))
    return pl.pallas_call(
        _gla_out_body,
        grid=(n_rows // tm,),
        in_specs=[o_spec, o_spec, pl.BlockSpec((tm, GLA_WIDTH), lambda i: (i, 0)),
                  pl.BlockSpec((1, GLA_V_DIM), lambda i: (0, 0))],
        out_specs=pl.BlockSpec((tm, GLA_WIDTH), lambda i: (i, 0)),
        out_shape=jax.ShapeDtypeStruct((n_rows, GLA_WIDTH), BF16),
        compiler_params=_cparams(("arbitrary",)),
        name="gla_output",
    )(o_f, o_b, r, g.reshape(1, GLA_V_DIM))


FFN_HALO = 16


def _ffn_up_body(n_row_tiles, a_ref, ap_ref, an_ref, wg_ref, wv_ref, cwg_ref, cwv_ref, cbg_ref, cbv_ref,
                 o_ref, ext_ref):
    i = pl.program_id(0)
    tm = a_ref.shape[0]
    hl = FFN_HALO

    @pl.when(pl.program_id(1) == 0)
    def _():
        ext_ref[hl:hl + tm, :] = a_ref[...]
        zero = jnp.zeros_like(ap_ref)
        ext_ref[0:hl, :] = jnp.where(i > 0, ap_ref[...], zero)
        ext_ref[hl + tm:, :] = jnp.where(i < n_row_tiles - 1, an_ref[...], zero)

    ext = ext_ref[...]

    def conv(w_ref, cw_ref, cb_ref):
        u = jnp.dot(ext, w_ref[...].astype(BF16), preferred_element_type=F32)
        c = (pltpu.roll(u, 1, 0) * cw_ref[0:1, :] + u * cw_ref[1:2, :]
             + pltpu.roll(u, tm + 2 * hl - 1, 0) * cw_ref[2:3, :])
        return c[hl:hl + tm] + cb_ref[...]

    gate = conv(wg_ref, cwg_ref, cbg_ref)
    val = conv(wv_ref, cwv_ref, cbv_ref)
    o_ref[...] = (gate * _sigmoid(gate) * val).astype(o_ref.dtype)


def ffn_up_gated(h, w_up, conv_w, conv_b, layer, *, tm, tn):
    m, k = h.shape
    f = w_up.shape[2] // 2
    tm = min(tm, m)
    nt = m // tm
    nc = f // tn
    hb = tm // FFN_HALO
    last = m // FFN_HALO - 1
    gate_half = lambda rows: pl.BlockSpec((None, rows, tn), lambda i, j: (layer, 0, j))
    val_half = lambda rows: pl.BlockSpec((None, rows, tn), lambda i, j: (layer, 0, j + nc))
    cb = conv_b.reshape(conv_b.shape[0], 1, 2 * f)
    return pl.pallas_call(
        functools.partial(_ffn_up_body, nt),
        grid=(nt, nc),
        in_specs=[
            pl.BlockSpec((tm, k), lambda i, j: (i, 0), pipeline_mode=pl.Buffered(1)),
            pl.BlockSpec((FFN_HALO, k), lambda i, j: (jnp.maximum(i * hb - 1, 0), 0)),
            pl.BlockSpec((FFN_HALO, k), lambda i, j: (jnp.minimum((i + 1) * hb, last), 0)),
            gate_half(k), val_half(k), gate_half(3), val_half(3), gate_half(1), val_half(1),
        ],
        out_specs=pl.BlockSpec((tm, tn), lambda i, j: (i, j)),
        out_shape=jax.ShapeDtypeStruct((m, f), BF16),
        scratch_shapes=[pltpu.VMEM((tm + 2 * FFN_HALO, k), BF16)],
        compiler_params=_cparams(("arbitrary", "arbitrary")),
        name="ffn_up_gated",
    )(h, h, h, w_up, w_up, conv_w, conv_w, cb, cb)


def _rope_tables(t_len):
    rows = t_len // GRID_W
    row = jnp.repeat(jnp.arange(rows), GRID_W).astype(F32)
    col = jnp.tile(jnp.arange(GRID_W), rows).astype(F32)
    half = DA_HEAD_DIM // 2
    freqs = ROPE_BASE ** (-jnp.arange(0, half, 2, dtype=F32) / half)
    ar = row[:, None] * freqs
    ac = col[:, None] * freqs
    ang = jnp.concatenate([ar, ar, ac, ac], axis=-1)
    cos, sin = jnp.cos(ang), jnp.sin(ang)
    hi = (jnp.arange(DA_HEAD_DIM) % 64) >= 32
    return cos, jnp.where(hi, sin, 0.0), jnp.where(hi, 0.0, -sin)


def _ffn(x, h, w_up, conv_w, conv_b, w_down_bf16, l, gate):
    act = ffn_up_gated(h, w_up, conv_w, conv_b, l, tm=1024, tn=256)
    return matmul_ksplit_res(act, w_down_bf16, l, x, gate, tm=512, tn=1024, tk=D_FF // 2, name="ffn_down")


def _in_proj(h, w_in_t, l, w_r, w_gd, suffix):
    skip_ft = (W_GQ - W_FT) // 512
    p = matmul_nt(h, w_in_t, layer=l, tm=2048, tn=512, n_out=P_WIDTH,
                  row_map=lambda j: jnp.where(j < W_FT // 512, j, j + skip_ft), name="in_proj" + suffix)
    z = matmul_nt(h, w_in_t, layer=l, tm=2048, tn=512, n_out=FT_WIDTH, row_map=lambda j: j + W_FT // 512,
                  name="in_proj_ft" + suffix)
    r = matmul_nt(h, w_r, tm=2048, tn=512, name="in_proj_r" + suffix)
    gd = matmul_nt(h, w_gd, tm=2048, tn=LANES, name="in_proj_gd" + suffix)
    return p, z, r, gd


def kernel(x, c, ctx, c_ctx, w_ada, b_ada, norm1_g, norm2_g, w_in, q_norm_g, k_norm_g, lambda_q1, lambda_k1,
           lambda_q2, lambda_k2, da_subln_g, gla_gate_w2, gla_gate_b, gla_norm_g, w_out, w_up, conv_w, conv_b,
           w_down):
    depth = w_ada.shape[0]
    t_len = x.shape[1]
    d = D_MODEL
    xl = x[0]
    xc = ctx[0]
    rope = _rope_tables(t_len)
    w_in_t = jnp.swapaxes(w_in, 1, 2)
    w_down_bf16 = w_down.astype(BF16)

    cvec = jnp.zeros((8, d), F32).at[0].set(c[0]).at[1].set(c_ctx)
    mod = adaln(cvec, w_ada, b_ada)

    for l in range(depth):
        need_ctx = l < depth - 1
        lam_init = 0.8 - 0.6 * math.exp(-0.3 * l)
        mods_l = [mod[l, 0, i * d:(i + 1) * d] for i in range(N_MOD)]
        mods_c = [mod[l, 1, i * d:(i + 1) * d] for i in range(N_MOD)]

        w_r = w_in_t[l, W_GR:W_GR + GLA_WIDTH, :]
        w_gd = jnp.pad(w_in_t[l, W_GD:W_GR, :], ((0, LANES - 2 * GLA_GATE_RANK), (0, 0)))

        h_l = modulate(xl, norm1_g[l], mods_l[0], mods_l[1])
        h_c = modulate(xc, norm1_g[l], mods_c[0], mods_c[1])
        p_l, z_l, r_l, gd_l = _in_proj(h_l, w_in_t, l, w_r, w_gd, "")
        p_c, z_c, r_c, gd_c = _in_proj(h_c, w_in_t, l, w_r, w_gd, "_ctx")

        qt_l, k_l, vt_l = qk_prep(p_l, q_norm_g[l], k_norm_g[l], rope, ATTN_TK)
        qt_c, k_c, vt_c = qk_prep(p_c, q_norm_g[l], k_norm_g[l], None, xc.shape[0])
        lams = (lambda_q1[l], lambda_k1[l], lambda_q2[l], lambda_k2[l])
        da_l = diff_attention(qt_l, k_l, vt_l, k_c, vt_c, lams, da_subln_g[l], lam_init, tq=256)

        ft_l = fourier_mix_latent(z_l)

        w2 = gla_gate_w2[l]
        zpad = jnp.zeros((LANES - 2 * GLA_GATE_RANK, w2.shape[-1]), F32)
        zr = jnp.zeros((GLA_GATE_RANK, w2.shape[-1]), F32)
        w2_f = jnp.concatenate([w2[0], zr, zpad], axis=0).astype(BF16)
        w2_b = jnp.concatenate([zr, w2[1], zpad], axis=0).astype(BF16)
        o_f = gla_direction(p_l, gd_l, p_c, gd_c, w2_f, gla_gate_b[l, 0], False)
        o_b = gla_direction(p_l, gd_l, p_c, gd_c, w2_b, gla_gate_b[l, 1], True)
        gla_l = gla_output(o_f, o_b, r_l, gla_norm_g[l], 0)

        x_new = out_proj(da_l, ft_l, gla_l, w_out, l, xl, mods_l[2], tm=1024, tn=512)
        h2_l = modulate(x_new, norm2_g[l], mods_l[3], mods_l[4])
        x_new = _ffn(x_new, h2_l, w_up, conv_w, conv_b, w_down_bf16, l, mods_l[5])

        if need_ctx:
            da_c = diff_attention(qt_c, k_c, vt_c, None, None, lams, da_subln_g[l], lam_init, tq=256)
            ft_c = fourier_mix_short(z_c)
            gla_c = gla_output(o_f, o_b, r_c, gla_norm_g[l], t_len // 256)
            xc_new = out_proj(da_c, ft_c, gla_c, w_out, l, xc, mods_c[2], tm=1024, tn=512, name="out_proj_ctx")
            h2_c = modulate(xc_new, norm2_g[l], mods_c[3], mods_c[4])
            xc = _ffn(xc_new, h2_c, w_up, conv_w, conv_b, w_down_bf16, l, mods_c[5])
        xl = x_new

    return xl[None]
```

```python
import functools
import math

import numpy as np
import jax
import jax.numpy as jnp
from jax import lax
from jax.experimental import pallas as pl
from jax.experimental.pallas import tpu as pltpu

F32 = jnp.float32
BF16 = jnp.bfloat16

D_MODEL = 4096
CTX_LEN = 256
GRID_W = 64
EPS = 1e-6
ROPE_BASE = 10000.0
DA_HEAD_DIM = 128
DA_HEADS = 8
DA_V_DIM = 256
DA_WIDTH = DA_HEADS * DA_V_DIM
FT_GROUPS = 4
FT_GROUP_DIM = 256
FT_WIDTH = FT_GROUPS * FT_GROUP_DIM
GLA_HEADS = 4
GLA_V_DIM = 256
GLA_K_DIM = 128
GLA_WIDTH = GLA_HEADS * GLA_V_DIM
GLA_GATE_RANK = 16
GLA_TAU = 16.0
GLA_CHUNK = 64
D_FF = 11008
N_MOD = 6

P_Q, P_K, P_V, P_FT, P_GQ, P_GK, P_GV = 0, 2048, 4096, 6144, 7168, 7680, 8192
P_WIDTH = 9216
W_GD, W_GR = 9216, 9248
RG_WIDTH = 1152
RG_TN = 384

V7X_VMEM_LIMIT_BYTES = 56 * 1024 * 1024
LANES = 128
FFT_N2 = 128
GLA_ROWS = 256
ATTN_TK = 512
ATTN_TQ_SUB = 256
Q_SCALE = DA_HEAD_DIM ** -0.5 * math.log2(math.e)


def _cparams(sem):
    return pltpu.CompilerParams(dimension_semantics=sem, vmem_limit_bytes=V7X_VMEM_LIMIT_BYTES)


def _sigmoid(x):
    return 1.0 / (1.0 + jnp.exp(-x))


def _adaln_body(c_ref, w_ref, b_ref, o_ref):
    c = c_ref[...]
    s = (c * _sigmoid(c)).astype(BF16)
    w = w_ref[...].astype(BF16)
    o_ref[...] = jnp.dot(s, w, preferred_element_type=F32) + b_ref[...]


def adaln(cvec, w_ada, b_ada, tn=512):
    depth, d, n = w_ada.shape
    return pl.pallas_call(
        _adaln_body,
        grid=(depth, n // tn),
        in_specs=[
            pl.BlockSpec((8, d), lambda l, j: (0, 0)),
            pl.BlockSpec((None, d, tn), lambda l, j: (l, 0, j)),
            pl.BlockSpec((None, 1, tn), lambda l, j: (l, 0, j)),
        ],
        out_specs=pl.BlockSpec((None, 8, tn), lambda l, j: (l, 0, j)),
        out_shape=jax.ShapeDtypeStruct((depth, 8, n), F32),
        compiler_params=_cparams(("arbitrary", "arbitrary")),
        name="adaln",
    )(cvec, w_ada, b_ada.reshape(depth, 1, n))


def _modulate_body(x_ref, g_ref, sh_ref, sc_ref, o_ref):
    x = x_ref[...]
    ms = jnp.mean(x * x, axis=-1, keepdims=True)
    y = x * lax.rsqrt(ms + EPS) * g_ref[...]
    o_ref[...] = (y * (1.0 + sc_ref[...]) + sh_ref[...]).astype(o_ref.dtype)


def modulate(x, g, shift, scale, tm=256):
    m, d = x.shape
    vec = pl.BlockSpec((1, d), lambda i: (0, 0))
    return pl.pallas_call(
        _modulate_body,
        grid=(m // tm,),
        in_specs=[pl.BlockSpec((tm, d), lambda i: (i, 0)), vec, vec, vec],
        out_specs=pl.BlockSpec((tm, d), lambda i: (i, 0)),
        out_shape=jax.ShapeDtypeStruct((m, d), BF16),
        compiler_params=_cparams(("arbitrary",)),
        name="modulate",
    )(x, g.reshape(1, d), shift.reshape(1, d), scale.reshape(1, d))


def _mm_nt_body(a_ref, w_ref, o_ref):
    w = w_ref[...].astype(BF16)
    acc = lax.dot_general(a_ref[...], w, (((1,), (1,)), ((), ())), preferred_element_type=F32)
    o_ref[...] = acc.astype(o_ref.dtype)


def matmul_nt(a, wt, *, tm, tn, out_dtype=BF16, layer=None, n_out=None, row_map=None, name="matmul_nt"):
    m, k = a.shape
    n = n_out if n_out is not None else wt.shape[-2]
    tm = min(tm, m)
    rm = row_map if row_map is not None else (lambda j: j)
    if wt.ndim == 3:
        w_spec = pl.BlockSpec((None, tn, k), lambda i, j: (layer, rm(j), 0))
    else:
        w_spec = pl.BlockSpec((tn, k), lambda i, j: (rm(j), 0))
    return pl.pallas_call(
        _mm_nt_body,
        grid=(m // tm, n // tn),
        in_specs=[pl.BlockSpec((tm, k), lambda i, j: (i, 0), pipeline_mode=pl.Buffered(1)), w_spec],
        out_specs=pl.BlockSpec((tm, tn), lambda i, j: (i, j)),
        out_shape=jax.ShapeDtypeStruct((m, n), out_dtype),
        compiler_params=_cparams(("arbitrary", "arbitrary")),
        name=name,
    )(a, wt)


def _out_proj_body(da_ref, ft_ref, gla_ref, w_ref, r_ref, g_ref, o_ref, mix_ref):
    @pl.when(pl.program_id(1) == 0)
    def _():
        mix_ref[:, :DA_WIDTH] = da_ref[...]
        mix_ref[:, DA_WIDTH:DA_WIDTH + FT_WIDTH] = ft_ref[...].astype(BF16)
        mix_ref[:, DA_WIDTH + FT_WIDTH:] = gla_ref[...]

    acc = jnp.dot(mix_ref[...], w_ref[...].astype(BF16), preferred_element_type=F32)
    o_ref[...] = r_ref[...] + g_ref[...] * acc


def out_proj(da, ft, gla, w_out, layer, res, gate, *, tm, tn, name="out_proj"):
    m = da.shape[0]
    k, n = w_out.shape[1:]
    tm = min(tm, m)
    once = pl.Buffered(1)
    return pl.pallas_call(
        _out_proj_body,
        grid=(m // tm, n // tn),
        in_specs=[
            pl.BlockSpec((tm, DA_WIDTH), lambda i, j: (i, 0), pipeline_mode=once),
            pl.BlockSpec((tm, FT_WIDTH), lambda i, j: (i, 0), pipeline_mode=once),
            pl.BlockSpec((tm, GLA_WIDTH), lambda i, j: (i, 0), pipeline_mode=once),
            pl.BlockSpec((None, k, tn), lambda i, j: (layer, 0, j)),
            pl.BlockSpec((tm, tn), lambda i, j: (i, j)),
            pl.BlockSpec((1, tn), lambda i, j: (0, j)),
        ],
        out_specs=pl.BlockSpec((tm, tn), lambda i, j: (i, j)),
        out_shape=jax.ShapeDtypeStruct((m, n), F32),
        scratch_shapes=[pltpu.VMEM((tm, k), BF16)],
        compiler_params=_cparams(("arbitrary", "arbitrary")),
        name=name,
    )(da, ft, gla, w_out, res, gate.reshape(1, n))


def _mm_ksplit_res_body(a_ref, b_ref, r_ref, g_ref, o_ref, acc_ref):
    kk = pl.program_id(2)

    @pl.when(kk == 0)
    def _():
        acc_ref[...] = jnp.zeros_like(acc_ref)

    acc_ref[...] += jnp.dot(a_ref[...], b_ref[...], preferred_element_type=F32)

    @pl.when(kk == pl.num_programs(2) - 1)
    def _():
        o_ref[...] = r_ref[...] + g_ref[...] * acc_ref[...]


def matmul_ksplit_res(a, b, layer, res, gate, *, tm, tn, tk, name="matmul_ksplit"):
    m, k = a.shape
    n = b.shape[2]
    tm = min(tm, m)
    return pl.pallas_call(
        _mm_ksplit_res_body,
        grid=(m // tm, n // tn, k // tk),
        in_specs=[
            pl.BlockSpec((tm, tk), lambda i, j, kk: (i, kk)),
            pl.BlockSpec((None, tk, tn), lambda i, j, kk: (layer, kk, j)),
            pl.BlockSpec((tm, tn), lambda i, j, kk: (i, j)),
            pl.BlockSpec((1, tn), lambda i, j, kk: (0, j)),
        ],
        out_specs=pl.BlockSpec((tm, tn), lambda i, j, kk: (i, j)),
        out_shape=jax.ShapeDtypeStruct((m, n), F32),
        scratch_shapes=[pltpu.VMEM((tm, tn), F32)],
        compiler_params=_cparams(("parallel", "arbitrary", "arbitrary")),
        name=name,
    )(a, b, res, gate.reshape(1, n))


def _qk_prep_body(use_rope, p_ref, gq_ref, gk_ref, cos_ref, sa_ref, sb_ref, qt_ref, k_ref, vt_ref):
    n_groups = (2 * DA_HEADS * DA_HEAD_DIM) // LANES
    if use_rope:
        cos, sa, sb = cos_ref[...], sa_ref[...], sb_ref[...]

    def prep(col, gain, scale):
        x = p_ref[:, col:col + LANES].astype(F32)
        ms = jnp.mean(x * x, axis=-1, keepdims=True)
        y = x * lax.rsqrt(ms + EPS) * gain
        if use_rope:
            y = y * cos + pltpu.roll(y, 32, 1) * sa + pltpu.roll(y, 96, 1) * sb
        return y * scale if scale != 1.0 else y

    gq, gk = gq_ref[...], gk_ref[...]
    for g in range(n_groups):
        qt_ref[g * LANES:(g + 1) * LANES, :] = prep(P_Q + g * LANES, gq, Q_SCALE).T.astype(qt_ref.dtype)
        k_ref[:, g * LANES:(g + 1) * LANES] = prep(P_K + g * LANES, gk, 1.0).astype(k_ref.dtype)
    for h in range(DA_HEADS):
        v = p_ref[:, P_V + h * DA_V_DIM:P_V + (h + 1) * DA_V_DIM].astype(F32)
        vt_ref[h] = v.T.astype(vt_ref.dtype)


def qk_prep(p, gq, gk, rope, tm):
    rows = p.shape[0]
    use_rope = rope is not None
    width = 2 * DA_HEADS * DA_HEAD_DIM
    vec = pl.BlockSpec((1, LANES), lambda i: (0, 0))
    tab = pl.BlockSpec((tm, LANES), lambda i: (i, 0))
    if use_rope:
        tabs = list(rope)
    else:
        tabs = [jnp.zeros((rows, LANES), F32)] * 3
    return pl.pallas_call(
        functools.partial(_qk_prep_body, use_rope),
        grid=(rows // tm,),
        in_specs=[pl.BlockSpec((tm, P_V + DA_WIDTH), lambda i: (i, 0)), vec, vec, tab, tab, tab],
        out_specs=[pl.BlockSpec((width, tm), lambda i: (0, i)),
                   pl.BlockSpec((tm, width), lambda i: (i, 0)),
                   pl.BlockSpec((DA_HEADS, None, DA_V_DIM, tm), lambda i: (0, i, 0, 0))],
        out_shape=[jax.ShapeDtypeStruct((width, rows), BF16),
                   jax.ShapeDtypeStruct((rows, width), BF16),
                   jax.ShapeDtypeStruct((DA_HEADS, rows // tm, DA_V_DIM, tm), BF16)],
        compiler_params=_cparams(("arbitrary",)),
        name="qk_prep",
    )(p, gq.reshape(1, LANES), gk.reshape(1, LANES), *tabs)


def _attn_body(lam_init, n_chunks, tk, has_ctx, *refs):
    if has_ctx:
        (qt_ref, k_ref, vt_ref, kc_ref, vtc_ref, lq1_ref, lk1_ref, lq2_ref, lk2_ref, g_ref,
         o_ref, m_ref, l_ref, acc_ref) = refs
    else:
        (qt_ref, k_ref, vt_ref, lq1_ref, lk1_ref, lq2_ref, lk2_ref, g_ref,
         o_ref, m_ref, l_ref, acc_ref) = refs
    d = DA_HEAD_DIM
    qt = (qt_ref[:d, :], qt_ref[d:, :])
    m_ref[...] = jnp.full_like(m_ref, -jnp.inf)
    l_ref[...] = jnp.zeros_like(l_ref)
    acc_ref[...] = jnp.zeros_like(acc_ref)

    def update(k_blk, vt_blk):
        for sb in range(qt_ref.shape[1] // ATTN_TQ_SUB):
            ls = slice(sb * ATTN_TQ_SUB, (sb + 1) * ATTN_TQ_SUB)
            for mi in range(2):
                s = jnp.dot(k_blk[:, mi * d:(mi + 1) * d], qt[mi][:, ls],
                            preferred_element_type=F32)
                m_old = m_ref[mi, :, ls]
                m_new = jnp.maximum(m_old, jnp.max(s, axis=0, keepdims=True))
                alpha = jnp.exp2(m_old - m_new)
                p = jnp.exp2(s - m_new)
                l_ref[mi, :, ls] = alpha * l_ref[mi, :, ls] + jnp.sum(p, axis=0, keepdims=True)
                acc_ref[mi, :, ls] = (alpha * acc_ref[mi, :, ls]
                                      + jnp.dot(vt_blk, p.astype(BF16), preferred_element_type=F32))
                m_ref[mi, :, ls] = m_new

    if has_ctx:
        update(kc_ref[...], vtc_ref[...])

    def step(c, carry):
        off = pl.multiple_of(c * tk, tk)
        update(k_ref[pl.ds(off, tk), :], vt_ref[c])
        return carry

    lax.fori_loop(0, n_chunks, step, 0, unroll=math.gcd(n_chunks, 16))

    lam =(jnp.exp(jnp.sum(lq1_ref[...] * lk1_ref[...])) - jnp.exp(jnp.sum(lq2_ref[...] * lk2_ref[...]))
           + lam_init)
    ot = acc_ref[0] * (1.0 / l_ref[0]) - lam * (acc_ref[1] * (1.0 / l_ref[1]))
    ms = jnp.mean(ot * ot, axis=0, keepdims=True)
    o = (ot * lax.rsqrt(ms + EPS)).T
    o_ref[...] = (o * g_ref[...] * (1.0 - lam_init)).astype(o_ref.dtype)


def diff_attention(qt, k, vt, kc, vtc, lams, subln_g, lam_init, *, tq):
    t_q = qt.shape[1]
    t_k = k.shape[0]
    n_chunks, tk = vt.shape[1], vt.shape[3]
    has_ctx = kc is not None
    tq = min(tq, t_q)
    hw = 2 * DA_HEAD_DIM
    in_specs = [
        pl.BlockSpec((hw, tq), lambda h, i: (h, i)),
        pl.BlockSpec((t_k, hw), lambda h, i: (0, h)),
        pl.BlockSpec((None, n_chunks, DA_V_DIM, tk), lambda h, i: (h, 0, 0, 0)),
    ]
    args = [qt, k, vt]
    if has_ctx:
        t_c = kc.shape[0]
        in_specs += [pl.BlockSpec((t_c, hw), lambda h, i: (0, h)),
                     pl.BlockSpec((None, None, DA_V_DIM, t_c), lambda h, i: (h, 0, 0, 0))]
        args += [kc, vtc]
    vec = pl.BlockSpec((1, DA_HEAD_DIM), lambda h, i: (0, 0))
    in_specs += [vec] * 4 + [pl.BlockSpec((1, DA_V_DIM), lambda h, i: (0, 0))]
    args += [v.reshape(1, DA_HEAD_DIM) for v in lams] + [subln_g.reshape(1, DA_V_DIM)]
    return pl.pallas_call(
        functools.partial(_attn_body, lam_init, n_chunks, tk, has_ctx),
        grid=(DA_HEADS, t_q // tq),
        in_specs=in_specs,
        out_specs=pl.BlockSpec((tq, DA_V_DIM), lambda h, i: (i, h)),
        out_shape=jax.ShapeDtypeStruct((t_q, DA_WIDTH), BF16),
        scratch_shapes=[pltpu.VMEM((2, 1, tq), F32), pltpu.VMEM((2, 1, tq), F32),
                        pltpu.VMEM((2, DA_V_DIM, tq), F32)],
        compiler_params=_cparams(("parallel", "arbitrary")),
        name="diff_attention",
    )(*args)


def _dft_tables_latent(t_len):
    n2 = FFT_N2
    n1 = t_len // n2
    f1 = np.arange(n1)[:, None]
    t1 = np.arange(n1)[None, :]
    ang = 2.0 * np.pi * ((f1 * t1) % n1) / n1
    stage_a = np.concatenate([np.cos(ang), np.sin(ang)], axis=0)
    f2 = np.arange(n2)[None, :, None]
    t2 = np.arange(n2)[None, None, :]
    f1b = np.arange(n1)[:, None, None]
    idx = (t2 * f2 * n1 + t2 * f1b) % t_len
    th = 2.0 * np.pi * idx / t_len
    mr, mi = np.cos(th), -np.sin(th)
    stage_b = np.concatenate([np.concatenate([mr, mi], axis=2), np.concatenate([mi, -mr], axis=2)], axis=1)
    return stage_a.astype(np.float32), stage_b.astype(np.float32)


def _dft_table_channels(t_len, sign):
    c = np.arange(FT_GROUP_DIM)
    ang = 2.0 * np.pi * ((c[:, None] * c[None, :]) % FT_GROUP_DIM) / FT_GROUP_DIM
    scale = 1.0 / math.sqrt(t_len * FT_GROUP_DIM)
    return (np.concatenate([np.cos(ang), sign * np.sin(ang)], axis=0) * scale).astype(np.float32)


def _fft_a_body(cs_ref, z_ref, p_ref, q_ref):
    n1 = z_ref.shape[0]
    r = jnp.dot(cs_ref[...], z_ref[...], preferred_element_type=F32)
    p_ref[...] = r[:n1].astype(p_ref.dtype)
    q_ref[...] = r[n1:].astype(q_ref.dtype)


def _fft_b_body(m_ref, p_ref, q_ref, ch_ref, o_ref):
    n2 = FFT_N2
    pq = jnp.concatenate([p_ref[...], q_ref[...]], axis=0)
    r = jnp.dot(m_ref[...], pq, preferred_element_type=F32)
    ch = ch_ref[...]
    for g in range(FT_GROUPS):
        c0 = g * FT_GROUP_DIM
        rg = jnp.concatenate([r[:n2, c0:c0 + FT_GROUP_DIM], r[n2:, c0:c0 + FT_GROUP_DIM]], axis=1).astype(BF16)
        o_ref[:, c0:c0 + FT_GROUP_DIM] = jnp.dot(rg, ch, preferred_element_type=F32).astype(o_ref.dtype)


def fourier_mix_latent(z, tn=8192):
    t_len, width = z.shape
    n2 = FFT_N2
    n1 = t_len // n2
    ta, tb = _dft_tables_latent(t_len)
    cs = jnp.asarray(ta).astype(BF16)
    mt = jnp.asarray(tb).astype(BF16)
    ch = jnp.asarray(_dft_table_channels(t_len, 1.0)).astype(BF16)
    cols = n2 * width
    tn = min(tn, cols)
    zv = z.reshape(n1, cols)
    pv, qv = pl.pallas_call(
        _fft_a_body,
        grid=(cols // tn,),
        in_specs=[pl.BlockSpec((2 * n1, n1), lambda j: (0, 0)), pl.BlockSpec((n1, tn), lambda j: (0, j))],
        out_specs=[pl.BlockSpec((n1, tn), lambda j: (0, j))] * 2,
        out_shape=[jax.ShapeDtypeStruct((n1, cols), BF16)] * 2,
        compiler_params=_cparams(("arbitrary",)),
        name="fft_stage_a",
    )(cs, zv)
    pm = pv.reshape(t_len, width)
    qm = qv.reshape(t_len, width)
    out = pl.pallas_call(
        _fft_b_body,
        grid=(n1,),
        in_specs=[
            pl.BlockSpec((None, 2 * n2, 2 * n2), lambda f: (f, 0, 0)),
            pl.BlockSpec((n2, width), lambda f: (f, 0)),
            pl.BlockSpec((n2, width), lambda f: (f, 0)),
            pl.BlockSpec((2 * FT_GROUP_DIM, FT_GROUP_DIM), lambda f: (0, 0)),
        ],
        out_specs=pl.BlockSpec((n2, width), lambda f: (0, f)),
        out_shape=jax.ShapeDtypeStruct((n2, n1 * width), BF16),
        compiler_params=_cparams(("arbitrary",)),
        name="fft_stage_b",
    )(mt, pm, qm, ch)
    return out.reshape(t_len, width)


def _fft_ctx_body(cs_ref, z_ref, ch_ref, o_ref):
    t_len = z_ref.shape[0]
    r = jnp.dot(cs_ref[...], z_ref[...], preferred_element_type=F32)
    ch = ch_ref[...]
    for g in range(FT_GROUPS):
        c0 = g * FT_GROUP_DIM
        rg = jnp.concatenate([r[:t_len, c0:c0 + FT_GROUP_DIM], r[t_len:, c0:c0 + FT_GROUP_DIM]], axis=1).astype(BF16)
        o_ref[:, c0:c0 + FT_GROUP_DIM] = jnp.dot(rg, ch, preferred_element_type=F32).astype(o_ref.dtype)


def fourier_mix_short(z):
    t_len, width = z.shape
    f = np.arange(t_len)
    ang = 2.0 * np.pi * ((f[:, None] * f[None, :]) % t_len) / t_len
    cs = jnp.asarray(np.concatenate([np.cos(ang), np.sin(ang)], axis=0).astype(np.float32)).astype(BF16)
    ch = jnp.asarray(_dft_table_channels(t_len, -1.0)).astype(BF16)
    return pl.pallas_call(
        _fft_ctx_body,
        out_shape=jax.ShapeDtypeStruct((t_len, width), BF16),
        compiler_params=pltpu.CompilerParams(vmem_limit_bytes=V7X_VMEM_LIMIT_BYTES),
        name="fft_short",
    )(cs, z, ch)


def _split3(x):
    hi = x.astype(BF16)
    r1 = x - hi.astype(F32)
    mid = r1.astype(BF16)
    lo = (r1 - mid.astype(F32)).astype(BF16)
    return hi, mid, lo


def _gla_body(reverse, q_ref, k_ref, v_ref, gd_ref, qc_ref, kc_ref, vc_ref, gdc_ref, w2_ref, b2_ref,
              o_ref, st_ref):
    n = pl.program_id(0)
    rows = GLA_ROWS
    ch = GLA_CHUNK
    n_chunks = rows // ch
    dk, dv = GLA_K_DIM, GLA_V_DIM

    @pl.when(n == 0)
    def _():
        st_ref[...] = jnp.zeros_like(st_ref)

    is_ctx = n == 0
    q = jnp.where(is_ctx, qc_ref[...], q_ref[...])
    k = jnp.where(is_ctx, kc_ref[...], k_ref[...])
    v = jnp.where(is_ctx, vc_ref[...], v_ref[...])
    gd = jnp.where(is_ctx, gdc_ref[...], gd_ref[...])

    logits = jnp.dot(gd, w2_ref[...], preferred_element_type=F32) + b2_ref[...]
    log_a = (jnp.minimum(logits, 0.0) - jnp.log(1.0 + jnp.exp(-jnp.abs(logits)))) / GLA_TAU

    ri = lax.broadcasted_iota(jnp.int32, (rows, rows), 0)
    ci = lax.broadcasted_iota(jnp.int32, (rows, rows), 1)
    same = (ri // ch) == (ci // ch)
    tri = jnp.logical_and(same, (ci >= ri) if reverse else (ci <= ri))
    tri_b = jnp.where(tri, 1.0, 0.0).astype(BF16)
    hi, mid, lo = _split3(log_a)
    cum = (jnp.dot(tri_b, hi, preferred_element_type=F32) + jnp.dot(tri_b, mid, preferred_element_type=F32)
           + jnp.dot(tri_b, lo, preferred_element_type=F32))
    e_pos = jnp.exp(cum)
    e_neg = jnp.exp(-cum)
    order = list(range(n_chunks))[::-1] if reverse else list(range(n_chunks))
    last_row = [(c * ch) if reverse else (c * ch + ch - 1) for c in range(n_chunks)]
    scale = dk ** -0.5

    for h in range(GLA_HEADS):
        ks = slice(h * dk, (h + 1) * dk)
        vs = slice(h * dv, (h + 1) * dv)
        qf = q[:, ks].astype(F32) * scale
        kf = k[:, ks].astype(F32)
        vh = v[:, vs]
        q_in = (qf * e_pos[:, ks]).astype(BF16)
        k_in = (kf * e_neg[:, ks]).astype(BF16)
        a = lax.dot_general(q_in, k_in, (((1,), (1,)), ((), ())), preferred_element_type=F32)
        a = jnp.where(tri, a, 0.0).astype(BF16)
        o_intra = jnp.dot(a, vh, preferred_element_type=F32)
        st = st_ref[h]
        for c in order:
            rs = slice(c * ch, (c + 1) * ch)
            cum_last = cum[last_row[c]:last_row[c] + 1, ks]
            o_inter = lax.dot_general(q_in[rs], st.astype(BF16), (((1,), (1,)), ((), ())),
                                      preferred_element_type=F32)
            o_ref[rs, vs] = o_intra[rs] + o_inter
            k_dec = (kf[rs] * jnp.exp(cum_last - cum[rs, ks])).astype(BF16)
            kv_t = lax.dot_general(vh[rs], k_dec, (((0,), (0,)), ((), ())), preferred_element_type=F32)
            st = jnp.exp(cum_last) * st + kv_t
        st_ref[h] = st


def gla_direction(p, gd, pc, gdc, w2pad, b2, reverse):
    t_len = p.shape[0]
    rows = GLA_ROWS
    nb = t_len // rows
    hk = GLA_HEADS * GLA_K_DIM

    if reverse:
        lat = lambda n: jnp.where(n == 0, nb - 1, nb - n)
    else:
        lat = lambda n: jnp.where(n == 0, 0, n - 1)
    out_blk = lambda n: jnp.where(n == 0, nb, lat(n))
    in_specs = [
        pl.BlockSpec((rows, hk), lambda n: (lat(n), P_GQ // hk)),
        pl.BlockSpec((rows, hk), lambda n: (lat(n), P_GK // hk)),
        pl.BlockSpec((rows, GLA_WIDTH), lambda n: (lat(n), P_GV // GLA_WIDTH)),
        pl.BlockSpec((rows, LANES), lambda n: (lat(n), GLA_WIDTH // LANES)),
        pl.BlockSpec((rows, hk), lambda n: (0, P_GQ // hk)),
        pl.BlockSpec((rows, hk), lambda n: (0, P_GK // hk)),
        pl.BlockSpec((rows, GLA_WIDTH), lambda n: (0, P_GV // GLA_WIDTH)),
        pl.BlockSpec((rows, LANES), lambda n: (0, GLA_WIDTH // LANES)),
        pl.BlockSpec((LANES, hk), lambda n: (0, 0)),
        pl.BlockSpec((1, hk), lambda n: (0, 0)),
    ]
    return pl.pallas_call(
        functools.partial(_gla_body, reverse),
        grid=(nb + 1,),
        in_specs=in_specs,
        out_specs=pl.BlockSpec((rows, GLA_WIDTH), lambda n: (out_blk(n), 0)),
        out_shape=jax.ShapeDtypeStruct((t_len + rows, GLA_WIDTH), F32),
        scratch_shapes=[pltpu.VMEM((GLA_HEADS, GLA_V_DIM, GLA_K_DIM), F32)],
        compiler_params=_cparams(("arbitrary",)),
        name="gla_bwd" if reverse else "gla_fwd",
    )(p, p, p, gd, pc, pc, pc, gdc, w2pad, b2.reshape(1, hk))


def _gla_out_body(of_ref, ob_ref, r_ref, g_ref, y_ref):
    g = g_ref[...]
    for h in range(GLA_HEADS):
        vs = slice(h * GLA_V_DIM, (h + 1) * GLA_V_DIM)
        o = of_ref[:, vs] + ob_ref[:, vs]
        ms = jnp.mean(o * o, axis=-1, keepdims=True)
        r = r_ref[:, vs].astype(F32)
        y_ref[:, vs] = (o * lax.rsqrt(ms + EPS) * g * (r * _sigmoid(r))).astype(y_ref.dtype)


def gla_output(o_f, o_b, r, g, row_block0, tm=256):
    n_rows = r.shape[0]
    o_spec = pl.BlockSpec((tm, GLA_WIDTH), lambda i: (row_block0 + i, 0))
    return pl.pallas_call(
        _gla_out_body,
        grid=(n_rows // tm,),
        in_specs=[o_spec, o_spec, pl.BlockSpec((tm, GLA_WIDTH), lambda i: (i, 0)),
                  pl.BlockSpec((1, GLA_V_DIM), lambda i: (0, 0))],
        out_specs=pl.BlockSpec((tm, GLA_WIDTH), lambda i: (i, 0)),
        out_shape=jax.ShapeDtypeStruct((n_rows, GLA_WIDTH), BF16),
        compiler_params=_cparams(("arbitrary",)),
        name="gla_output",
    )(o_f, o_b, r, g.reshape(1, GLA_V_DIM))


FFN_HALO = 16


def _ffn_up_body(n_row_tiles, a_ref, ap_ref, an_ref, wg_ref, wv_ref, cwg_ref, cwv_ref, cbg_ref, cbv_ref,
                 o_ref, ext_ref):
    i = pl.program_id(0)
    tm = a_ref.shape[0]
    hl = FFN_HALO

    @pl.when(pl.program_id(1) == 0)
    def _():
        ext_ref[hl:hl + tm, :] = a_ref[...]
        zero = jnp.zeros_like(ap_ref)
        ext_ref[0:hl, :] = jnp.where(i > 0, ap_ref[...], zero)
        ext_ref[hl + tm:, :] = jnp.where(i < n_row_tiles - 1, an_ref[...], zero)

    ext = ext_ref[...]

    def conv(w_ref, cw_ref, cb_ref):
        u = jnp.dot(ext, w_ref[...].astype(BF16), preferred_element_type=F32)
        c = (pltpu.roll(u, 1, 0) * cw_ref[0:1, :] + u * cw_ref[1:2, :]
             + pltpu.roll(u, tm + 2 * hl - 1, 0) * cw_ref[2:3, :])
        return c[hl:hl + tm] + cb_ref[...]

    gate = conv(wg_ref, cwg_ref, cbg_ref)
    val = conv(wv_ref, cwv_ref, cbv_ref)
    o_ref[...] = (gate * _sigmoid(gate) * val).astype(o_ref.dtype)


def ffn_up_gated(h, w_up, conv_w, conv_b, layer, *, tm, tn):
    m, k = h.shape
    f = w_up.shape[2] // 2
    tm = min(tm, m)
    nt = m // tm
    nc = f // tn
    hb = tm // FFN_HALO
    last = m // FFN_HALO - 1
    gate_half = lambda rows: pl.BlockSpec((None, rows, tn), lambda i, j: (layer, 0, j))
    val_half = lambda rows: pl.BlockSpec((None, rows, tn), lambda i, j: (layer, 0, j + nc))
    cb = conv_b.reshape(conv_b.shape[0], 1, 2 * f)
    return pl.pallas_call(
        functools.partial(_ffn_up_body, nt),
        grid=(nt, nc),
        in_specs=[
            pl.BlockSpec((tm, k), lambda i, j: (i, 0), pipeline_mode=pl.Buffered(1)),
            pl.BlockSpec((FFN_HALO, k), lambda i, j: (jnp.maximum(i * hb - 1, 0), 0)),
            pl.BlockSpec((FFN_HALO, k), lambda i, j: (jnp.minimum((i + 1) * hb, last), 0)),
            gate_half(k), val_half(k), gate_half(3), val_half(3), gate_half(1), val_half(1),
        ],
        out_specs=pl.BlockSpec((tm, tn), lambda i, j: (i, j)),
        out_shape=jax.ShapeDtypeStruct((m, f), BF16),
        scratch_shapes=[pltpu.VMEM((tm + 2 * FFN_HALO, k), BF16)],
        compiler_params=_cparams(("arbitrary", "arbitrary")),
        name="ffn_up_gated",
    )(h, h, h, w_up, w_up, conv_w, conv_w, cb, cb)


def _rope_tables(t_len):
    rows = t_len // GRID_W
    row = jnp.repeat(jnp.arange(rows), GRID_W).astype(F32)
    col = jnp.tile(jnp.arange(GRID_W), rows).astype(F32)
    half = DA_HEAD_DIM // 2
    freqs = ROPE_BASE ** (-jnp.arange(0, half, 2, dtype=F32) / half)
    ar = row[:, None] * freqs
    ac = col[:, None] * freqs
    ang = jnp.concatenate([ar, ar, ac, ac], axis=-1)
    cos, sin = jnp.cos(ang), jnp.sin(ang)
    hi = (jnp.arange(DA_HEAD_DIM) % 64) >= 32
    return cos, jnp.where(hi, sin, 0.0), jnp.where(hi, 0.0, -sin)


def _ffn(x, h, w_up, conv_w, conv_b, w_down_bf16, l, gate):
    act = ffn_up_gated(h, w_up, conv_w, conv_b, l, tm=1024, tn=256)
    return matmul_ksplit_res(act, w_down_bf16, l, x, gate, tm=512, tn=1024, tk=D_FF // 2, name="ffn_down")


def _in_proj(h, w_in_t, l, w_rg, suffix):
    p = matmul_nt(h, w_in_t, layer=l, tm=2048, tn=512, n_out=P_WIDTH, name="in_proj" + suffix)
    rg = matmul_nt(h, w_rg, tm=2048, tn=RG_TN, name="in_proj_rg" + suffix)
    return p, rg


def kernel(x, c, ctx, c_ctx, w_ada, b_ada, norm1_g, norm2_g, w_in, q_norm_g, k_norm_g, lambda_q1, lambda_k1,
           lambda_q2, lambda_k2, da_subln_g, gla_gate_w2, gla_gate_b, gla_norm_g, w_out, w_up, conv_w, conv_b,
           w_down):
    depth = w_ada.shape[0]
    t_len = x.shape[1]
    d = D_MODEL
    xl = x[0]
    xc = ctx[0]
    rope = _rope_tables(t_len)
    w_in_t = jnp.swapaxes(w_in, 1, 2)
    w_down_bf16 = w_down.astype(BF16)

    cvec = jnp.zeros((8, d), F32).at[0].set(c[0]).at[1].set(c_ctx)
    mod = adaln(cvec, w_ada, b_ada)

    for l in range(depth):
        need_ctx = l < depth - 1
        lam_init = 0.8 - 0.6 * math.exp(-0.3 * l)
        mods_l = [mod[l, 0, i * d:(i + 1) * d] for i in range(N_MOD)]
        mods_c = [mod[l, 1, i * d:(i + 1) * d] for i in range(N_MOD)]

        w_rg = jnp.concatenate([w_in_t[l, W_GR:W_GR + GLA_WIDTH, :], w_in_t[l, W_GD:W_GR, :],
                                jnp.zeros((RG_WIDTH - GLA_WIDTH - 2 * GLA_GATE_RANK, d), F32)], axis=0)

        h_l = modulate(xl, norm1_g[l], mods_l[0], mods_l[1])
        h_c = modulate(xc, norm1_g[l], mods_c[0], mods_c[1])
        p_l, rg_l = _in_proj(h_l, w_in_t, l, w_rg, "")
        p_c, rg_c = _in_proj(h_c, w_in_t, l, w_rg, "_ctx")

        qt_l, k_l, vt_l = qk_prep(p_l, q_norm_g[l], k_norm_g[l], rope, ATTN_TK)
        qt_c, k_c, vt_c = qk_prep(p_c, q_norm_g[l], k_norm_g[l], None, xc.shape[0])
        lams = (lambda_q1[l], lambda_k1[l], lambda_q2[l], lambda_k2[l])
        da_l = diff_attention(qt_l, k_l, vt_l, k_c, vt_c, lams, da_subln_g[l], lam_init, tq=512)

        ft_l = fourier_mix_latent(p_l[:, P_FT:P_FT + FT_WIDTH])

        w2 = gla_gate_w2[l]
        zpad = jnp.zeros((LANES - 2 * GLA_GATE_RANK, w2.shape[-1]), F32)
        zr = jnp.zeros((GLA_GATE_RANK, w2.shape[-1]), F32)
        w2_f = jnp.concatenate([w2[0], zr, zpad], axis=0).astype(BF16)
        w2_b = jnp.concatenate([zr, w2[1], zpad], axis=0).astype(BF16)
        o_f = gla_direction(p_l, rg_l, p_c, rg_c, w2_f, gla_gate_b[l, 0], False)
        o_b = gla_direction(p_l, rg_l, p_c, rg_c, w2_b, gla_gate_b[l, 1], True)
        gla_l = gla_output(o_f, o_b, rg_l, gla_norm_g[l], 0)

        x_new = out_proj(da_l, ft_l, gla_l, w_out, l, xl, mods_l[2], tm=1024, tn=512)
        h2_l = modulate(x_new, norm2_g[l], mods_l[3], mods_l[4])
        x_new = _ffn(x_new, h2_l, w_up, conv_w, conv_b, w_down_bf16, l, mods_l[5])

        if need_ctx:
            da_c = diff_attention(qt_c, k_c, vt_c, None, None, lams, da_subln_g[l], lam_init, tq=256)
            ft_c = fourier_mix_short(p_c[:, P_FT:P_FT + FT_WIDTH])
            gla_c = gla_output(o_f, o_b, rg_c, gla_norm_g[l], t_len // 256)
            xc_new = out_proj(da_c, ft_c, gla_c, w_out, l, xc, mods_c[2], tm=1024, tn=512, name="out_proj_ctx")
            h2_c = modulate(xc_new, norm2_g[l], mods_c[3], mods_c[4])
            xc = _ffn(xc_new, h2_c, w_up, conv_w, conv_b, w_down_bf16, l, mods_c[5])
        xl = x_new

    return xl[None]
```

```python
import functools
import math

import numpy as np
import jax
import jax.numpy as jnp
from jax import lax
from jax.experimental import pallas as pl
from jax.experimental.pallas import tpu as pltpu

F32 = jnp.float32
BF16 = jnp.bfloat16

D_MODEL = 4096
CTX_LEN = 256
GRID_W = 64
EPS = 1e-6
ROPE_BASE = 10000.0
DA_HEAD_DIM = 128
DA_HEADS = 8
DA_V_DIM = 256
DA_WIDTH = DA_HEADS * DA_V_DIM
FT_GROUPS = 4
FT_GROUP_DIM = 256
FT_WIDTH = FT_GROUPS * FT_GROUP_DIM
GLA_HEADS = 4
GLA_V_DIM = 256
GLA_K_DIM = 128
GLA_WIDTH = GLA_HEADS * GLA_V_DIM
GLA_GATE_RANK = 16
GLA_TAU = 16.0
GLA_CHUNK = 64
D_FF = 11008
N_MOD = 6

P_Q, P_K, P_V, P_FT, P_GQ, P_GK, P_GV = 0, 2048, 4096, 6144, 7168, 7680, 8192
P_WIDTH = 9216
W_GD, W_GR = 9216, 9248
RG_WIDTH = 1152
RG_TN = 384

V7X_VMEM_LIMIT_BYTES = 56 * 1024 * 1024
LANES = 128
FFT_N2 = 128
GLA_ROWS = 256
ATTN_TK = 512
ATTN_TQ_SUB = 256
Q_SCALE = DA_HEAD_DIM ** -0.5 * math.log2(math.e)


def _cparams(sem):
    return pltpu.CompilerParams(dimension_semantics=sem, vmem_limit_bytes=V7X_VMEM_LIMIT_BYTES)


def _sigmoid(x):
    return 1.0 / (1.0 + jnp.exp(-x))


def _adaln_body(c_ref, w_ref, b_ref, o_ref):
    c = c_ref[...]
    s = (c * _sigmoid(c)).astype(BF16)
    w = w_ref[...].astype(BF16)
    o_ref[...] = jnp.dot(s, w, preferred_element_type=F32) + b_ref[...]


def adaln(cvec, w_ada, b_ada, tn=512):
    depth, d, n = w_ada.shape
    return pl.pallas_call(
        _adaln_body,
        grid=(depth, n // tn),
        in_specs=[
            pl.BlockSpec((8, d), lambda l, j: (0, 0)),
            pl.BlockSpec((None, d, tn), lambda l, j: (l, 0, j)),
            pl.BlockSpec((None, 1, tn), lambda l, j: (l, 0, j)),
        ],
        out_specs=pl.BlockSpec((None, 8, tn), lambda l, j: (l, 0, j)),
        out_shape=jax.ShapeDtypeStruct((depth, 8, n), F32),
        compiler_params=_cparams(("arbitrary", "arbitrary")),
        name="adaln",
    )(cvec, w_ada, b_ada.reshape(depth, 1, n))


def _modulate_body(x_ref, g_ref, sh_ref, sc_ref, o_ref):
    x = x_ref[...]
    ms = jnp.mean(x * x, axis=-1, keepdims=True)
    y = x * lax.rsqrt(ms + EPS) * g_ref[...]
    o_ref[...] = (y * (1.0 + sc_ref[...]) + sh_ref[...]).astype(o_ref.dtype)


def modulate(x, g, shift, scale, tm=512):
    m, d = x.shape
    tm = min(tm, m)
    vec = pl.BlockSpec((1, d), lambda i: (0, 0))
    return pl.pallas_call(
        _modulate_body,
        grid=(m // tm,),
        in_specs=[pl.BlockSpec((tm, d), lambda i: (i, 0)), vec, vec, vec],
        out_specs=pl.BlockSpec((tm, d), lambda i: (i, 0)),
        out_shape=jax.ShapeDtypeStruct((m, d), BF16),
        compiler_params=_cparams(("arbitrary",)),
        name="modulate",
    )(x, g.reshape(1, d), shift.reshape(1, d), scale.reshape(1, d))


def _mm_nt_body(a_ref, w_ref, o_ref):
    w = w_ref[...].astype(BF16)
    acc = lax.dot_general(a_ref[...], w, (((1,), (1,)), ((), ())), preferred_element_type=F32)
    o_ref[...] = acc.astype(o_ref.dtype)


def matmul_nt(a, wt, *, tm, tn, out_dtype=BF16, layer=None, n_out=None, row_map=None, name="matmul_nt"):
    m, k = a.shape
    n = n_out if n_out is not None else wt.shape[-2]
    tm = min(tm, m)
    rm = row_map if row_map is not None else (lambda j: j)
    if wt.ndim == 3:
        w_spec = pl.BlockSpec((None, tn, k), lambda i, j: (layer, rm(j), 0))
    else:
        w_spec = pl.BlockSpec((tn, k), lambda i, j: (rm(j), 0))
    return pl.pallas_call(
        _mm_nt_body,
        grid=(m // tm, n // tn),
        in_specs=[pl.BlockSpec((tm, k), lambda i, j: (i, 0), pipeline_mode=pl.Buffered(1)), w_spec],
        out_specs=pl.BlockSpec((tm, tn), lambda i, j: (i, j)),
        out_shape=jax.ShapeDtypeStruct((m, n), out_dtype),
        compiler_params=_cparams(("arbitrary", "arbitrary")),
        name=name,
    )(a, wt)


def _out_proj_body(da_ref, ft_ref, gla_ref, w_ref, r_ref, g_ref, o_ref):
    k1, k2 = DA_WIDTH, DA_WIDTH + FT_WIDTH
    acc = (jnp.dot(da_ref[...], w_ref[:k1, :].astype(BF16), preferred_element_type=F32)
           + jnp.dot(ft_ref[...], w_ref[k1:k2, :].astype(BF16), preferred_element_type=F32)
           + jnp.dot(gla_ref[...], w_ref[k2:, :].astype(BF16), preferred_element_type=F32))
    o_ref[...] = r_ref[...] + g_ref[...] * acc


def out_proj(da, ft, gla, w_out, layer, res, gate, *, tm, tn, name="out_proj"):
    m = da.shape[0]
    k, n = w_out.shape[1:]
    tm = min(tm, m)
    once = pl.Buffered(1)
    return pl.pallas_call(
        _out_proj_body,
        grid=(m // tm, n // tn),
        in_specs=[
            pl.BlockSpec((tm, DA_WIDTH), lambda i, j: (i, 0), pipeline_mode=once),
            pl.BlockSpec((tm, FT_WIDTH), lambda i, j: (i, 0), pipeline_mode=once),
            pl.BlockSpec((tm, GLA_WIDTH), lambda i, j: (i, 0), pipeline_mode=once),
            pl.BlockSpec((None, k, tn), lambda i, j: (layer, 0, j)),
            pl.BlockSpec((tm, tn), lambda i, j: (i, j)),
            pl.BlockSpec((1, tn), lambda i, j: (0, j)),
        ],
        out_specs=pl.BlockSpec((tm, tn), lambda i, j: (i, j)),
        out_shape=jax.ShapeDtypeStruct((m, n), F32),
        compiler_params=_cparams(("arbitrary", "arbitrary")),
        name=name,
    )(da, ft, gla, w_out, res, gate.reshape(1, n))


def _mm_ksplit_res_body(a_ref, b_ref, r_ref, g_ref, o_ref, acc_ref):
    kk = pl.program_id(2)

    @pl.when(kk == 0)
    def _():
        acc_ref[...] = jnp.zeros_like(acc_ref)

    acc_ref[...] += jnp.dot(a_ref[...], b_ref[...], preferred_element_type=F32)

    @pl.when(kk == pl.num_programs(2) - 1)
    def _():
        o_ref[...] = r_ref[...] + g_ref[...] * acc_ref[...]


def matmul_ksplit_res(a, b, layer, res, gate, *, tm, tn, tk, name="matmul_ksplit"):
    m, k = a.shape
    n = b.shape[2]
    tm = min(tm, m)
    return pl.pallas_call(
        _mm_ksplit_res_body,
        grid=(m // tm, n // tn, k // tk),
        in_specs=[
            pl.BlockSpec((tm, tk), lambda i, j, kk: (i, kk)),
            pl.BlockSpec((None, tk, tn), lambda i, j, kk: (layer, kk, j)),
            pl.BlockSpec((tm, tn), lambda i, j, kk: (i, j)),
            pl.BlockSpec((1, tn), lambda i, j, kk: (0, j)),
        ],
        out_specs=pl.BlockSpec((tm, tn), lambda i, j, kk: (i, j)),
        out_shape=jax.ShapeDtypeStruct((m, n), F32),
        scratch_shapes=[pltpu.VMEM((tm, tn), F32)],
        compiler_params=_cparams(("parallel", "arbitrary", "arbitrary")),
        name=name,
    )(a, b, res, gate.reshape(1, n))


def _qk_prep_body(use_rope, p_ref, gq_ref, gk_ref, cos_ref, sa_ref, sb_ref, qt_ref, k_ref, vt_ref):
    n_groups = (2 * DA_HEADS * DA_HEAD_DIM) // LANES
    if use_rope:
        cos, sa, sb = cos_ref[...], sa_ref[...], sb_ref[...]

    def prep(col, gain, scale):
        x = p_ref[:, col:col + LANES].astype(F32)
        ms = jnp.mean(x * x, axis=-1, keepdims=True)
        y = x * lax.rsqrt(ms + EPS) * gain
        if use_rope:
            y = y * cos + pltpu.roll(y, 32, 1) * sa + pltpu.roll(y, 96, 1) * sb
        return y * scale if scale != 1.0 else y

    gq, gk = gq_ref[...], gk_ref[...]
    for g in range(n_groups):
        qt_ref[g * LANES:(g + 1) * LANES, :] = prep(P_Q + g * LANES, gq, Q_SCALE).T.astype(qt_ref.dtype)
        k_ref[:, g * LANES:(g + 1) * LANES] = prep(P_K + g * LANES, gk, 1.0).astype(k_ref.dtype)
    for h in range(DA_HEADS):
        v = p_ref[:, P_V + h * DA_V_DIM:P_V + (h + 1) * DA_V_DIM].astype(F32)
        vt_ref[h] = v.T.astype(vt_ref.dtype)


def qk_prep(p, gq, gk, rope, tm):
    rows = p.shape[0]
    use_rope = rope is not None
    width = 2 * DA_HEADS * DA_HEAD_DIM
    vec = pl.BlockSpec((1, LANES), lambda i: (0, 0))
    tab = pl.BlockSpec((tm, LANES), lambda i: (i, 0))
    if use_rope:
        tabs = list(rope)
    else:
        tabs = [jnp.zeros((rows, LANES), F32)] * 3
    return pl.pallas_call(
        functools.partial(_qk_prep_body, use_rope),
        grid=(rows // tm,),
        in_specs=[pl.BlockSpec((tm, P_V + DA_WIDTH), lambda i: (i, 0)), vec, vec, tab, tab, tab],
        out_specs=[pl.BlockSpec((width, tm), lambda i: (0, i)),
                   pl.BlockSpec((tm, width), lambda i: (i, 0)),
                   pl.BlockSpec((DA_HEADS, None, DA_V_DIM, tm), lambda i: (0, i, 0, 0))],
        out_shape=[jax.ShapeDtypeStruct((width, rows), BF16),
                   jax.ShapeDtypeStruct((rows, width), BF16),
                   jax.ShapeDtypeStruct((DA_HEADS, rows // tm, DA_V_DIM, tm), BF16)],
        compiler_params=_cparams(("arbitrary",)),
        name="qk_prep",
    )(p, gq.reshape(1, LANES), gk.reshape(1, LANES), *tabs)


def _attn_body(lam_init, n_chunks, tk, has_ctx, *refs):
    if has_ctx:
        (qt_ref, k_ref, vt_ref, kc_ref, vtc_ref, lq1_ref, lk1_ref, lq2_ref, lk2_ref, g_ref,
         o_ref, m_ref, l_ref, acc_ref) = refs
    else:
        (qt_ref, k_ref, vt_ref, lq1_ref, lk1_ref, lq2_ref, lk2_ref, g_ref,
         o_ref, m_ref, l_ref, acc_ref) = refs
    d = DA_HEAD_DIM
    qt = (qt_ref[:d, :], qt_ref[d:, :])
    m_ref[...] = jnp.full_like(m_ref, -jnp.inf)
    l_ref[...] = jnp.zeros_like(l_ref)
    acc_ref[...] = jnp.zeros_like(acc_ref)

    def update(k_blk, vt_blk):
        for sb in range(qt_ref.shape[1] // ATTN_TQ_SUB):
            ls = slice(sb * ATTN_TQ_SUB, (sb + 1) * ATTN_TQ_SUB)
            for mi in range(2):
                s = jnp.dot(k_blk[:, mi * d:(mi + 1) * d], qt[mi][:, ls],
                            preferred_element_type=F32)
                m_old = m_ref[mi, :, ls]
                m_new = jnp.maximum(m_old, jnp.max(s, axis=0, keepdims=True))
                alpha = jnp.exp2(m_old - m_new)
                p = jnp.exp2(s - m_new)
                l_ref[mi, :, ls] = alpha * l_ref[mi, :, ls] + jnp.sum(p, axis=0, keepdims=True)
                acc_ref[mi, :, ls] = (alpha * acc_ref[mi, :, ls]
                                      + jnp.dot(vt_blk, p.astype(BF16), preferred_element_type=F32))
                m_ref[mi, :, ls] = m_new

    if has_ctx:
        update(kc_ref[...], vtc_ref[...])

    def step(c, carry):
        off = pl.multiple_of(c * tk, tk)
        update(k_ref[pl.ds(off, tk), :], vt_ref[c])
        return carry

    lax.fori_loop(0, n_chunks, step, 0, unroll=math.gcd(n_chunks, 16))

    lam =(jnp.exp(jnp.sum(lq1_ref[...] * lk1_ref[...])) - jnp.exp(jnp.sum(lq2_ref[...] * lk2_ref[...]))
           + lam_init)
    ot = acc_ref[0] * (1.0 / l_ref[0]) - lam * (acc_ref[1] * (1.0 / l_ref[1]))
    ms = jnp.mean(ot * ot, axis=0, keepdims=True)
    o = (ot * lax.rsqrt(ms + EPS)).T
    o_ref[...] = (o * g_ref[...] * (1.0 - lam_init)).astype(o_ref.dtype)


def diff_attention(qt, k, vt, kc, vtc, lams, subln_g, lam_init, *, tq):
    t_q = qt.shape[1]
    t_k = k.shape[0]
    n_chunks, tk = vt.shape[1], vt.shape[3]
    has_ctx = kc is not None
    tq = min(tq, t_q)
    hw = 2 * DA_HEAD_DIM
    in_specs = [
        pl.BlockSpec((hw, tq), lambda h, i: (h, i)),
        pl.BlockSpec((t_k, hw), lambda h, i: (0, h)),
        pl.BlockSpec((None, n_chunks, DA_V_DIM, tk), lambda h, i: (h, 0, 0, 0)),
    ]
    args = [qt, k, vt]
    if has_ctx:
        t_c = kc.shape[0]
        in_specs += [pl.BlockSpec((t_c, hw), lambda h, i: (0, h)),
                     pl.BlockSpec((None, None, DA_V_DIM, t_c), lambda h, i: (h, 0, 0, 0))]
        args += [kc, vtc]
    vec = pl.BlockSpec((1, DA_HEAD_DIM), lambda h, i: (0, 0))
    in_specs += [vec] * 4 + [pl.BlockSpec((1, DA_V_DIM), lambda h, i: (0, 0))]
    args += [v.reshape(1, DA_HEAD_DIM) for v in lams] + [subln_g.reshape(1, DA_V_DIM)]
    return pl.pallas_call(
        functools.partial(_attn_body, lam_init, n_chunks, tk, has_ctx),
        grid=(DA_HEADS, t_q // tq),
        in_specs=in_specs,
        out_specs=pl.BlockSpec((tq, DA_V_DIM), lambda h, i: (i, h)),
        out_shape=jax.ShapeDtypeStruct((t_q, DA_WIDTH), BF16),
        scratch_shapes=[pltpu.VMEM((2, 1, tq), F32), pltpu.VMEM((2, 1, tq), F32),
                        pltpu.VMEM((2, DA_V_DIM, tq), F32)],
        compiler_params=_cparams(("parallel", "arbitrary")),
        name="diff_attention",
    )(*args)


def _dft_tables_latent(t_len):
    n2 = FFT_N2
    n1 = t_len // n2
    f1 = np.arange(n1)[:, None]
    t1 = np.arange(n1)[None, :]
    ang = 2.0 * np.pi * ((f1 * t1) % n1) / n1
    stage_a = np.concatenate([np.cos(ang), np.sin(ang)], axis=0)
    f2 = np.arange(n2)[None, :, None]
    t2 = np.arange(n2)[None, None, :]
    f1b = np.arange(n1)[:, None, None]
    idx = (t2 * f2 * n1 + t2 * f1b) % t_len
    th = 2.0 * np.pi * idx / t_len
    mr, mi = np.cos(th), -np.sin(th)
    stage_b = np.concatenate([np.concatenate([mr, mi], axis=2), np.concatenate([mi, -mr], axis=2)], axis=1)
    return stage_a.astype(np.float32), stage_b.astype(np.float32)


def _dft_table_channels(t_len, sign):
    c = np.arange(FT_GROUP_DIM)
    ang = 2.0 * np.pi * ((c[:, None] * c[None, :]) % FT_GROUP_DIM) / FT_GROUP_DIM
    scale = 1.0 / math.sqrt(t_len * FT_GROUP_DIM)
    return (np.concatenate([np.cos(ang), sign * np.sin(ang)], axis=0) * scale).astype(np.float32)


def _fft_a_body(cs_ref, z_ref, p_ref, q_ref):
    n1 = z_ref.shape[0]
    r = jnp.dot(cs_ref[...], z_ref[...], preferred_element_type=F32)
    p_ref[...] = r[:n1].astype(p_ref.dtype)
    q_ref[...] = r[n1:].astype(q_ref.dtype)


def _fft_b_body(m_ref, p_ref, q_ref, ch_ref, o_ref):
    n2 = FFT_N2
    pq = jnp.concatenate([p_ref[...], q_ref[...]], axis=0)
    r = jnp.dot(m_ref[...], pq, preferred_element_type=F32)
    ch = ch_ref[...]
    for g in range(FT_GROUPS):
        c0 = g * FT_GROUP_DIM
        rg = jnp.concatenate([r[:n2, c0:c0 + FT_GROUP_DIM], r[n2:, c0:c0 + FT_GROUP_DIM]], axis=1).astype(BF16)
        o_ref[:, c0:c0 + FT_GROUP_DIM] = jnp.dot(rg, ch, preferred_element_type=F32).astype(o_ref.dtype)


def fourier_mix_latent(z, tn=8192):
    t_len, width = z.shape
    n2 = FFT_N2
    n1 = t_len // n2
    ta, tb = _dft_tables_latent(t_len)
    cs = jnp.asarray(ta).astype(BF16)
    mt = jnp.asarray(tb).astype(BF16)
    ch = jnp.asarray(_dft_table_channels(t_len, 1.0)).astype(BF16)
    cols = n2 * width
    tn = min(tn, cols)
    zv = z.reshape(n1, cols)
    pv, qv = pl.pallas_call(
        _fft_a_body,
        grid=(cols // tn,),
        in_specs=[pl.BlockSpec((2 * n1, n1), lambda j: (0, 0)), pl.BlockSpec((n1, tn), lambda j: (0, j))],
        out_specs=[pl.BlockSpec((n1, tn), lambda j: (0, j))] * 2,
        out_shape=[jax.ShapeDtypeStruct((n1, cols), BF16)] * 2,
        compiler_params=_cparams(("arbitrary",)),
        name="fft_stage_a",
    )(cs, zv)
    pm = pv.reshape(t_len, width)
    qm = qv.reshape(t_len, width)
    out = pl.pallas_call(
        _fft_b_body,
        grid=(n1,),
        in_specs=[
            pl.BlockSpec((None, 2 * n2, 2 * n2), lambda f: (f, 0, 0)),
            pl.BlockSpec((n2, width), lambda f: (f, 0)),
            pl.BlockSpec((n2, width), lambda f: (f, 0)),
            pl.BlockSpec((2 * FT_GROUP_DIM, FT_GROUP_DIM), lambda f: (0, 0)),
        ],
        out_specs=pl.BlockSpec((n2, width), lambda f: (0, f)),
        out_shape=jax.ShapeDtypeStruct((n2, n1 * width), BF16),
        compiler_params=_cparams(("arbitrary",)),
        name="fft_stage_b",
    )(mt, pm, qm, ch)
    return out.reshape(t_len, width)


def _fft_ctx_body(cs_ref, z_ref, ch_ref, o_ref):
    t_len = z_ref.shape[0]
    r = jnp.dot(cs_ref[...], z_ref[...], preferred_element_type=F32)
    ch = ch_ref[...]
    for g in range(FT_GROUPS):
        c0 = g * FT_GROUP_DIM
        rg = jnp.concatenate([r[:t_len, c0:c0 + FT_GROUP_DIM], r[t_len:, c0:c0 + FT_GROUP_DIM]], axis=1).astype(BF16)
        o_ref[:, c0:c0 + FT_GROUP_DIM] = jnp.dot(rg, ch, preferred_element_type=F32).astype(o_ref.dtype)


def fourier_mix_short(z):
    t_len, width = z.shape
    f = np.arange(t_len)
    ang = 2.0 * np.pi * ((f[:, None] * f[None, :]) % t_len) / t_len
    cs = jnp.asarray(np.concatenate([np.cos(ang), np.sin(ang)], axis=0).astype(np.float32)).astype(BF16)
    ch = jnp.asarray(_dft_table_channels(t_len, -1.0)).astype(BF16)
    return pl.pallas_call(
        _fft_ctx_body,
        out_shape=jax.ShapeDtypeStruct((t_len, width), BF16),
        compiler_params=pltpu.CompilerParams(vmem_limit_bytes=V7X_VMEM_LIMIT_BYTES),
        name="fft_short",
    )(cs, z, ch)


def _split3(x):
    hi = x.astype(BF16)
    r1 = x - hi.astype(F32)
    mid = r1.astype(BF16)
    lo = (r1 - mid.astype(F32)).astype(BF16)
    return hi, mid, lo


def _gla_body(reverse, q_ref, k_ref, v_ref, gd_ref, qc_ref, kc_ref, vc_ref, gdc_ref, w2_ref, b2_ref,
              o_ref, st_ref):
    n = pl.program_id(0)
    rows = GLA_ROWS
    ch = GLA_CHUNK
    n_chunks = rows // ch
    dk, dv = GLA_K_DIM, GLA_V_DIM

    @pl.when(n == 0)
    def _():
        st_ref[...] = jnp.zeros_like(st_ref)

    is_ctx = n == 0
    q = jnp.where(is_ctx, qc_ref[...], q_ref[...])
    k = jnp.where(is_ctx, kc_ref[...], k_ref[...])
    v = jnp.where(is_ctx, vc_ref[...], v_ref[...])
    gd = jnp.where(is_ctx, gdc_ref[...], gd_ref[...])

    logits = jnp.dot(gd, w2_ref[...], preferred_element_type=F32) + b2_ref[...]
    log_a = (jnp.minimum(logits, 0.0) - jnp.log(1.0 + jnp.exp(-jnp.abs(logits)))) / GLA_TAU

    ri = lax.broadcasted_iota(jnp.int32, (rows, rows), 0)
    ci = lax.broadcasted_iota(jnp.int32, (rows, rows), 1)
    same = (ri // ch) == (ci // ch)
    tri = jnp.logical_and(same, (ci >= ri) if reverse else (ci <= ri))
    tri_b = jnp.where(tri, 1.0, 0.0).astype(BF16)
    hi, mid, lo = _split3(log_a)
    cum = (jnp.dot(tri_b, hi, preferred_element_type=F32) + jnp.dot(tri_b, mid, preferred_element_type=F32)
           + jnp.dot(tri_b, lo, preferred_element_type=F32))
    e_pos = jnp.exp(cum)
    e_neg = jnp.exp(-cum)
    order = list(range(n_chunks))[::-1] if reverse else list(range(n_chunks))
    last_row = [(c * ch) if reverse else (c * ch + ch - 1) for c in range(n_chunks)]
    scale = dk ** -0.5

    for h in range(GLA_HEADS):
        ks = slice(h * dk, (h + 1) * dk)
        vs = slice(h * dv, (h + 1) * dv)
        qf = q[:, ks].astype(F32) * scale
        kf = k[:, ks].astype(F32)
        vh = v[:, vs]
        q_in = (qf * e_pos[:, ks]).astype(BF16)
        k_in = (kf * e_neg[:, ks]).astype(BF16)
        a = lax.dot_general(q_in, k_in, (((1,), (1,)), ((), ())), preferred_element_type=F32)
        a = jnp.where(tri, a, 0.0).astype(BF16)
        o_intra = jnp.dot(a, vh, preferred_element_type=F32)
        st = st_ref[h]
        for c in order:
            rs = slice(c * ch, (c + 1) * ch)
            cum_last = cum[last_row[c]:last_row[c] + 1, ks]
            o_inter = lax.dot_general(q_in[rs], st.astype(BF16), (((1,), (1,)), ((), ())),
                                      preferred_element_type=F32)
            o_ref[rs, vs] = o_intra[rs] + o_inter
            k_dec = (kf[rs] * jnp.exp(cum_last - cum[rs, ks])).astype(BF16)
            kv_t = lax.dot_general(vh[rs], k_dec, (((0,), (0,)), ((), ())), preferred_element_type=F32)
            st = jnp.exp(cum_last) * st + kv_t
        st_ref[h] = st


def gla_direction(p, gd, pc, gdc, w2pad, b2, reverse):
    t_len = p.shape[0]
    rows = GLA_ROWS
    nb = t_len // rows
    hk = GLA_HEADS * GLA_K_DIM

    if reverse:
        lat = lambda n: jnp.where(n == 0, nb - 1, nb - n)
    else:
        lat = lambda n: jnp.where(n == 0, 0, n - 1)
    out_blk = lambda n: jnp.where(n == 0, nb, lat(n))
    in_specs = [
        pl.BlockSpec((rows, hk), lambda n: (lat(n), P_GQ // hk)),
        pl.BlockSpec((rows, hk), lambda n: (lat(n), P_GK // hk)),
        pl.BlockSpec((rows, GLA_WIDTH), lambda n: (lat(n), P_GV // GLA_WIDTH)),
        pl.BlockSpec((rows, LANES), lambda n: (lat(n), GLA_WIDTH // LANES)),
        pl.BlockSpec((rows, hk), lambda n: (0, P_GQ // hk)),
        pl.BlockSpec((rows, hk), lambda n: (0, P_GK // hk)),
        pl.BlockSpec((rows, GLA_WIDTH), lambda n: (0, P_GV // GLA_WIDTH)),
        pl.BlockSpec((rows, LANES), lambda n: (0, GLA_WIDTH // LANES)),
        pl.BlockSpec((LANES, hk), lambda n: (0, 0)),
        pl.BlockSpec((1, hk), lambda n: (0, 0)),
    ]
    return pl.pallas_call(
        functools.partial(_gla_body, reverse),
        grid=(nb + 1,),
        in_specs=in_specs,
        out_specs=pl.BlockSpec((rows, GLA_WIDTH), lambda n: (out_blk(n), 0)),
        out_shape=jax.ShapeDtypeStruct((t_len + rows, GLA_WIDTH), F32),
        scratch_shapes=[pltpu.VMEM((GLA_HEADS, GLA_V_DIM, GLA_K_DIM), F32)],
        compiler_params=_cparams(("arbitrary",)),
        name="gla_bwd" if reverse else "gla_fwd",
    )(p, p, p, gd, pc, pc, pc, gdc, w2pad, b2.reshape(1, hk))


def _gla_out_body(of_ref, ob_ref, r_ref, g_ref, y_ref):
    g = g_ref[...]
    for h in range(GLA_HEADS):
        vs = slice(h * GLA_V_DIM, (h + 1) * GLA_V_DIM)
        o = of_ref[:, vs] + ob_ref[:, vs]
        ms = jnp.mean(o * o, axis=-1, keepdims=True)
        r = r_ref[:, vs].astype(F32)
        y_ref[:, vs] = (o * lax.rsqrt(ms + EPS) * g * (r * _sigmoid(r))).astype(y_ref.dtype)


def gla_output(o_f, o_b, r, g, row_block0, tm=256):
    n_rows = r.shape[0]
    o_spec = pl.BlockSpec((tm, GLA_WIDTH), lambda i: (row_block0 + i, 0))
    return pl.pallas_call(
        _gla_out_body,
        grid=(n_rows // tm,),
        in_specs=[o_spec, o_spec, pl.BlockSpec((tm, GLA_WIDTH), lambda i: (i, 0)),
                  pl.BlockSpec((1, GLA_V_DIM), lambda i: (0, 0))],
        out_specs=pl.BlockSpec((tm, GLA_WIDTH), lambda i: (i, 0)),
        out_shape=jax.ShapeDtypeStruct((n_rows, GLA_WIDTH), BF16),
        compiler_params=_cparams(("arbitrary",)),
        name="gla_output",
    )(o_f, o_b, r, g.reshape(1, GLA_V_DIM))


FFN_HALO = 16


def _ffn_up_body(n_row_tiles, a_ref, ap_ref, an_ref, wg_ref, wv_ref, cwg_ref, cwv_ref, cbg_ref, cbv_ref,
                 o_ref, ext_ref):
    i = pl.program_id(0)
    tm = a_ref.shape[0]
    hl = FFN_HALO

    @pl.when(pl.program_id(1) == 0)
    def _():
        ext_ref[hl:hl + tm, :] = a_ref[...]
        zero = jnp.zeros_like(ap_ref)
        ext_ref[0:hl, :] = jnp.where(i > 0, ap_ref[...], zero)
        ext_ref[hl + tm:, :] = jnp.where(i < n_row_tiles - 1, an_ref[...], zero)

    ext = ext_ref[...]

    def conv(w_ref, cw_ref, cb_ref):
        u = jnp.dot(ext, w_ref[...].astype(BF16), preferred_element_type=F32)
        c = (pltpu.roll(u, 1, 0) * cw_ref[0:1, :] + u * cw_ref[1:2, :]
             + pltpu.roll(u, tm + 2 * hl - 1, 0) * cw_ref[2:3, :])
        return c[hl:hl + tm] + cb_ref[...]

    gate = conv(wg_ref, cwg_ref, cbg_ref)
    val = conv(wv_ref, cwv_ref, cbv_ref)
    o_ref[...] = (gate * _sigmoid(gate) * val).astype(o_ref.dtype)


def ffn_up_gated(h, w_up, conv_w, conv_b, layer, *, tm, tn):
    m, k = h.shape
    f = w_up.shape[2] // 2
    tm = min(tm, m)
    nt = m // tm
    nc = f // tn
    hb = tm // FFN_HALO
    last = m // FFN_HALO - 1
    gate_half = lambda rows: pl.BlockSpec((None, rows, tn), lambda i, j: (layer, 0, j))
    val_half = lambda rows: pl.BlockSpec((None, rows, tn), lambda i, j: (layer, 0, j + nc))
    cb = conv_b.reshape(conv_b.shape[0], 1, 2 * f)
    return pl.pallas_call(
        functools.partial(_ffn_up_body, nt),
        grid=(nt, nc),
        in_specs=[
            pl.BlockSpec((tm, k), lambda i, j: (i, 0), pipeline_mode=pl.Buffered(1)),
            pl.BlockSpec((FFN_HALO, k), lambda i, j: (jnp.maximum(i * hb - 1, 0), 0)),
            pl.BlockSpec((FFN_HALO, k), lambda i, j: (jnp.minimum((i + 1) * hb, last), 0)),
            gate_half(k), val_half(k), gate_half(3), val_half(3), gate_half(1), val_half(1),
        ],
        out_specs=pl.BlockSpec((tm, tn), lambda i, j: (i, j)),
        out_shape=jax.ShapeDtypeStruct((m, f), BF16),
        scratch_shapes=[pltpu.VMEM((tm + 2 * FFN_HALO, k), BF16)],
        compiler_params=_cparams(("arbitrary", "arbitrary")),
        name="ffn_up_gated",
    )(h, h, h, w_up, w_up, conv_w, conv_w, cb, cb)


def _rope_tables(t_len):
    rows = t_len // GRID_W
    row = jnp.repeat(jnp.arange(rows), GRID_W).astype(F32)
    col = jnp.tile(jnp.arange(GRID_W), rows).astype(F32)
    half = DA_HEAD_DIM // 2
    freqs = ROPE_BASE ** (-jnp.arange(0, half, 2, dtype=F32) / half)
    ar = row[:, None] * freqs
    ac = col[:, None] * freqs
    ang = jnp.concatenate([ar, ar, ac, ac], axis=-1)
    cos, sin = jnp.cos(ang), jnp.sin(ang)
    hi = (jnp.arange(DA_HEAD_DIM) % 64) >= 32
    return cos, jnp.where(hi, sin, 0.0), jnp.where(hi, 0.0, -sin)


def _ffn(x, h, w_up, conv_w, conv_b, w_down_bf16, l, gate):
    act = ffn_up_gated(h, w_up, conv_w, conv_b, l, tm=1024, tn=256)
    return matmul_ksplit_res(act, w_down_bf16, l, x, gate, tm=1024, tn=512, tk=D_FF // 2, name="ffn_down")


def _in_proj(h, w_in_t, l, w_rg, suffix):
    p = matmul_nt(h, w_in_t, layer=l, tm=2048, tn=512, n_out=P_WIDTH, name="in_proj" + suffix)
    rg = matmul_nt(h, w_rg, tm=2048, tn=RG_TN, name="in_proj_rg" + suffix)
    return p, rg


def kernel(x, c, ctx, c_ctx, w_ada, b_ada, norm1_g, norm2_g, w_in, q_norm_g, k_norm_g, lambda_q1, lambda_k1,
           lambda_q2, lambda_k2, da_subln_g, gla_gate_w2, gla_gate_b, gla_norm_g, w_out, w_up, conv_w, conv_b,
           w_down):
    depth = w_ada.shape[0]
    t_len = x.shape[1]
    d = D_MODEL
    xl = x[0]
    xc = ctx[0]
    rope = _rope_tables(t_len)
    w_in_t = jnp.swapaxes(w_in, 1, 2)
    w_down_bf16 = w_down.astype(BF16)

    cvec = jnp.zeros((8, d), F32).at[0].set(c[0]).at[1].set(c_ctx)
    mod = adaln(cvec, w_ada, b_ada)

    for l in range(depth):
        need_ctx = l < depth - 1
        lam_init = 0.8 - 0.6 * math.exp(-0.3 * l)
        mods_l = [mod[l, 0, i * d:(i + 1) * d] for i in range(N_MOD)]
        mods_c = [mod[l, 1, i * d:(i + 1) * d] for i in range(N_MOD)]

        w_rg = jnp.concatenate([w_in_t[l, W_GR:W_GR + GLA_WIDTH, :], w_in_t[l, W_GD:W_GR, :],
                                jnp.zeros((RG_WIDTH - GLA_WIDTH - 2 * GLA_GATE_RANK, d), F32)], axis=0)

        h_l = modulate(xl, norm1_g[l], mods_l[0], mods_l[1])
        h_c = modulate(xc, norm1_g[l], mods_c[0], mods_c[1])
        p_l, rg_l = _in_proj(h_l, w_in_t, l, w_rg, "")
        p_c, rg_c = _in_proj(h_c, w_in_t, l, w_rg, "_ctx")

        qt_l, k_l, vt_l = qk_prep(p_l, q_norm_g[l], k_norm_g[l], rope, ATTN_TK)
        qt_c, k_c, vt_c = qk_prep(p_c, q_norm_g[l], k_norm_g[l], None, xc.shape[0])
        lams = (lambda_q1[l], lambda_k1[l], lambda_q2[l], lambda_k2[l])
        da_l = diff_attention(qt_l, k_l, vt_l, k_c, vt_c, lams, da_subln_g[l], lam_init, tq=1024)

        ft_l = fourier_mix_latent(p_l[:, P_FT:P_FT + FT_WIDTH])

        w2 = gla_gate_w2[l]
        zpad = jnp.zeros((LANES - 2 * GLA_GATE_RANK, w2.shape[-1]), F32)
        zr = jnp.zeros((GLA_GATE_RANK, w2.shape[-1]), F32)
        w2_f = jnp.concatenate([w2[0], zr, zpad], axis=0).astype(BF16)
        w2_b = jnp.concatenate([zr, w2[1], zpad], axis=0).astype(BF16)
        o_f = gla_direction(p_l, rg_l, p_c, rg_c, w2_f, gla_gate_b[l, 0], False)
        o_b = gla_direction(p_l, rg_l, p_c, rg_c, w2_b, gla_gate_b[l, 1], True)
        gla_l = gla_output(o_f, o_b, rg_l, gla_norm_g[l], 0)

        x_new = out_proj(da_l, ft_l, gla_l, w_out, l, xl, mods_l[2], tm=2048, tn=256)
        h2_l = modulate(x_new, norm2_g[l], mods_l[3], mods_l[4])
        x_new = _ffn(x_new, h2_l, w_up, conv_w, conv_b, w_down_bf16, l, mods_l[5])

        if need_ctx:
            da_c = diff_attention(qt_c, k_c, vt_c, None, None, lams, da_subln_g[l], lam_init, tq=256)
            ft_c = fourier_mix_short(p_c[:, P_FT:P_FT + FT_WIDTH])
            gla_c = gla_output(o_f, o_b, rg_c, gla_norm_g[l], t_len // 256)
            xc_new = out_proj(da_c, ft_c, gla_c, w_out, l, xc, mods_c[2], tm=1024, tn=512, name="out_proj_ctx")
            h2_c = modulate(xc_new, norm2_g[l], mods_c[3], mods_c[4])
            xc = _ffn(xc_new, h2_c, w_up, conv_w, conv_b, w_down_bf16, l, mods_c[5])
        xl = x_new

    return xl[None]
```

```python
import functools
import math

import numpy as np
import jax
import jax.numpy as jnp
from jax import lax
from jax.experimental import pallas as pl
from jax.experimental.pallas import tpu as pltpu

F32 = jnp.float32
BF16 = jnp.bfloat16

D_MODEL = 4096
CTX_LEN = 256
GRID_W = 64
EPS = 1e-6
ROPE_BASE = 10000.0
DA_HEAD_DIM = 128
DA_HEADS = 8
DA_V_DIM = 256
DA_WIDTH = DA_HEADS * DA_V_DIM
FT_GROUPS = 4
FT_GROUP_DIM = 256
FT_WIDTH = FT_GROUPS * FT_GROUP_DIM
GLA_HEADS = 4
GLA_V_DIM = 256
GLA_K_DIM = 128
GLA_WIDTH = GLA_HEADS * GLA_V_DIM
GLA_GATE_RANK = 16
GLA_TAU = 16.0
GLA_CHUNK = 64
D_FF = 11008
N_MOD = 6

P_Q, P_K, P_V, P_FT, P_GQ, P_GK, P_GV = 0, 2048, 4096, 6144, 7168, 7680, 8192
P_WIDTH = 9216
W_GD, W_GR = 9216, 9248
RG_WIDTH = 1152
RG_TN = 384

V7X_VMEM_LIMIT_BYTES = 56 * 1024 * 1024
LANES = 128
FFT_N2 = 128
GLA_ROWS = 256
ATTN_TK = 512
ATTN_TQ_SUB = 256
Q_SCALE = DA_HEAD_DIM ** -0.5 * math.log2(math.e)


def _cparams(sem):
    return pltpu.CompilerParams(dimension_semantics=sem, vmem_limit_bytes=V7X_VMEM_LIMIT_BYTES)


def _sigmoid(x):
    return 1.0 / (1.0 + jnp.exp(-x))


def _adaln_body(c_ref, w_ref, b_ref, o_ref):
    c = c_ref[...]
    s = (c * _sigmoid(c)).astype(BF16)
    w = w_ref[...].astype(BF16)
    o_ref[...] = jnp.dot(s, w, preferred_element_type=F32) + b_ref[...]


def adaln(cvec, w_ada, b_ada, tn=512):
    depth, d, n = w_ada.shape
    return pl.pallas_call(
        _adaln_body,
        grid=(depth, n // tn),
        in_specs=[
            pl.BlockSpec((8, d), lambda l, j: (0, 0)),
            pl.BlockSpec((None, d, tn), lambda l, j: (l, 0, j)),
            pl.BlockSpec((None, 1, tn), lambda l, j: (l, 0, j)),
        ],
        out_specs=pl.BlockSpec((None, 8, tn), lambda l, j: (l, 0, j)),
        out_shape=jax.ShapeDtypeStruct((depth, 8, n), F32),
        compiler_params=_cparams(("arbitrary", "arbitrary")),
        name="adaln",
    )(cvec, w_ada, b_ada.reshape(depth, 1, n))


def _modulate_body(x_ref, g_ref, sh_ref, sc_ref, o_ref):
    x = x_ref[...]
    ms = jnp.mean(x * x, axis=-1, keepdims=True)
    y = x * lax.rsqrt(ms + EPS) * g_ref[...]
    o_ref[...] = (y * (1.0 + sc_ref[...]) + sh_ref[...]).astype(o_ref.dtype)


def modulate(x, g, shift, scale, tm=512):
    m, d = x.shape
    tm = min(tm, m)
    vec = pl.BlockSpec((1, d), lambda i: (0, 0))
    return pl.pallas_call(
        _modulate_body,
        grid=(m // tm,),
        in_specs=[pl.BlockSpec((tm, d), lambda i: (i, 0)), vec, vec, vec],
        out_specs=pl.BlockSpec((tm, d), lambda i: (i, 0)),
        out_shape=jax.ShapeDtypeStruct((m, d), BF16),
        compiler_params=_cparams(("arbitrary",)),
        name="modulate",
    )(x, g.reshape(1, d), shift.reshape(1, d), scale.reshape(1, d))


def _mm_nt_body(a_ref, w_ref, o_ref):
    w = w_ref[...].astype(BF16)
    acc = lax.dot_general(a_ref[...], w, (((1,), (1,)), ((), ())), preferred_element_type=F32)
    o_ref[...] = acc.astype(o_ref.dtype)


def matmul_nt(a, wt, *, tm, tn, out_dtype=BF16, layer=None, n_out=None, row_map=None, name="matmul_nt"):
    m, k = a.shape
    n = n_out if n_out is not None else wt.shape[-2]
    tm = min(tm, m)
    rm = row_map if row_map is not None else (lambda j: j)
    if wt.ndim == 3:
        w_spec = pl.BlockSpec((None, tn, k), lambda i, j: (layer, rm(j), 0))
    else:
        w_spec = pl.BlockSpec((tn, k), lambda i, j: (rm(j), 0))
    return pl.pallas_call(
        _mm_nt_body,
        grid=(m // tm, n // tn),
        in_specs=[pl.BlockSpec((tm, k), lambda i, j: (i, 0), pipeline_mode=pl.Buffered(1)), w_spec],
        out_specs=pl.BlockSpec((tm, tn), lambda i, j: (i, j)),
        out_shape=jax.ShapeDtypeStruct((m, n), out_dtype),
        compiler_params=_cparams(("arbitrary", "arbitrary")),
        name=name,
    )(a, wt)


def _out_proj_body(da_ref, ft_ref, gla_ref, w_ref, r_ref, g_ref, o_ref):
    k1, k2 = DA_WIDTH, DA_WIDTH + FT_WIDTH
    acc = (jnp.dot(da_ref[...], w_ref[:k1, :].astype(BF16), preferred_element_type=F32)
           + jnp.dot(ft_ref[...].astype(BF16), w_ref[k1:k2, :].astype(BF16), preferred_element_type=F32)
           + jnp.dot(gla_ref[...], w_ref[k2:, :].astype(BF16), preferred_element_type=F32))
    o_ref[...] = r_ref[...] + g_ref[...] * acc


def out_proj(da, ft, gla, w_out, layer, res, gate, *, tm, tn, name="out_proj"):
    m = da.shape[0]
    k, n = w_out.shape[1:]
    tm = min(tm, m)
    once = pl.Buffered(1)
    return pl.pallas_call(
        _out_proj_body,
        grid=(m // tm, n // tn),
        in_specs=[
            pl.BlockSpec((tm, DA_WIDTH), lambda i, j: (i, 0), pipeline_mode=once),
            pl.BlockSpec((tm, FT_WIDTH), lambda i, j: (i, 0), pipeline_mode=once),
            pl.BlockSpec((tm, GLA_WIDTH), lambda i, j: (i, 0), pipeline_mode=once),
            pl.BlockSpec((None, k, tn), lambda i, j: (layer, 0, j)),
            pl.BlockSpec((tm, tn), lambda i, j: (i, j)),
            pl.BlockSpec((1, tn), lambda i, j: (0, j)),
        ],
        out_specs=pl.BlockSpec((tm, tn), lambda i, j: (i, j)),
        out_shape=jax.ShapeDtypeStruct((m, n), F32),
        compiler_params=_cparams(("arbitrary", "arbitrary")),
        name=name,
    )(da, ft, gla, w_out, res, gate.reshape(1, n))


def _mm_ksplit_res_body(a_ref, b_ref, r_ref, g_ref, o_ref, acc_ref):
    kk = pl.program_id(2)

    @pl.when(kk == 0)
    def _():
        acc_ref[...] = jnp.zeros_like(acc_ref)

    acc_ref[...] += jnp.dot(a_ref[...], b_ref[...], preferred_element_type=F32)

    @pl.when(kk == pl.num_programs(2) - 1)
    def _():
        o_ref[...] = r_ref[...] + g_ref[...] * acc_ref[...]


def matmul_ksplit_res(a, b, layer, res, gate, *, tm, tn, tk, name="matmul_ksplit"):
    m, k = a.shape
    n = b.shape[2]
    tm = min(tm, m)
    return pl.pallas_call(
        _mm_ksplit_res_body,
        grid=(m // tm, n // tn, k // tk),
        in_specs=[
            pl.BlockSpec((tm, tk), lambda i, j, kk: (i, kk)),
            pl.BlockSpec((None, tk, tn), lambda i, j, kk: (layer, kk, j)),
            pl.BlockSpec((tm, tn), lambda i, j, kk: (i, j)),
            pl.BlockSpec((1, tn), lambda i, j, kk: (0, j)),
        ],
        out_specs=pl.BlockSpec((tm, tn), lambda i, j, kk: (i, j)),
        out_shape=jax.ShapeDtypeStruct((m, n), F32),
        scratch_shapes=[pltpu.VMEM((tm, tn), F32)],
        compiler_params=_cparams(("parallel", "arbitrary", "arbitrary")),
        name=name,
    )(a, b, res, gate.reshape(1, n))


def _qk_prep_body(use_rope, p_ref, gq_ref, gk_ref, cos_ref, sa_ref, sb_ref, qt_ref, k_ref, vt_ref):
    n_groups = (2 * DA_HEADS * DA_HEAD_DIM) // LANES
    if use_rope:
        cos, sa, sb = cos_ref[...], sa_ref[...], sb_ref[...]

    def prep(col, gain, scale):
        x = p_ref[:, col:col + LANES].astype(F32)
        ms = jnp.mean(x * x, axis=-1, keepdims=True)
        y = x * lax.rsqrt(ms + EPS) * gain
        if use_rope:
            y = y * cos + pltpu.roll(y, 32, 1) * sa + pltpu.roll(y, 96, 1) * sb
        return y * scale if scale != 1.0 else y

    gq, gk = gq_ref[...], gk_ref[...]
    for g in range(n_groups):
        qt_ref[g * LANES:(g + 1) * LANES, :] = prep(P_Q + g * LANES, gq, Q_SCALE).T.astype(qt_ref.dtype)
        k_ref[:, g * LANES:(g + 1) * LANES] = prep(P_K + g * LANES, gk, 1.0).astype(k_ref.dtype)
    for h in range(DA_HEADS):
        v = p_ref[:, P_V + h * DA_V_DIM:P_V + (h + 1) * DA_V_DIM].astype(F32)
        vt_ref[h] = v.T.astype(vt_ref.dtype)


def qk_prep(p, gq, gk, rope, tm):
    rows = p.shape[0]
    use_rope = rope is not None
    width = 2 * DA_HEADS * DA_HEAD_DIM
    vec = pl.BlockSpec((1, LANES), lambda i: (0, 0))
    tab = pl.BlockSpec((tm, LANES), lambda i: (i, 0))
    if use_rope:
        tabs = list(rope)
    else:
        tabs = [jnp.zeros((rows, LANES), F32)] * 3
    return pl.pallas_call(
        functools.partial(_qk_prep_body, use_rope),
        grid=(rows // tm,),
        in_specs=[pl.BlockSpec((tm, P_V + DA_WIDTH), lambda i: (i, 0)), vec, vec, tab, tab, tab],
        out_specs=[pl.BlockSpec((width, tm), lambda i: (0, i)),
                   pl.BlockSpec((tm, width), lambda i: (i, 0)),
                   pl.BlockSpec((DA_HEADS, None, DA_V_DIM, tm), lambda i: (0, i, 0, 0))],
        out_shape=[jax.ShapeDtypeStruct((width, rows), BF16),
                   jax.ShapeDtypeStruct((rows, width), BF16),
                   jax.ShapeDtypeStruct((DA_HEADS, rows // tm, DA_V_DIM, tm), BF16)],
        compiler_params=_cparams(("arbitrary",)),
        name="qk_prep",
    )(p, gq.reshape(1, LANES), gk.reshape(1, LANES), *tabs)


def _attn_body(lam_init, n_chunks, tk, has_ctx, *refs):
    if has_ctx:
        (qt_ref, k_ref, vt_ref, kc_ref, vtc_ref, lq1_ref, lk1_ref, lq2_ref, lk2_ref, g_ref,
         o_ref, m_ref, l_ref, acc_ref) = refs
    else:
        (qt_ref, k_ref, vt_ref, lq1_ref, lk1_ref, lq2_ref, lk2_ref, g_ref,
         o_ref, m_ref, l_ref, acc_ref) = refs
    d = DA_HEAD_DIM
    qt = (qt_ref[:d, :], qt_ref[d:, :])
    m_ref[...] = jnp.full_like(m_ref, -jnp.inf)
    l_ref[...] = jnp.zeros_like(l_ref)
    acc_ref[...] = jnp.zeros_like(acc_ref)

    def update(k_blk, vt_blk):
        for sb in range(qt_ref.shape[1] // ATTN_TQ_SUB):
            ls = slice(sb * ATTN_TQ_SUB, (sb + 1) * ATTN_TQ_SUB)
            for mi in range(2):
                s = jnp.dot(k_blk[:, mi * d:(mi + 1) * d], qt[mi][:, ls],
                            preferred_element_type=F32)
                m_old = m_ref[mi, :, ls]
                m_new = jnp.maximum(m_old, jnp.max(s, axis=0, keepdims=True))
                alpha = jnp.exp2(m_old - m_new)
                p = jnp.exp2(s - m_new)
                l_ref[mi, :, ls] = alpha * l_ref[mi, :, ls] + jnp.sum(p, axis=0, keepdims=True)
                acc_ref[mi, :, ls] = (alpha * acc_ref[mi, :, ls]
                                      + jnp.dot(vt_blk, p.astype(BF16), preferred_element_type=F32))
                m_ref[mi, :, ls] = m_new

    if has_ctx:
        update(kc_ref[...], vtc_ref[...])

    def step(c, carry):
        off = pl.multiple_of(c * tk, tk)
        update(k_ref[pl.ds(off, tk), :], vt_ref[c])
        return carry

    lax.fori_loop(0, n_chunks, step, 0, unroll=math.gcd(n_chunks, 16))

    lam =(jnp.exp(jnp.sum(lq1_ref[...] * lk1_ref[...])) - jnp.exp(jnp.sum(lq2_ref[...] * lk2_ref[...]))
           + lam_init)
    ot = acc_ref[0] * (1.0 / l_ref[0]) - lam * (acc_ref[1] * (1.0 / l_ref[1]))
    ms = jnp.mean(ot * ot, axis=0, keepdims=True)
    o = (ot * lax.rsqrt(ms + EPS)).T
    o_ref[...] = (o * g_ref[...] * (1.0 - lam_init)).astype(o_ref.dtype)


def diff_attention(qt, k, vt, kc, vtc, lams, subln_g, lam_init, *, tq):
    t_q = qt.shape[1]
    t_k = k.shape[0]
    n_chunks, tk = vt.shape[1], vt.shape[3]
    has_ctx = kc is not None
    tq = min(tq, t_q)
    hw = 2 * DA_HEAD_DIM
    in_specs = [
        pl.BlockSpec((hw, tq), lambda h, i: (h, i)),
        pl.BlockSpec((t_k, hw), lambda h, i: (0, h)),
        pl.BlockSpec((None, n_chunks, DA_V_DIM, tk), lambda h, i: (h, 0, 0, 0)),
    ]
    args = [qt, k, vt]
    if has_ctx:
        t_c = kc.shape[0]
        in_specs += [pl.BlockSpec((t_c, hw), lambda h, i: (0, h)),
                     pl.BlockSpec((None, None, DA_V_DIM, t_c), lambda h, i: (h, 0, 0, 0))]
        args += [kc, vtc]
    vec = pl.BlockSpec((1, DA_HEAD_DIM), lambda h, i: (0, 0))
    in_specs += [vec] * 4 + [pl.BlockSpec((1, DA_V_DIM), lambda h, i: (0, 0))]
    args += [v.reshape(1, DA_HEAD_DIM) for v in lams] + [subln_g.reshape(1, DA_V_DIM)]
    return pl.pallas_call(
        functools.partial(_attn_body, lam_init, n_chunks, tk, has_ctx),
        grid=(DA_HEADS, t_q // tq),
        in_specs=in_specs,
        out_specs=pl.BlockSpec((tq, DA_V_DIM), lambda h, i: (i, h)),
        out_shape=jax.ShapeDtypeStruct((t_q, DA_WIDTH), BF16),
        scratch_shapes=[pltpu.VMEM((2, 1, tq), F32), pltpu.VMEM((2, 1, tq), F32),
                        pltpu.VMEM((2, DA_V_DIM, tq), F32)],
        compiler_params=_cparams(("parallel", "arbitrary")),
        name="diff_attention",
    )(*args)


def _dft_tables_latent(t_len):
    n2 = FFT_N2
    n1 = t_len // n2
    f1 = np.arange(n1)[:, None]
    t1 = np.arange(n1)[None, :]
    ang = 2.0 * np.pi * ((f1 * t1) % n1) / n1
    stage_a = np.concatenate([np.cos(ang), np.sin(ang)], axis=0)
    f2 = np.arange(n2)[None, :, None]
    t2 = np.arange(n2)[None, None, :]
    f1b = np.arange(n1)[:, None, None]
    idx = (t2 * f2 * n1 + t2 * f1b) % t_len
    th = 2.0 * np.pi * idx / t_len
    mr, mi = np.cos(th), -np.sin(th)
    stage_b = np.concatenate([np.concatenate([mr, mi], axis=2), np.concatenate([mi, -mr], axis=2)], axis=1)
    return stage_a.astype(np.float32), stage_b.astype(np.float32)


def _dft_table_channels(t_len, sign):
    c = np.arange(FT_GROUP_DIM)
    ang = 2.0 * np.pi * ((c[:, None] * c[None, :]) % FT_GROUP_DIM) / FT_GROUP_DIM
    scale = 1.0 / math.sqrt(t_len * FT_GROUP_DIM)
    return (np.concatenate([np.cos(ang), sign * np.sin(ang)], axis=0) * scale).astype(np.float32)


FFT_GROUP = 8


def _fft_a_body(cs_ref, z_ref, p_ref, q_ref):
    n1 = z_ref.shape[0]
    width = z_ref.shape[2]
    x = jnp.concatenate([z_ref[:, s, :] for s in range(FFT_GROUP)], axis=1).astype(BF16)
    r = jnp.dot(cs_ref[...], x, preferred_element_type=F32)
    for s in range(FFT_GROUP):
        p_ref[:, s, :] = r[:n1, s * width:(s + 1) * width]
        q_ref[:, s, :] = r[n1:, s * width:(s + 1) * width]


def _fft_b_body(m_ref, p_ref, q_ref, ch_ref, o_ref):
    n2 = FFT_N2
    ch = ch_ref[...]
    for ff in range(FFT_GROUP):
        pq = jnp.concatenate([p_ref[ff], q_ref[ff]], axis=0).astype(BF16)
        r = jnp.dot(m_ref[ff], pq, preferred_element_type=F32)
        ys = []
        for g in range(FT_GROUPS):
            c0 = g * FT_GROUP_DIM
            rg = jnp.concatenate([r[:n2, c0:c0 + FT_GROUP_DIM], r[n2:, c0:c0 + FT_GROUP_DIM]], axis=1)
            ys.append(jnp.dot(rg.astype(BF16), ch, preferred_element_type=F32))
        o_ref[:, ff, :] = jnp.concatenate(ys, axis=1)


def fourier_mix_latent(z):
    t_len, width = z.shape
    n2 = FFT_N2
    n1 = t_len // n2
    ta, tb = _dft_tables_latent(t_len)
    cs = jnp.asarray(ta).astype(BF16)
    mt = jnp.asarray(tb).astype(BF16)
    ch = jnp.asarray(_dft_table_channels(t_len, 1.0)).astype(BF16)
    gsz = FFT_GROUP
    slab = pl.BlockSpec((n1, gsz, width), lambda j: (0, j, 0))
    pv, qv = pl.pallas_call(
        _fft_a_body,
        grid=(n2 // gsz,),
        in_specs=[pl.BlockSpec((2 * n1, n1), lambda j: (0, 0)), slab],
        out_specs=[slab, slab],
        out_shape=[jax.ShapeDtypeStruct((n1, n2, width), F32)] * 2,
        compiler_params=_cparams(("arbitrary",)),
        name="fft_stage_a",
    )(cs, z.reshape(n1, n2, width))
    slabs = pl.BlockSpec((gsz, n2, width), lambda f: (f, 0, 0))
    out = pl.pallas_call(
        _fft_b_body,
        grid=(n1 // gsz,),
        in_specs=[
            pl.BlockSpec((gsz, 2 * n2, 2 * n2), lambda f: (f, 0, 0)),
            slabs, slabs,
            pl.BlockSpec((2 * FT_GROUP_DIM, FT_GROUP_DIM), lambda f: (0, 0)),
        ],
        out_specs=pl.BlockSpec((n2, gsz, width), lambda f: (0, f, 0)),
        out_shape=jax.ShapeDtypeStruct((n2, n1, width), F32),
        compiler_params=_cparams(("arbitrary",)),
        name="fft_stage_b",
    )(mt, pv, qv, ch)
    return out.reshape(t_len, width)


def _fft_ctx_body(cs_ref, z_ref, ch_ref, o_ref):
    t_len = z_ref.shape[0]
    r = jnp.dot(cs_ref[...], z_ref[...], preferred_element_type=F32)
    ch = ch_ref[...]
    for g in range(FT_GROUPS):
        c0 = g * FT_GROUP_DIM
        rg = jnp.concatenate([r[:t_len, c0:c0 + FT_GROUP_DIM], r[t_len:, c0:c0 + FT_GROUP_DIM]], axis=1).astype(BF16)
        o_ref[:, c0:c0 + FT_GROUP_DIM] = jnp.dot(rg, ch, preferred_element_type=F32).astype(o_ref.dtype)


def fourier_mix_short(z):
    t_len, width = z.shape
    f = np.arange(t_len)
    ang = 2.0 * np.pi * ((f[:, None] * f[None, :]) % t_len) / t_len
    cs = jnp.asarray(np.concatenate([np.cos(ang), np.sin(ang)], axis=0).astype(np.float32)).astype(BF16)
    ch = jnp.asarray(_dft_table_channels(t_len, -1.0)).astype(BF16)
    return pl.pallas_call(
        _fft_ctx_body,
        out_shape=jax.ShapeDtypeStruct((t_len, width), BF16),
        compiler_params=pltpu.CompilerParams(vmem_limit_bytes=V7X_VMEM_LIMIT_BYTES),
        name="fft_short",
    )(cs, z, ch)


def _split3(x):
    hi = x.astype(BF16)
    r1 = x - hi.astype(F32)
    mid = r1.astype(BF16)
    lo = (r1 - mid.astype(F32)).astype(BF16)
    return hi, mid, lo


def _gla_body(reverse, q_ref, k_ref, v_ref, gd_ref, qc_ref, kc_ref, vc_ref, gdc_ref, w2_ref, b2_ref,
              o_ref, st_ref):
    n = pl.program_id(0)
    rows = GLA_ROWS
    ch = GLA_CHUNK
    n_chunks = rows // ch
    dk, dv = GLA_K_DIM, GLA_V_DIM

    @pl.when(n == 0)
    def _():
        st_ref[...] = jnp.zeros_like(st_ref)

    is_ctx = n == 0
    q = jnp.where(is_ctx, qc_ref[...], q_ref[...])
    k = jnp.where(is_ctx, kc_ref[...], k_ref[...])
    v = jnp.where(is_ctx, vc_ref[...], v_ref[...])
    gd = jnp.where(is_ctx, gdc_ref[...], gd_ref[...])

    logits = jnp.dot(gd, w2_ref[...], preferred_element_type=F32) + b2_ref[...]
    log_a = (jnp.minimum(logits, 0.0) - jnp.log(1.0 + jnp.exp(-jnp.abs(logits)))) / GLA_TAU

    ri = lax.broadcasted_iota(jnp.int32, (rows, rows), 0)
    ci = lax.broadcasted_iota(jnp.int32, (rows, rows), 1)
    same = (ri // ch) == (ci // ch)
    tri = jnp.logical_and(same, (ci >= ri) if reverse else (ci <= ri))
    tri_b = jnp.where(tri, 1.0, 0.0).astype(BF16)
    hi, mid, lo = _split3(log_a)
    cum = (jnp.dot(tri_b, hi, preferred_element_type=F32) + jnp.dot(tri_b, mid, preferred_element_type=F32)
           + jnp.dot(tri_b, lo, preferred_element_type=F32))
    e_pos = jnp.exp(cum)
    e_neg = jnp.exp(-cum)
    order = list(range(n_chunks))[::-1] if reverse else list(range(n_chunks))
    last_row = [(c * ch) if reverse else (c * ch + ch - 1) for c in range(n_chunks)]
    scale = dk ** -0.5

    for h in range(GLA_HEADS):
        ks = slice(h * dk, (h + 1) * dk)
        vs = slice(h * dv, (h + 1) * dv)
        qf = q[:, ks].astype(F32) * scale
        kf = k[:, ks].astype(F32)
        vh = v[:, vs]
        q_in = (qf * e_pos[:, ks]).astype(BF16)
        k_in = (kf * e_neg[:, ks]).astype(BF16)
        a = lax.dot_general(q_in, k_in, (((1,), (1,)), ((), ())), preferred_element_type=F32)
        a = jnp.where(tri, a, 0.0).astype(BF16)
        o_intra = jnp.dot(a, vh, preferred_element_type=F32)
        st = st_ref[h]
        for c in order:
            rs = slice(c * ch, (c + 1) * ch)
            cum_last = cum[last_row[c]:last_row[c] + 1, ks]
            o_inter = lax.dot_general(q_in[rs], st.astype(BF16), (((1,), (1,)), ((), ())),
                                      preferred_element_type=F32)
            o_ref[rs, vs] = o_intra[rs] + o_inter
            k_dec = (kf[rs] * jnp.exp(cum_last - cum[rs, ks])).astype(BF16)
            kv_t = lax.dot_general(vh[rs], k_dec, (((0,), (0,)), ((), ())), preferred_element_type=F32)
            st = jnp.exp(cum_last) * st + kv_t
        st_ref[h] = st


def gla_direction(p, gd, pc, gdc, w2pad, b2, reverse):
    t_len = p.shape[0]
    rows = GLA_ROWS
    nb = t_len // rows
    hk = GLA_HEADS * GLA_K_DIM

    if reverse:
        lat = lambda n: jnp.where(n == 0, nb - 1, nb - n)
    else:
        lat = lambda n: jnp.where(n == 0, 0, n - 1)
    out_blk = lambda n: jnp.where(n == 0, nb, lat(n))
    in_specs = [
        pl.BlockSpec((rows, hk), lambda n: (lat(n), P_GQ // hk)),
        pl.BlockSpec((rows, hk), lambda n: (lat(n), P_GK // hk)),
        pl.BlockSpec((rows, GLA_WIDTH), lambda n: (lat(n), P_GV // GLA_WIDTH)),
        pl.BlockSpec((rows, LANES), lambda n: (lat(n), GLA_WIDTH // LANES)),
        pl.BlockSpec((rows, hk), lambda n: (0, P_GQ // hk)),
        pl.BlockSpec((rows, hk), lambda n: (0, P_GK // hk)),
        pl.BlockSpec((rows, GLA_WIDTH), lambda n: (0, P_GV // GLA_WIDTH)),
        pl.BlockSpec((rows, LANES), lambda n: (0, GLA_WIDTH // LANES)),
        pl.BlockSpec((LANES, hk), lambda n: (0, 0)),
        pl.BlockSpec((1, hk), lambda n: (0, 0)),
    ]
    return pl.pallas_call(
        functools.partial(_gla_body, reverse),
        grid=(nb + 1,),
        in_specs=in_specs,
        out_specs=pl.BlockSpec((rows, GLA_WIDTH), lambda n: (out_blk(n), 0)),
        out_shape=jax.ShapeDtypeStruct((t_len + rows, GLA_WIDTH), F32),
        scratch_shapes=[pltpu.VMEM((GLA_HEADS, GLA_V_DIM, GLA_K_DIM), F32)],
        compiler_params=_cparams(("arbitrary",)),
        name="gla_bwd" if reverse else "gla_fwd",
    )(p, p, p, gd, pc, pc, pc, gdc, w2pad, b2.reshape(1, hk))


def _gla_out_body(of_ref, ob_ref, r_ref, g_ref, y_ref):
    g = g_ref[...]
    for h in range(GLA_HEADS):
        vs = slice(h * GLA_V_DIM, (h + 1) * GLA_V_DIM)
        o = of_ref[:, vs] + ob_ref[:, vs]
        ms = jnp.mean(o * o, axis=-1, keepdims=True)
        r = r_ref[:, vs].astype(F32)
        y_ref[:, vs] = (o * lax.rsqrt(ms + EPS) * g * (r * _sigmoid(r))).astype(y_ref.dtype)


def gla_output(o_f, o_b, r, g, row_block0, tm=256):
    n_rows = r.shape[0]
    o_spec = pl.BlockSpec((tm, GLA_WIDTH), lambda i: (row_block0 + i, 0))
    return pl.pallas_call(
        _gla_out_body,
        grid=(n_rows // tm,),
        in_specs=[o_spec, o_spec, pl.BlockSpec((tm, GLA_WIDTH), lambda i: (i, 0)),
                  pl.BlockSpec((1, GLA_V_DIM), lambda i: (0, 0))],
        out_specs=pl.BlockSpec((tm, GLA_WIDTH), lambda i: (i, 0)),
        out_shape=jax.ShapeDtypeStruct((n_rows, GLA_WIDTH), BF16),
        compiler_params=_cparams(("arbitrary",)),
        name="gla_output",
    )(o_f, o_b, r, g.reshape(1, GLA_V_DIM))


FFN_HALO = 16


def _ffn_up_body(n_row_tiles, a_ref, ap_ref, an_ref, wg_ref, wv_ref, cwg_ref, cwv_ref, cbg_ref, cbv_ref,
                 o_ref, ext_ref):
    i = pl.program_id(0)
    tm = a_ref.shape[0]
    hl = FFN_HALO

    @pl.when(pl.program_id(1) == 0)
    def _():
        ext_ref[hl:hl + tm, :] = a_ref[...]
        zero = jnp.zeros_like(ap_ref)
        ext_ref[0:hl, :] = jnp.where(i > 0, ap_ref[...], zero)
        ext_ref[hl + tm:, :] = jnp.where(i < n_row_tiles - 1, an_ref[...], zero)

    ext = ext_ref[...]

    def conv(w_ref, cw_ref, cb_ref):
        u = jnp.dot(ext, w_ref[...].astype(BF16), preferred_element_type=F32)
        c = (pltpu.roll(u, 1, 0) * cw_ref[0:1, :] + u * cw_ref[1:2, :]
             + pltpu.roll(u, tm + 2 * hl - 1, 0) * cw_ref[2:3, :])
        return c[hl:hl + tm] + cb_ref[...]

    gate = conv(wg_ref, cwg_ref, cbg_ref)
    val = conv(wv_ref, cwv_ref, cbv_ref)
    o_ref[...] = (gate * _sigmoid(gate) * val).astype(o_ref.dtype)


def ffn_up_gated(h, w_up, conv_w, conv_b, layer, *, tm, tn):
    m, k = h.shape
    f = w_up.shape[2] // 2
    tm = min(tm, m)
    nt = m // tm
    nc = f // tn
    hb = tm // FFN_HALO
    last = m // FFN_HALO - 1
    gate_half = lambda rows: pl.BlockSpec((None, rows, tn), lambda i, j: (layer, 0, j))
    val_half = lambda rows: pl.BlockSpec((None, rows, tn), lambda i, j: (layer, 0, j + nc))
    cb = conv_b.reshape(conv_b.shape[0], 1, 2 * f)
    return pl.pallas_call(
        functools.partial(_ffn_up_body, nt),
        grid=(nt, nc),
        in_specs=[
            pl.BlockSpec((tm, k), lambda i, j: (i, 0), pipeline_mode=pl.Buffered(1)),
            pl.BlockSpec((FFN_HALO, k), lambda i, j: (jnp.maximum(i * hb - 1, 0), 0)),
            pl.BlockSpec((FFN_HALO, k), lambda i, j: (jnp.minimum((i + 1) * hb, last), 0)),
            gate_half(k), val_half(k), gate_half(3), val_half(3), gate_half(1), val_half(1),
        ],
        out_specs=pl.BlockSpec((tm, tn), lambda i, j: (i, j)),
        out_shape=jax.ShapeDtypeStruct((m, f), BF16),
        scratch_shapes=[pltpu.VMEM((tm + 2 * FFN_HALO, k), BF16)],
        compiler_params=_cparams(("arbitrary", "arbitrary")),
        name="ffn_up_gated",
    )(h, h, h, w_up, w_up, conv_w, conv_w, cb, cb)


def _rope_tables(t_len):
    rows = t_len // GRID_W
    row = jnp.repeat(jnp.arange(rows), GRID_W).astype(F32)
    col = jnp.tile(jnp.arange(GRID_W), rows).astype(F32)
    half = DA_HEAD_DIM // 2
    freqs = ROPE_BASE ** (-jnp.arange(0, half, 2, dtype=F32) / half)
    ar = row[:, None] * freqs
    ac = col[:, None] * freqs
    ang = jnp.concatenate([ar, ar, ac, ac], axis=-1)
    cos, sin = jnp.cos(ang), jnp.sin(ang)
    hi = (jnp.arange(DA_HEAD_DIM) % 64) >= 32
    return cos, jnp.where(hi, sin, 0.0), jnp.where(hi, 0.0, -sin)


def _ffn(x, h, w_up, conv_w, conv_b, w_down_bf16, l, gate):
    act = ffn_up_gated(h, w_up, conv_w, conv_b, l, tm=1024, tn=256)
    return matmul_ksplit_res(act, w_down_bf16, l, x, gate, tm=1024, tn=512, tk=D_FF // 2, name="ffn_down")


def _in_proj(h, w_in_t, l, w_rg, suffix):
    p = matmul_nt(h, w_in_t, layer=l, tm=2048, tn=512, n_out=P_WIDTH, name="in_proj" + suffix)
    rg = matmul_nt(h, w_rg, tm=2048, tn=RG_TN, name="in_proj_rg" + suffix)
    return p, rg


def kernel(x, c, ctx, c_ctx, w_ada, b_ada, norm1_g, norm2_g, w_in, q_norm_g, k_norm_g, lambda_q1, lambda_k1,
           lambda_q2, lambda_k2, da_subln_g, gla_gate_w2, gla_gate_b, gla_norm_g, w_out, w_up, conv_w, conv_b,
           w_down):
    depth = w_ada.shape[0]
    t_len = x.shape[1]
    d = D_MODEL
    xl = x[0]
    xc = ctx[0]
    rope = _rope_tables(t_len)
    w_in_t = jnp.swapaxes(w_in, 1, 2)
    w_down_bf16 = w_down.astype(BF16)

    cvec = jnp.zeros((8, d), F32).at[0].set(c[0]).at[1].set(c_ctx)
    mod = adaln(cvec, w_ada, b_ada)

    for l in range(depth):
        need_ctx = l < depth - 1
        lam_init = 0.8 - 0.6 * math.exp(-0.3 * l)
        mods_l = [mod[l, 0, i * d:(i + 1) * d] for i in range(N_MOD)]
        mods_c = [mod[l, 1, i * d:(i + 1) * d] for i in range(N_MOD)]

        w_rg = jnp.concatenate([w_in_t[l, W_GR:W_GR + GLA_WIDTH, :], w_in_t[l, W_GD:W_GR, :],
                                jnp.zeros((RG_WIDTH - GLA_WIDTH - 2 * GLA_GATE_RANK, d), F32)], axis=0)

        h_l = modulate(xl, norm1_g[l], mods_l[0], mods_l[1])
        h_c = modulate(xc, norm1_g[l], mods_c[0], mods_c[1])
        p_l, rg_l = _in_proj(h_l, w_in_t, l, w_rg, "")
        p_c, rg_c = _in_proj(h_c, w_in_t, l, w_rg, "_ctx")

        qt_l, k_l, vt_l = qk_prep(p_l, q_norm_g[l], k_norm_g[l], rope, ATTN_TK)
        qt_c, k_c, vt_c = qk_prep(p_c, q_norm_g[l], k_norm_g[l], None, xc.shape[0])
        lams = (lambda_q1[l], lambda_k1[l], lambda_q2[l], lambda_k2[l])
        da_l = diff_attention(qt_l, k_l, vt_l, k_c, vt_c, lams, da_subln_g[l], lam_init, tq=1024)

        ft_l = fourier_mix_latent(p_l[:, P_FT:P_FT + FT_WIDTH].astype(F32))

        w2 = gla_gate_w2[l]
        zpad = jnp.zeros((LANES - 2 * GLA_GATE_RANK, w2.shape[-1]), F32)
        zr = jnp.zeros((GLA_GATE_RANK, w2.shape[-1]), F32)
        w2_f = jnp.concatenate([w2[0], zr, zpad], axis=0).astype(BF16)
        w2_b = jnp.concatenate([zr, w2[1], zpad], axis=0).astype(BF16)
        o_f = gla_direction(p_l, rg_l, p_c, rg_c, w2_f, gla_gate_b[l, 0], False)
        o_b = gla_direction(p_l, rg_l, p_c, rg_c, w2_b, gla_gate_b[l, 1], True)
        gla_l = gla_output(o_f, o_b, rg_l, gla_norm_g[l], 0)

        x_new = out_proj(da_l, ft_l, gla_l, w_out, l, xl, mods_l[2], tm=2048, tn=256)
        h2_l = modulate(x_new, norm2_g[l], mods_l[3], mods_l[4])
        x_new = _ffn(x_new, h2_l, w_up, conv_w, conv_b, w_down_bf16, l, mods_l[5])

        if need_ctx:
            da_c = diff_attention(qt_c, k_c, vt_c, None, None, lams, da_subln_g[l], lam_init, tq=256)
            ft_c = fourier_mix_short(p_c[:, P_FT:P_FT + FT_WIDTH])
            gla_c = gla_output(o_f, o_b, rg_c, gla_norm_g[l], t_len // 256)
            xc_new = out_proj(da_c, ft_c, gla_c, w_out, l, xc, mods_c[2], tm=1024, tn=512, name="out_proj_ctx")
            h2_c = modulate(xc_new, norm2_g[l], mods_c[3], mods_c[4])
            xc = _ffn(xc_new, h2_c, w_up, conv_w, conv_b, w_down_bf16, l, mods_c[5])
        xl = x_new

    return xl[None]
```

```python
import functools
import math

import numpy as np
import jax
import jax.numpy as jnp
from jax import lax
from jax.experimental import pallas as pl
from jax.experimental.pallas import tpu as pltpu

F32 = jnp.float32
BF16 = jnp.bfloat16

D_MODEL = 4096
CTX_LEN = 256
GRID_W = 64
EPS = 1e-6
ROPE_BASE = 10000.0
DA_HEAD_DIM = 128
DA_HEADS = 8
DA_V_DIM = 256
DA_WIDTH = DA_HEADS * DA_V_DIM
FT_GROUPS = 4
FT_GROUP_DIM = 256
FT_WIDTH = FT_GROUPS * FT_GROUP_DIM
GLA_HEADS = 4
GLA_V_DIM = 256
GLA_K_DIM = 128
GLA_WIDTH = GLA_HEADS * GLA_V_DIM
GLA_GATE_RANK = 16
GLA_TAU = 16.0
GLA_CHUNK = 64
D_FF = 11008
N_MOD = 6

P_Q, P_K, P_V, P_FT, P_GQ, P_GK, P_GV = 0, 2048, 4096, 6144, 7168, 7680, 8192
P_GD, P_GR = 9216, 9248
P_WIDTH = 10272
RG_BLOCK = 1152

V7X_VMEM_LIMIT_BYTES = 56 * 1024 * 1024
LANES = 128
FFT_N2 = 128
GLA_ROWS = 256
ATTN_TK = 512
ATTN_TQ_SUB = 256
Q_SCALE = DA_HEAD_DIM ** -0.5 * math.log2(math.e)


def _cparams(sem):
    return pltpu.CompilerParams(dimension_semantics=sem, vmem_limit_bytes=V7X_VMEM_LIMIT_BYTES)


def _sigmoid(x):
    return 1.0 / (1.0 + jnp.exp(-x))


def _adaln_body(c_ref, w_ref, b_ref, o_ref):
    c = c_ref[...]
    s = (c * _sigmoid(c)).astype(BF16)
    w = w_ref[...].astype(BF16)
    o_ref[...] = jnp.dot(s, w, preferred_element_type=F32) + b_ref[...]


def adaln(cvec, w_ada, b_ada, tn=512):
    depth, d, n = w_ada.shape
    return pl.pallas_call(
        _adaln_body,
        grid=(depth, n // tn),
        in_specs=[
            pl.BlockSpec((8, d), lambda l, j: (0, 0)),
            pl.BlockSpec((None, d, tn), lambda l, j: (l, 0, j)),
            pl.BlockSpec((None, 1, tn), lambda l, j: (l, 0, j)),
        ],
        out_specs=pl.BlockSpec((None, 8, tn), lambda l, j: (l, 0, j)),
        out_shape=jax.ShapeDtypeStruct((depth, 8, n), F32),
        compiler_params=_cparams(("arbitrary", "arbitrary")),
        name="adaln",
    )(cvec, w_ada, b_ada.reshape(depth, 1, n))


def _modulate_body(x_ref, g_ref, sh_ref, sc_ref, o_ref):
    x = x_ref[...]
    ms = jnp.mean(x * x, axis=-1, keepdims=True)
    y = x * lax.rsqrt(ms + EPS) * g_ref[...]
    o_ref[...] = (y * (1.0 + sc_ref[...]) + sh_ref[...]).astype(o_ref.dtype)


def modulate(x, g, shift, scale, tm=512):
    m, d = x.shape
    tm = min(tm, m)
    vec = pl.BlockSpec((1, d), lambda i: (0, 0))
    return pl.pallas_call(
        _modulate_body,
        grid=(m // tm,),
        in_specs=[pl.BlockSpec((tm, d), lambda i: (i, 0)), vec, vec, vec],
        out_specs=pl.BlockSpec((tm, d), lambda i: (i, 0)),
        out_shape=jax.ShapeDtypeStruct((m, d), BF16),
        compiler_params=_cparams(("arbitrary",)),
        name="modulate",
    )(x, g.reshape(1, d), shift.reshape(1, d), scale.reshape(1, d))


def _mm_nt_body(a_ref, w_ref, o_ref):
    w = w_ref[...].astype(BF16)
    acc = lax.dot_general(a_ref[...], w, (((1,), (1,)), ((), ())), preferred_element_type=F32)
    o_ref[...] = acc.astype(o_ref.dtype)


def matmul_nt(a, wt, *, tm, tn, out_dtype=BF16, layer=None, n_out=None, row_map=None, name="matmul_nt"):
    m, k = a.shape
    n = n_out if n_out is not None else wt.shape[-2]
    tm = min(tm, m)
    rm = row_map if row_map is not None else (lambda j: j)
    if wt.ndim == 3:
        w_spec = pl.BlockSpec((None, tn, k), lambda i, j: (layer, rm(j), 0))
    else:
        w_spec = pl.BlockSpec((tn, k), lambda i, j: (rm(j), 0))
    return pl.pallas_call(
        _mm_nt_body,
        grid=(m // tm, pl.cdiv(n, tn)),
        in_specs=[pl.BlockSpec((tm, k), lambda i, j: (i, 0), pipeline_mode=pl.Buffered(1)), w_spec],
        out_specs=pl.BlockSpec((tm, tn), lambda i, j: (i, j)),
        out_shape=jax.ShapeDtypeStruct((m, n), out_dtype),
        compiler_params=_cparams(("arbitrary", "arbitrary")),
        name=name,
    )(a, wt)


def _out_proj_body(da_ref, ft_ref, gla_ref, w_ref, r_ref, g_ref, o_ref):
    k1, k2 = DA_WIDTH, DA_WIDTH + FT_WIDTH
    acc = (jnp.dot(da_ref[...], w_ref[:k1, :].astype(BF16), preferred_element_type=F32)
           + jnp.dot(ft_ref[...].astype(BF16), w_ref[k1:k2, :].astype(BF16), preferred_element_type=F32)
           + jnp.dot(gla_ref[...], w_ref[k2:, :].astype(BF16), preferred_element_type=F32))
    o_ref[...] = r_ref[...] + g_ref[...] * acc


def out_proj(da, ft, gla, w_out, layer, res, gate, *, tm, tn, name="out_proj"):
    m = da.shape[0]
    k, n = w_out.shape[1:]
    tm = min(tm, m)
    once = pl.Buffered(1)
    return pl.pallas_call(
        _out_proj_body,
        grid=(m // tm, n // tn),
        in_specs=[
            pl.BlockSpec((tm, DA_WIDTH), lambda i, j: (i, 0), pipeline_mode=once),
            pl.BlockSpec((tm, FT_WIDTH), lambda i, j: (i, 0), pipeline_mode=once),
            pl.BlockSpec((tm, GLA_WIDTH), lambda i, j: (i, 0), pipeline_mode=once),
            pl.BlockSpec((None, k, tn), lambda i, j: (layer, 0, j)),
            pl.BlockSpec((tm, tn), lambda i, j: (i, j)),
            pl.BlockSpec((1, tn), lambda i, j: (0, j)),
        ],
        out_specs=pl.BlockSpec((tm, tn), lambda i, j: (i, j)),
        out_shape=jax.ShapeDtypeStruct((m, n), F32),
        compiler_params=_cparams(("arbitrary", "arbitrary")),
        name=name,
    )(da, ft, gla, w_out, res, gate.reshape(1, n))


def _mm_ksplit_res_body(a_ref, b_ref, r_ref, g_ref, o_ref, acc_ref):
    kk = pl.program_id(2)

    @pl.when(kk == 0)
    def _():
        acc_ref[...] = jnp.zeros_like(acc_ref)

    acc_ref[...] += jnp.dot(a_ref[...], b_ref[...], preferred_element_type=F32)

    @pl.when(kk == pl.num_programs(2) - 1)
    def _():
        o_ref[...] = r_ref[...] + g_ref[...] * acc_ref[...]


def matmul_ksplit_res(a, b, layer, res, gate, *, tm, tn, tk, name="matmul_ksplit"):
    m, k = a.shape
    n = b.shape[2]
    tm = min(tm, m)
    return pl.pallas_call(
        _mm_ksplit_res_body,
        grid=(m // tm, n // tn, k // tk),
        in_specs=[
            pl.BlockSpec((tm, tk), lambda i, j, kk: (i, kk)),
            pl.BlockSpec((None, tk, tn), lambda i, j, kk: (layer, kk, j)),
            pl.BlockSpec((tm, tn), lambda i, j, kk: (i, j)),
            pl.BlockSpec((1, tn), lambda i, j, kk: (0, j)),
        ],
        out_specs=pl.BlockSpec((tm, tn), lambda i, j, kk: (i, j)),
        out_shape=jax.ShapeDtypeStruct((m, n), F32),
        scratch_shapes=[pltpu.VMEM((tm, tn), F32)],
        compiler_params=_cparams(("parallel", "arbitrary", "arbitrary")),
        name=name,
    )(a, b, res, gate.reshape(1, n))


def _qk_prep_body(use_rope, p_ref, gq_ref, gk_ref, cos_ref, sa_ref, sb_ref, qt_ref, k_ref, vt_ref):
    n_groups = (2 * DA_HEADS * DA_HEAD_DIM) // LANES
    if use_rope:
        cos, sa, sb = cos_ref[...], sa_ref[...], sb_ref[...]

    def prep(col, gain, scale):
        x = p_ref[:, col:col + LANES].astype(F32)
        ms = jnp.mean(x * x, axis=-1, keepdims=True)
        y = x * lax.rsqrt(ms + EPS) * gain
        if use_rope:
            y = y * cos + pltpu.roll(y, 32, 1) * sa + pltpu.roll(y, 96, 1) * sb
        return y * scale if scale != 1.0 else y

    gq, gk = gq_ref[...], gk_ref[...]
    for g in range(n_groups):
        qt_ref[g * LANES:(g + 1) * LANES, :] = prep(P_Q + g * LANES, gq, Q_SCALE).T.astype(qt_ref.dtype)
        k_ref[:, g * LANES:(g + 1) * LANES] = prep(P_K + g * LANES, gk, 1.0).astype(k_ref.dtype)
    for h in range(DA_HEADS):
        v = p_ref[:, P_V + h * DA_V_DIM:P_V + (h + 1) * DA_V_DIM].astype(F32)
        vt_ref[h] = v.T.astype(vt_ref.dtype)


def qk_prep(p, gq, gk, rope, tm):
    rows = p.shape[0]
    use_rope = rope is not None
    width = 2 * DA_HEADS * DA_HEAD_DIM
    vec = pl.BlockSpec((1, LANES), lambda i: (0, 0))
    tab = pl.BlockSpec((tm, LANES), lambda i: (i, 0))
    if use_rope:
        tabs = list(rope)
    else:
        tabs = [jnp.zeros((rows, LANES), F32)] * 3
    return pl.pallas_call(
        functools.partial(_qk_prep_body, use_rope),
        grid=(rows // tm,),
        in_specs=[pl.BlockSpec((tm, P_V + DA_WIDTH), lambda i: (i, 0)), vec, vec, tab, tab, tab],
        out_specs=[pl.BlockSpec((width, tm), lambda i: (0, i)),
                   pl.BlockSpec((tm, width), lambda i: (i, 0)),
                   pl.BlockSpec((DA_HEADS, None, DA_V_DIM, tm), lambda i: (0, i, 0, 0))],
        out_shape=[jax.ShapeDtypeStruct((width, rows), BF16),
                   jax.ShapeDtypeStruct((rows, width), BF16),
                   jax.ShapeDtypeStruct((DA_HEADS, rows // tm, DA_V_DIM, tm), BF16)],
        compiler_params=_cparams(("arbitrary",)),
        name="qk_prep",
    )(p, gq.reshape(1, LANES), gk.reshape(1, LANES), *tabs)


def _attn_body(lam_init, n_chunks, tk, has_ctx, *refs):
    if has_ctx:
        (qt_ref, k_ref, vt_ref, kc_ref, vtc_ref, lq1_ref, lk1_ref, lq2_ref, lk2_ref, g_ref,
         o_ref, m_ref, l_ref, acc_ref) = refs
    else:
        (qt_ref, k_ref, vt_ref, lq1_ref, lk1_ref, lq2_ref, lk2_ref, g_ref,
         o_ref, m_ref, l_ref, acc_ref) = refs
    d = DA_HEAD_DIM
    qt = (qt_ref[:d, :], qt_ref[d:, :])
    m_ref[...] = jnp.full_like(m_ref, -jnp.inf)
    l_ref[...] = jnp.zeros_like(l_ref)
    acc_ref[...] = jnp.zeros_like(acc_ref)

    def update(k_blk, vt_blk):
        for sb in range(qt_ref.shape[1] // ATTN_TQ_SUB):
            ls = slice(sb * ATTN_TQ_SUB, (sb + 1) * ATTN_TQ_SUB)
            for mi in range(2):
                s = jnp.dot(k_blk[:, mi * d:(mi + 1) * d], qt[mi][:, ls],
                            preferred_element_type=F32)
                m_old = m_ref[mi, :, ls]
                m_new = jnp.maximum(m_old, jnp.max(s, axis=0, keepdims=True))
                alpha = jnp.exp2(m_old - m_new)
                p = jnp.exp2(s - m_new)
                l_ref[mi, :, ls] = alpha * l_ref[mi, :, ls] + jnp.sum(p, axis=0, keepdims=True)
                acc_ref[mi, :, ls] = (alpha * acc_ref[mi, :, ls]
                                      + jnp.dot(vt_blk, p.astype(BF16), preferred_element_type=F32))
                m_ref[mi, :, ls] = m_new

    if has_ctx:
        update(kc_ref[...], vtc_ref[...])

    def step(c, carry):
        off = pl.multiple_of(c * tk, tk)
        update(k_ref[pl.ds(off, tk), :], vt_ref[c])
        return carry

    lax.fori_loop(0, n_chunks, step, 0, unroll=math.gcd(n_chunks, 16))

    lam =(jnp.exp(jnp.sum(lq1_ref[...] * lk1_ref[...])) - jnp.exp(jnp.sum(lq2_ref[...] * lk2_ref[...]))
           + lam_init)
    ot = acc_ref[0] * (1.0 / l_ref[0]) - lam * (acc_ref[1] * (1.0 / l_ref[1]))
    ms = jnp.mean(ot * ot, axis=0, keepdims=True)
    o = (ot * lax.rsqrt(ms + EPS)).T
    o_ref[...] = (o * g_ref[...] * (1.0 - lam_init)).astype(o_ref.dtype)


def diff_attention(qt, k, vt, kc, vtc, lams, subln_g, lam_init, *, tq):
    t_q = qt.shape[1]
    t_k = k.shape[0]
    n_chunks, tk = vt.shape[1], vt.shape[3]
    has_ctx = kc is not None
    tq = min(tq, t_q)
    hw = 2 * DA_HEAD_DIM
    in_specs = [
        pl.BlockSpec((hw, tq), lambda h, i: (h, i)),
        pl.BlockSpec((t_k, hw), lambda h, i: (0, h)),
        pl.BlockSpec((None, n_chunks, DA_V_DIM, tk), lambda h, i: (h, 0, 0, 0)),
    ]
    args = [qt, k, vt]
    if has_ctx:
        t_c = kc.shape[0]
        in_specs += [pl.BlockSpec((t_c, hw), lambda h, i: (0, h)),
                     pl.BlockSpec((None, None, DA_V_DIM, t_c), lambda h, i: (h, 0, 0, 0))]
        args += [kc, vtc]
    vec = pl.BlockSpec((1, DA_HEAD_DIM), lambda h, i: (0, 0))
    in_specs += [vec] * 4 + [pl.BlockSpec((1, DA_V_DIM), lambda h, i: (0, 0))]
    args += [v.reshape(1, DA_HEAD_DIM) for v in lams] + [subln_g.reshape(1, DA_V_DIM)]
    return pl.pallas_call(
        functools.partial(_attn_body, lam_init, n_chunks, tk, has_ctx),
        grid=(DA_HEADS, t_q // tq),
        in_specs=in_specs,
        out_specs=pl.BlockSpec((tq, DA_V_DIM), lambda h, i: (i, h)),
        out_shape=jax.ShapeDtypeStruct((t_q, DA_WIDTH), BF16),
        scratch_shapes=[pltpu.VMEM((2, 1, tq), F32), pltpu.VMEM((2, 1, tq), F32),
                        pltpu.VMEM((2, DA_V_DIM, tq), F32)],
        compiler_params=_cparams(("parallel", "arbitrary")),
        name="diff_attention",
    )(*args)


def _dft_tables_latent(t_len):
    n2 = FFT_N2
    n1 = t_len // n2
    f1 = np.arange(n1)[:, None]
    t1 = np.arange(n1)[None, :]
    ang = 2.0 * np.pi * ((f1 * t1) % n1) / n1
    stage_a = np.concatenate([np.cos(ang), np.sin(ang)], axis=0)
    f2 = np.arange(n2)[None, :, None]
    t2 = np.arange(n2)[None, None, :]
    f1b = np.arange(n1)[:, None, None]
    idx = (t2 * f2 * n1 + t2 * f1b) % t_len
    th = 2.0 * np.pi * idx / t_len
    mr, mi = np.cos(th), -np.sin(th)
    stage_b = np.concatenate([np.concatenate([mr, mi], axis=2), np.concatenate([mi, -mr], axis=2)], axis=1)
    return stage_a.astype(np.float32), stage_b.astype(np.float32)


def _dft_table_channels(t_len, sign):
    c = np.arange(FT_GROUP_DIM)
    ang = 2.0 * np.pi * ((c[:, None] * c[None, :]) % FT_GROUP_DIM) / FT_GROUP_DIM
    scale = 1.0 / math.sqrt(t_len * FT_GROUP_DIM)
    return (np.concatenate([np.cos(ang), sign * np.sin(ang)], axis=0) * scale).astype(np.float32)


FFT_GROUP = 8


def _fft_a_body(cs_ref, z_ref, p_ref, q_ref):
    n1 = z_ref.shape[0]
    width = z_ref.shape[2]
    x = jnp.concatenate([z_ref[:, s, :] for s in range(FFT_GROUP)], axis=1).astype(BF16)
    r = jnp.dot(cs_ref[...], x, preferred_element_type=F32)
    for s in range(FFT_GROUP):
        p_ref[:, s, :] = r[:n1, s * width:(s + 1) * width]
        q_ref[:, s, :] = r[n1:, s * width:(s + 1) * width]


def _fft_b_body(m_ref, p_ref, q_ref, ch_ref, o_ref):
    n2 = FFT_N2
    ch = ch_ref[...]
    for ff in range(FFT_GROUP):
        pq = jnp.concatenate([p_ref[ff], q_ref[ff]], axis=0).astype(BF16)
        r = jnp.dot(m_ref[ff], pq, preferred_element_type=F32)
        ys = []
        for g in range(FT_GROUPS):
            c0 = g * FT_GROUP_DIM
            rg = jnp.concatenate([r[:n2, c0:c0 + FT_GROUP_DIM], r[n2:, c0:c0 + FT_GROUP_DIM]], axis=1)
            ys.append(jnp.dot(rg.astype(BF16), ch, preferred_element_type=F32))
        o_ref[:, ff, :] = jnp.concatenate(ys, axis=1)


def fourier_mix_latent(z):
    t_len, width = z.shape
    n2 = FFT_N2
    n1 = t_len // n2
    ta, tb = _dft_tables_latent(t_len)
    cs = jnp.asarray(ta).astype(BF16)
    mt = jnp.asarray(tb).astype(BF16)
    ch = jnp.asarray(_dft_table_channels(t_len, 1.0)).astype(BF16)
    gsz = FFT_GROUP
    slab = pl.BlockSpec((n1, gsz, width), lambda j: (0, j, 0))
    pv, qv = pl.pallas_call(
        _fft_a_body,
        grid=(n2 // gsz,),
        in_specs=[pl.BlockSpec((2 * n1, n1), lambda j: (0, 0)), slab],
        out_specs=[slab, slab],
        out_shape=[jax.ShapeDtypeStruct((n1, n2, width), F32)] * 2,
        compiler_params=_cparams(("arbitrary",)),
        name="fft_stage_a",
    )(cs, z.reshape(n1, n2, width))
    slabs = pl.BlockSpec((gsz, n2, width), lambda f: (f, 0, 0))
    out = pl.pallas_call(
        _fft_b_body,
        grid=(n1 // gsz,),
        in_specs=[
            pl.BlockSpec((gsz, 2 * n2, 2 * n2), lambda f: (f, 0, 0)),
            slabs, slabs,
            pl.BlockSpec((2 * FT_GROUP_DIM, FT_GROUP_DIM), lambda f: (0, 0)),
        ],
        out_specs=pl.BlockSpec((n2, gsz, width), lambda f: (0, f, 0)),
        out_shape=jax.ShapeDtypeStruct((n2, n1, width), F32),
        compiler_params=_cparams(("arbitrary",)),
        name="fft_stage_b",
    )(mt, pv, qv, ch)
    return out.reshape(t_len, width)


def _fft_ctx_body(cs_ref, z_ref, ch_ref, o_ref):
    t_len = z_ref.shape[0]
    r = jnp.dot(cs_ref[...], z_ref[...], preferred_element_type=F32)
    ch = ch_ref[...]
    for g in range(FT_GROUPS):
        c0 = g * FT_GROUP_DIM
        rg = jnp.concatenate([r[:t_len, c0:c0 + FT_GROUP_DIM], r[t_len:, c0:c0 + FT_GROUP_DIM]], axis=1).astype(BF16)
        o_ref[:, c0:c0 + FT_GROUP_DIM] = jnp.dot(rg, ch, preferred_element_type=F32).astype(o_ref.dtype)


def fourier_mix_short(z):
    t_len, width = z.shape
    f = np.arange(t_len)
    ang = 2.0 * np.pi * ((f[:, None] * f[None, :]) % t_len) / t_len
    cs = jnp.asarray(np.concatenate([np.cos(ang), np.sin(ang)], axis=0).astype(np.float32)).astype(BF16)
    ch = jnp.asarray(_dft_table_channels(t_len, -1.0)).astype(BF16)
    return pl.pallas_call(
        _fft_ctx_body,
        out_shape=jax.ShapeDtypeStruct((t_len, width), BF16),
        compiler_params=pltpu.CompilerParams(vmem_limit_bytes=V7X_VMEM_LIMIT_BYTES),
        name="fft_short",
    )(cs, z, ch)


def _split3(x):
    hi = x.astype(BF16)
    r1 = x - hi.astype(F32)
    mid = r1.astype(BF16)
    lo = (r1 - mid.astype(F32)).astype(BF16)
    return hi, mid, lo


def _gla_body(reverse, q_ref, k_ref, v_ref, gd_ref, qc_ref, kc_ref, vc_ref, gdc_ref, w2_ref, b2_ref,
              o_ref, st_ref):
    n = pl.program_id(0)
    rows = GLA_ROWS
    ch = GLA_CHUNK
    n_chunks = rows // ch
    dk, dv = GLA_K_DIM, GLA_V_DIM

    @pl.when(n == 0)
    def _():
        st_ref[...] = jnp.zeros_like(st_ref)

    is_ctx = n == 0
    q = jnp.where(is_ctx, qc_ref[...], q_ref[...])
    k = jnp.where(is_ctx, kc_ref[...], k_ref[...])
    v = jnp.where(is_ctx, vc_ref[...], v_ref[...])
    gd = jnp.where(is_ctx, gdc_ref[...], gd_ref[...])

    logits = jnp.dot(gd, w2_ref[...], preferred_element_type=F32) + b2_ref[...]
    log_a = (jnp.minimum(logits, 0.0) - jnp.log(1.0 + jnp.exp(-jnp.abs(logits)))) / GLA_TAU

    ri = lax.broadcasted_iota(jnp.int32, (rows, rows), 0)
    ci = lax.broadcasted_iota(jnp.int32, (rows, rows), 1)
    same = (ri // ch) == (ci // ch)
    tri = jnp.logical_and(same, (ci >= ri) if reverse else (ci <= ri))
    tri_b = jnp.where(tri, 1.0, 0.0).astype(BF16)
    hi, mid, lo = _split3(log_a)
    cum = (jnp.dot(tri_b, hi, preferred_element_type=F32) + jnp.dot(tri_b, mid, preferred_element_type=F32)
           + jnp.dot(tri_b, lo, preferred_element_type=F32))
    e_pos = jnp.exp(cum)
    e_neg = jnp.exp(-cum)
    order = list(range(n_chunks))[::-1] if reverse else list(range(n_chunks))
    last_row = [(c * ch) if reverse else (c * ch + ch - 1) for c in range(n_chunks)]
    scale = dk ** -0.5

    for h in range(GLA_HEADS):
        ks = slice(h * dk, (h + 1) * dk)
        vs = slice(h * dv, (h + 1) * dv)
        qf = q[:, ks].astype(F32) * scale
        kf = k[:, ks].astype(F32)
        vh = v[:, vs]
        q_in = (qf * e_pos[:, ks]).astype(BF16)
        k_in = (kf * e_neg[:, ks]).astype(BF16)
        a = lax.dot_general(q_in, k_in, (((1,), (1,)), ((), ())), preferred_element_type=F32)
        a = jnp.where(tri, a, 0.0).astype(BF16)
        o_intra = jnp.dot(a, vh, preferred_element_type=F32)
        st = st_ref[h]
        for c in order:
            rs = slice(c * ch, (c + 1) * ch)
            cum_last = cum[last_row[c]:last_row[c] + 1, ks]
            o_inter = lax.dot_general(q_in[rs], st.astype(BF16), (((1,), (1,)), ((), ())),
                                      preferred_element_type=F32)
            o_ref[rs, vs] = o_intra[rs] + o_inter
            k_dec = (kf[rs] * jnp.exp(cum_last - cum[rs, ks])).astype(BF16)
            kv_t = lax.dot_general(vh[rs], k_dec, (((0,), (0,)), ((), ())), preferred_element_type=F32)
            st = jnp.exp(cum_last) * st + kv_t
        st_ref[h] = st


def gla_direction(p, pc, w2pad, b2, reverse):
    t_len = p.shape[0]
    rows = GLA_ROWS
    nb = t_len // rows
    hk = GLA_HEADS * GLA_K_DIM

    if reverse:
        lat = lambda n: jnp.where(n == 0, nb - 1, nb - n)
    else:
        lat = lambda n: jnp.where(n == 0, 0, n - 1)
    out_blk = lambda n: jnp.where(n == 0, nb, lat(n))
    in_specs = [
        pl.BlockSpec((rows, hk), lambda n: (lat(n), P_GQ // hk)),
        pl.BlockSpec((rows, hk), lambda n: (lat(n), P_GK // hk)),
        pl.BlockSpec((rows, GLA_WIDTH), lambda n: (lat(n), P_GV // GLA_WIDTH)),
        pl.BlockSpec((rows, LANES), lambda n: (lat(n), P_GD // LANES)),
        pl.BlockSpec((rows, hk), lambda n: (0, P_GQ // hk)),
        pl.BlockSpec((rows, hk), lambda n: (0, P_GK // hk)),
        pl.BlockSpec((rows, GLA_WIDTH), lambda n: (0, P_GV // GLA_WIDTH)),
        pl.BlockSpec((rows, LANES), lambda n: (0, P_GD // LANES)),
        pl.BlockSpec((LANES, hk), lambda n: (0, 0)),
        pl.BlockSpec((1, hk), lambda n: (0, 0)),
    ]
    return pl.pallas_call(
        functools.partial(_gla_body, reverse),
        grid=(nb + 1,),
        in_specs=in_specs,
        out_specs=pl.BlockSpec((rows, GLA_WIDTH), lambda n: (out_blk(n), 0)),
        out_shape=jax.ShapeDtypeStruct((t_len + rows, GLA_WIDTH), F32),
        scratch_shapes=[pltpu.VMEM((GLA_HEADS, GLA_V_DIM, GLA_K_DIM), F32)],
        compiler_params=_cparams(("arbitrary",)),
        name="gla_bwd" if reverse else "gla_fwd",
    )(p, p, p, p, pc, pc, pc, pc, w2pad, b2.reshape(1, hk))


def _gla_out_body(of_ref, ob_ref, rg_ref, g_ref, y_ref):
    g = g_ref[...]
    r_all = rg_ref[...].astype(F32)
    for h in range(GLA_HEADS):
        vs = slice(h * GLA_V_DIM, (h + 1) * GLA_V_DIM)
        o = of_ref[:, vs] + ob_ref[:, vs]
        ms = jnp.mean(o * o, axis=-1, keepdims=True)
        r0 = P_GR - P_GD + h * GLA_V_DIM
        r = r_all[:, r0:r0 + GLA_V_DIM]
        y_ref[:, vs] = (o * lax.rsqrt(ms + EPS) * g * (r * _sigmoid(r))).astype(y_ref.dtype)


def gla_output(o_f, o_b, p, g, row_block0, tm=256):
    n_rows = p.shape[0]
    o_spec = pl.BlockSpec((tm, GLA_WIDTH), lambda i: (row_block0 + i, 0))
    return pl.pallas_call(
        _gla_out_body,
        grid=(n_rows // tm,),
        in_specs=[o_spec, o_spec, pl.BlockSpec((tm, RG_BLOCK), lambda i: (i, P_GD // RG_BLOCK)),
                  pl.BlockSpec((1, GLA_V_DIM), lambda i: (0, 0))],
        out_specs=pl.BlockSpec((tm, GLA_WIDTH), lambda i: (i, 0)),
        out_shape=jax.ShapeDtypeStruct((n_rows, GLA_WIDTH), BF16),
        compiler_params=_cparams(("arbitrary",)),
        name="gla_output",
    )(o_f, o_b, p, g.reshape(1, GLA_V_DIM))


FFN_HALO = 16


def _ffn_up_body(n_row_tiles, a_ref, ap_ref, an_ref, wg_ref, wv_ref, cwg_ref, cwv_ref, cbg_ref, cbv_ref,
                 o_ref, ext_ref):
    i = pl.program_id(0)
    tm = a_ref.shape[0]
    hl = FFN_HALO

    @pl.when(pl.program_id(1) == 0)
    def _():
        ext_ref[hl:hl + tm, :] = a_ref[...]
        zero = jnp.zeros_like(ap_ref)
        ext_ref[0:hl, :] = jnp.where(i > 0, ap_ref[...], zero)
        ext_ref[hl + tm:, :] = jnp.where(i < n_row_tiles - 1, an_ref[...], zero)

    ext = ext_ref[...]

    def conv(w_ref, cw_ref, cb_ref):
        u = jnp.dot(ext, w_ref[...].astype(BF16), preferred_element_type=F32)
        c = (pltpu.roll(u, 1, 0) * cw_ref[0:1, :] + u * cw_ref[1:2, :]
             + pltpu.roll(u, tm + 2 * hl - 1, 0) * cw_ref[2:3, :])
        return c[hl:hl + tm] + cb_ref[...]

    gate = conv(wg_ref, cwg_ref, cbg_ref)
    val = conv(wv_ref, cwv_ref, cbv_ref)
    o_ref[...] = (gate * _sigmoid(gate) * val).astype(o_ref.dtype)


def ffn_up_gated(h, w_up, conv_w, conv_b, layer, *, tm, tn):
    m, k = h.shape
    f = w_up.shape[2] // 2
    tm = min(tm, m)
    nt = m // tm
    nc = f // tn
    hb = tm // FFN_HALO
    last = m // FFN_HALO - 1
    gate_half = lambda rows: pl.BlockSpec((None, rows, tn), lambda i, j: (layer, 0, j))
    val_half = lambda rows: pl.BlockSpec((None, rows, tn), lambda i, j: (layer, 0, j + nc))
    cb = conv_b.reshape(conv_b.shape[0], 1, 2 * f)
    return pl.pallas_call(
        functools.partial(_ffn_up_body, nt),
        grid=(nt, nc),
        in_specs=[
            pl.BlockSpec((tm, k), lambda i, j: (i, 0), pipeline_mode=pl.Buffered(1)),
            pl.BlockSpec((FFN_HALO, k), lambda i, j: (jnp.maximum(i * hb - 1, 0), 0)),
            pl.BlockSpec((FFN_HALO, k), lambda i, j: (jnp.minimum((i + 1) * hb, last), 0)),
            gate_half(k), val_half(k), gate_half(3), val_half(3), gate_half(1), val_half(1),
        ],
        out_specs=pl.BlockSpec((tm, tn), lambda i, j: (i, j)),
        out_shape=jax.ShapeDtypeStruct((m, f), BF16),
        scratch_shapes=[pltpu.VMEM((tm + 2 * FFN_HALO, k), BF16)],
        compiler_params=_cparams(("arbitrary", "arbitrary")),
        name="ffn_up_gated",
    )(h, h, h, w_up, w_up, conv_w, conv_w, cb, cb)


def _rope_tables(t_len):
    rows = t_len // GRID_W
    row = jnp.repeat(jnp.arange(rows), GRID_W).astype(F32)
    col = jnp.tile(jnp.arange(GRID_W), rows).astype(F32)
    half = DA_HEAD_DIM // 2
    freqs = ROPE_BASE ** (-jnp.arange(0, half, 2, dtype=F32) / half)
    ar = row[:, None] * freqs
    ac = col[:, None] * freqs
    ang = jnp.concatenate([ar, ar, ac, ac], axis=-1)
    cos, sin = jnp.cos(ang), jnp.sin(ang)
    hi = (jnp.arange(DA_HEAD_DIM) % 64) >= 32
    return cos, jnp.where(hi, sin, 0.0), jnp.where(hi, 0.0, -sin)


def _ffn(x, h, w_up, conv_w, conv_b, w_down_bf16, l, gate):
    act = ffn_up_gated(h, w_up, conv_w, conv_b, l, tm=1024, tn=256)
    return matmul_ksplit_res(act, w_down_bf16, l, x, gate, tm=1024, tn=512, tk=D_FF // 2, name="ffn_down")


def _in_proj(h, w_in_t, l, suffix):
    return matmul_nt(h, w_in_t, layer=l, tm=2048, tn=512, name="in_proj" + suffix)


def kernel(x, c, ctx, c_ctx, w_ada, b_ada, norm1_g, norm2_g, w_in, q_norm_g, k_norm_g, lambda_q1, lambda_k1,
           lambda_q2, lambda_k2, da_subln_g, gla_gate_w2, gla_gate_b, gla_norm_g, w_out, w_up, conv_w, conv_b,
           w_down):
    depth = w_ada.shape[0]
    t_len = x.shape[1]
    d = D_MODEL
    xl = x[0]
    xc = ctx[0]
    rope = _rope_tables(t_len)
    w_in_t = jnp.swapaxes(w_in, 1, 2)
    w_down_bf16 = w_down.astype(BF16)

    cvec = jnp.zeros((8, d), F32).at[0].set(c[0]).at[1].set(c_ctx)
    mod = adaln(cvec, w_ada, b_ada)

    for l in range(depth):
        need_ctx = l < depth - 1
        lam_init = 0.8 - 0.6 * math.exp(-0.3 * l)
        mods_l = [mod[l, 0, i * d:(i + 1) * d] for i in range(N_MOD)]
        mods_c = [mod[l, 1, i * d:(i + 1) * d] for i in range(N_MOD)]

        h_l = modulate(xl, norm1_g[l], mods_l[0], mods_l[1])
        h_c = modulate(xc, norm1_g[l], mods_c[0], mods_c[1])
        p_l = _in_proj(h_l, w_in_t, l, "")
        p_c = _in_proj(h_c, w_in_t, l, "_ctx")

        qt_l, k_l, vt_l = qk_prep(p_l, q_norm_g[l], k_norm_g[l], rope, ATTN_TK)
        qt_c, k_c, vt_c = qk_prep(p_c, q_norm_g[l], k_norm_g[l], None, xc.shape[0])
        lams = (lambda_q1[l], lambda_k1[l], lambda_q2[l], lambda_k2[l])
        da_l = diff_attention(qt_l, k_l, vt_l, k_c, vt_c, lams, da_subln_g[l], lam_init, tq=1024)

        ft_l = fourier_mix_latent(p_l[:, P_FT:P_FT + FT_WIDTH].astype(F32))

        w2 = gla_gate_w2[l]
        zpad = jnp.zeros((LANES - 2 * GLA_GATE_RANK, w2.shape[-1]), F32)
        zr = jnp.zeros((GLA_GATE_RANK, w2.shape[-1]), F32)
        w2_f = jnp.concatenate([w2[0], zr, zpad], axis=0).astype(BF16)
        w2_b = jnp.concatenate([zr, w2[1], zpad], axis=0).astype(BF16)
        o_f = gla_direction(p_l, p_c, w2_f, gla_gate_b[l, 0], False)
        o_b = gla_direction(p_l, p_c, w2_b, gla_gate_b[l, 1], True)
        gla_l = gla_output(o_f, o_b, p_l, gla_norm_g[l], 0)

        x_new = out_proj(da_l, ft_l, gla_l, w_out, l, xl, mods_l[2], tm=2048, tn=256)
        h2_l = modulate(x_new, norm2_g[l], mods_l[3], mods_l[4])
        x_new = _ffn(x_new, h2_l, w_up, conv_w, conv_b, w_down_bf16, l, mods_l[5])

        if need_ctx:
            da_c = diff_attention(qt_c, k_c, vt_c, None, None, lams, da_subln_g[l], lam_init, tq=256)
            ft_c = fourier_mix_short(p_c[:, P_FT:P_FT + FT_WIDTH])
            gla_c = gla_output(o_f, o_b, p_c, gla_norm_g[l], t_len // 256)
            xc_new = out_proj(da_c, ft_c, gla_c, w_out, l, xc, mods_c[2], tm=1024, tn=512, name="out_proj_ctx")
            h2_c = modulate(xc_new, norm2_g[l], mods_c[3], mods_c[4])
            xc = _ffn(xc_new, h2_c, w_up, conv_w, conv_b, w_down_bf16, l, mods_c[5])
        xl = x_new

    return xl[None]
```

```python
import functools
import math

import numpy as np
import jax
import jax.numpy as jnp
from jax import lax
from jax.experimental import pallas as pl
from jax.experimental.pallas import tpu as pltpu

F32 = jnp.float32
BF16 = jnp.bfloat16

D_MODEL = 4096
CTX_LEN = 256
GRID_W = 64
EPS = 1e-6
ROPE_BASE = 10000.0
DA_HEAD_DIM = 128
DA_HEADS = 8
DA_V_DIM = 256
DA_WIDTH = DA_HEADS * DA_V_DIM
FT_GROUPS = 4
FT_GROUP_DIM = 256
FT_WIDTH = FT_GROUPS * FT_GROUP_DIM
GLA_HEADS = 4
GLA_V_DIM = 256
GLA_K_DIM = 128
GLA_WIDTH = GLA_HEADS * GLA_V_DIM
GLA_GATE_RANK = 16
GLA_TAU = 16.0
GLA_CHUNK = 64
D_FF = 11008
N_MOD = 6

P_Q, P_K, P_V, P_FT, P_GQ, P_GK, P_GV = 0, 2048, 4096, 6144, 7168, 7680, 8192
P_GD, P_GR = 9216, 9248
P_WIDTH = 10272
RG_BLOCK = 1152

V7X_VMEM_LIMIT_BYTES = 56 * 1024 * 1024
LANES = 128
FFT_N2 = 128
GLA_ROWS = 256
ATTN_TK = 512
ATTN_TQ_SUB = 256
Q_SCALE = DA_HEAD_DIM ** -0.5 * math.log2(math.e)


def _cparams(sem):
    return pltpu.CompilerParams(dimension_semantics=sem, vmem_limit_bytes=V7X_VMEM_LIMIT_BYTES)


def _sigmoid(x):
    return 1.0 / (1.0 + jnp.exp(-x))


def _adaln_body(c_ref, w_ref, b_ref, o_ref):
    c = c_ref[...]
    s = (c * _sigmoid(c)).astype(BF16)
    w = w_ref[...].astype(BF16)
    o_ref[...] = jnp.dot(s, w, preferred_element_type=F32) + b_ref[...]


def adaln(cvec, w_ada, b_ada, tn=512):
    depth, d, n = w_ada.shape
    return pl.pallas_call(
        _adaln_body,
        grid=(depth, n // tn),
        in_specs=[
            pl.BlockSpec((8, d), lambda l, j: (0, 0)),
            pl.BlockSpec((None, d, tn), lambda l, j: (l, 0, j)),
            pl.BlockSpec((None, 1, tn), lambda l, j: (l, 0, j)),
        ],
        out_specs=pl.BlockSpec((None, 8, tn), lambda l, j: (l, 0, j)),
        out_shape=jax.ShapeDtypeStruct((depth, 8, n), F32),
        compiler_params=_cparams(("arbitrary", "arbitrary")),
        name="adaln",
    )(cvec, w_ada, b_ada.reshape(depth, 1, n))


def _modulate_body(x_ref, g_ref, sh_ref, sc_ref, o_ref):
    x = x_ref[...]
    ms = jnp.mean(x * x, axis=-1, keepdims=True)
    y = x * lax.rsqrt(ms + EPS) * g_ref[...]
    o_ref[...] = (y * (1.0 + sc_ref[...]) + sh_ref[...]).astype(o_ref.dtype)


def modulate(x, g, shift, scale, tm=512):
    m, d = x.shape
    tm = min(tm, m)
    vec = pl.BlockSpec((1, d), lambda i: (0, 0))
    return pl.pallas_call(
        _modulate_body,
        grid=(m // tm,),
        in_specs=[pl.BlockSpec((tm, d), lambda i: (i, 0)), vec, vec, vec],
        out_specs=pl.BlockSpec((tm, d), lambda i: (i, 0)),
        out_shape=jax.ShapeDtypeStruct((m, d), BF16),
        compiler_params=_cparams(("arbitrary",)),
        name="modulate",
    )(x, g.reshape(1, d), shift.reshape(1, d), scale.reshape(1, d))


def _mm_nt_body(a_ref, w_ref, o_ref):
    w = w_ref[...].astype(BF16)
    acc = lax.dot_general(a_ref[...], w, (((1,), (1,)), ((), ())), preferred_element_type=F32)
    o_ref[...] = acc.astype(o_ref.dtype)


def matmul_nt(a, wt, *, tm, tn, out_dtype=BF16, layer=None, n_out=None, row_map=None, name="matmul_nt"):
    m, k = a.shape
    n = n_out if n_out is not None else wt.shape[-2]
    tm = min(tm, m)
    rm = row_map if row_map is not None else (lambda j: j)
    if wt.ndim == 3:
        w_spec = pl.BlockSpec((None, tn, k), lambda i, j: (layer, rm(j), 0))
    else:
        w_spec = pl.BlockSpec((tn, k), lambda i, j: (rm(j), 0))
    return pl.pallas_call(
        _mm_nt_body,
        grid=(m // tm, pl.cdiv(n, tn)),
        in_specs=[pl.BlockSpec((tm, k), lambda i, j: (i, 0), pipeline_mode=pl.Buffered(1)), w_spec],
        out_specs=pl.BlockSpec((tm, tn), lambda i, j: (i, j)),
        out_shape=jax.ShapeDtypeStruct((m, n), out_dtype),
        compiler_params=_cparams(("arbitrary", "arbitrary")),
        name=name,
    )(a, wt)


def _out_proj_body(da_ref, ft_ref, gla_ref, w_ref, r_ref, g_ref, o_ref):
    k1, k2 = DA_WIDTH, DA_WIDTH + FT_WIDTH
    acc = (jnp.dot(da_ref[...], w_ref[:k1, :].astype(BF16), preferred_element_type=F32)
           + jnp.dot(ft_ref[...].astype(BF16), w_ref[k1:k2, :].astype(BF16), preferred_element_type=F32)
           + jnp.dot(gla_ref[...], w_ref[k2:, :].astype(BF16), preferred_element_type=F32))
    o_ref[...] = r_ref[...] + g_ref[...] * acc


def out_proj(da, ft, gla, w_out, layer, res, gate, *, tm, tn, name="out_proj"):
    m = da.shape[0]
    k, n = w_out.shape[1:]
    tm = min(tm, m)
    once = pl.Buffered(1)
    return pl.pallas_call(
        _out_proj_body,
        grid=(m // tm, n // tn),
        in_specs=[
            pl.BlockSpec((tm, DA_WIDTH), lambda i, j: (i, 0), pipeline_mode=once),
            pl.BlockSpec((tm, FT_WIDTH), lambda i, j: (i, 0), pipeline_mode=once),
            pl.BlockSpec((tm, GLA_WIDTH), lambda i, j: (i, 0), pipeline_mode=once),
            pl.BlockSpec((None, k, tn), lambda i, j: (layer, 0, j)),
            pl.BlockSpec((tm, tn), lambda i, j: (i, j)),
            pl.BlockSpec((1, tn), lambda i, j: (0, j)),
        ],
        out_specs=pl.BlockSpec((tm, tn), lambda i, j: (i, j)),
        out_shape=jax.ShapeDtypeStruct((m, n), F32),
        compiler_params=_cparams(("arbitrary", "arbitrary")),
        name=name,
    )(da, ft, gla, w_out, res, gate.reshape(1, n))


def _mm_ksplit_res_body(a_ref, b_ref, r_ref, g_ref, o_ref, acc_ref):
    kk = pl.program_id(2)

    @pl.when(kk == 0)
    def _():
        acc_ref[...] = jnp.zeros_like(acc_ref)

    acc_ref[...] += jnp.dot(a_ref[...], b_ref[...], preferred_element_type=F32)

    @pl.when(kk == pl.num_programs(2) - 1)
    def _():
        o_ref[...] = r_ref[...] + g_ref[...] * acc_ref[...]


def matmul_ksplit_res(a, b, layer, res, gate, *, tm, tn, tk, name="matmul_ksplit"):
    m, k = a.shape
    n = b.shape[2]
    tm = min(tm, m)
    return pl.pallas_call(
        _mm_ksplit_res_body,
        grid=(m // tm, n // tn, k // tk),
        in_specs=[
            pl.BlockSpec((tm, tk), lambda i, j, kk: (i, kk)),
            pl.BlockSpec((None, tk, tn), lambda i, j, kk: (layer, kk, j)),
            pl.BlockSpec((tm, tn), lambda i, j, kk: (i, j)),
            pl.BlockSpec((1, tn), lambda i, j, kk: (0, j)),
        ],
        out_specs=pl.BlockSpec((tm, tn), lambda i, j, kk: (i, j)),
        out_shape=jax.ShapeDtypeStruct((m, n), F32),
        scratch_shapes=[pltpu.VMEM((tm, tn), F32)],
        compiler_params=_cparams(("parallel", "arbitrary", "arbitrary")),
        name=name,
    )(a, b, res, gate.reshape(1, n))


def _qk_prep_body(use_rope, p_ref, gq_ref, gk_ref, cos_ref, sa_ref, sb_ref, qt_ref, k_ref, vt_ref):
    n_groups = (2 * DA_HEADS * DA_HEAD_DIM) // LANES
    if use_rope:
        cos, sa, sb = cos_ref[...], sa_ref[...], sb_ref[...]

    def prep(col, gain, scale):
        x = p_ref[:, col:col + LANES].astype(F32)
        ms = jnp.mean(x * x, axis=-1, keepdims=True)
        y = x * lax.rsqrt(ms + EPS) * gain
        if use_rope:
            y = y * cos + pltpu.roll(y, 32, 1) * sa + pltpu.roll(y, 96, 1) * sb
        return y * scale if scale != 1.0 else y

    gq, gk = gq_ref[...], gk_ref[...]
    for g in range(n_groups):
        qt_ref[g * LANES:(g + 1) * LANES, :] = prep(P_Q + g * LANES, gq, Q_SCALE).T.astype(qt_ref.dtype)
        k_ref[:, g * LANES:(g + 1) * LANES] = prep(P_K + g * LANES, gk, 1.0).astype(k_ref.dtype)
    for h in range(DA_HEADS):
        v = p_ref[:, P_V + h * DA_V_DIM:P_V + (h + 1) * DA_V_DIM].astype(F32)
        vt_ref[h] = v.T.astype(vt_ref.dtype)


def qk_prep(p, gq, gk, rope, tm):
    rows = p.shape[0]
    use_rope = rope is not None
    width = 2 * DA_HEADS * DA_HEAD_DIM
    vec = pl.BlockSpec((1, LANES), lambda i: (0, 0))
    tab = pl.BlockSpec((tm, LANES), lambda i: (i, 0))
    if use_rope:
        tabs = list(rope)
    else:
        tabs = [jnp.zeros((rows, LANES), F32)] * 3
    return pl.pallas_call(
        functools.partial(_qk_prep_body, use_rope),
        grid=(rows // tm,),
        in_specs=[pl.BlockSpec((tm, P_V + DA_WIDTH), lambda i: (i, 0)), vec, vec, tab, tab, tab],
        out_specs=[pl.BlockSpec((width, tm), lambda i: (0, i)),
                   pl.BlockSpec((tm, width), lambda i: (i, 0)),
                   pl.BlockSpec((DA_HEADS, None, DA_V_DIM, tm), lambda i: (0, i, 0, 0))],
        out_shape=[jax.ShapeDtypeStruct((width, rows), BF16),
                   jax.ShapeDtypeStruct((rows, width), BF16),
                   jax.ShapeDtypeStruct((DA_HEADS, rows // tm, DA_V_DIM, tm), BF16)],
        compiler_params=_cparams(("arbitrary",)),
        name="qk_prep",
    )(p, gq.reshape(1, LANES), gk.reshape(1, LANES), *tabs)


def _attn_body(lam_init, n_chunks, tk, has_ctx, *refs):
    if has_ctx:
        (qt_ref, k_ref, vt_ref, kc_ref, vtc_ref, lq1_ref, lk1_ref, lq2_ref, lk2_ref, g_ref,
         o_ref, m_ref, l_ref, acc_ref) = refs
    else:
        (qt_ref, k_ref, vt_ref, lq1_ref, lk1_ref, lq2_ref, lk2_ref, g_ref,
         o_ref, m_ref, l_ref, acc_ref) = refs
    d = DA_HEAD_DIM
    qt = (qt_ref[:d, :], qt_ref[d:, :])
    m_ref[...] = jnp.full_like(m_ref, -jnp.inf)
    l_ref[...] = jnp.zeros_like(l_ref)
    acc_ref[...] = jnp.zeros_like(acc_ref)

    def update(k_blk, vt_blk):
        for sb in range(qt_ref.shape[1] // ATTN_TQ_SUB):
            ls = slice(sb * ATTN_TQ_SUB, (sb + 1) * ATTN_TQ_SUB)
            for mi in range(2):
                s = jnp.dot(k_blk[:, mi * d:(mi + 1) * d], qt[mi][:, ls],
                            preferred_element_type=F32)
                m_old = m_ref[mi, :, ls]
                m_new = jnp.maximum(m_old, jnp.max(s, axis=0, keepdims=True))
                alpha = jnp.exp2(m_old - m_new)
                p = jnp.exp2(s - m_new)
                l_ref[mi, :, ls] = alpha * l_ref[mi, :, ls] + jnp.sum(p, axis=0, keepdims=True)
                acc_ref[mi, :, ls] = (alpha * acc_ref[mi, :, ls]
                                      + jnp.dot(vt_blk, p.astype(BF16), preferred_element_type=F32))
                m_ref[mi, :, ls] = m_new

    if has_ctx:
        update(kc_ref[...], vtc_ref[...])

    def step(c, carry):
        off = pl.multiple_of(c * tk, tk)
        update(k_ref[pl.ds(off, tk), :], vt_ref[c])
        return carry

    lax.fori_loop(0, n_chunks, step, 0, unroll=math.gcd(n_chunks, 16))

    lam =(jnp.exp(jnp.sum(lq1_ref[...] * lk1_ref[...])) - jnp.exp(jnp.sum(lq2_ref[...] * lk2_ref[...]))
           + lam_init)
    ot = acc_ref[0] * (1.0 / l_ref[0]) - lam * (acc_ref[1] * (1.0 / l_ref[1]))
    ms = jnp.mean(ot * ot, axis=0, keepdims=True)
    o = (ot * lax.rsqrt(ms + EPS)).T
    o_ref[...] = (o * g_ref[...] * (1.0 - lam_init)).astype(o_ref.dtype)


def diff_attention(qt, k, vt, kc, vtc, lams, subln_g, lam_init, *, tq):
    t_q = qt.shape[1]
    t_k = k.shape[0]
    n_chunks, tk = vt.shape[1], vt.shape[3]
    has_ctx = kc is not None
    tq = min(tq, t_q)
    hw = 2 * DA_HEAD_DIM
    in_specs = [
        pl.BlockSpec((hw, tq), lambda h, i: (h, i)),
        pl.BlockSpec((t_k, hw), lambda h, i: (0, h)),
        pl.BlockSpec((None, n_chunks, DA_V_DIM, tk), lambda h, i: (h, 0, 0, 0)),
    ]
    args = [qt, k, vt]
    if has_ctx:
        t_c = kc.shape[0]
        in_specs += [pl.BlockSpec((t_c, hw), lambda h, i: (0, h)),
                     pl.BlockSpec((None, None, DA_V_DIM, t_c), lambda h, i: (h, 0, 0, 0))]
        args += [kc, vtc]
    vec = pl.BlockSpec((1, DA_HEAD_DIM), lambda h, i: (0, 0))
    in_specs += [vec] * 4 + [pl.BlockSpec((1, DA_V_DIM), lambda h, i: (0, 0))]
    args += [v.reshape(1, DA_HEAD_DIM) for v in lams] + [subln_g.reshape(1, DA_V_DIM)]
    return pl.pallas_call(
        functools.partial(_attn_body, lam_init, n_chunks, tk, has_ctx),
        grid=(DA_HEADS, t_q // tq),
        in_specs=in_specs,
        out_specs=pl.BlockSpec((tq, DA_V_DIM), lambda h, i: (i, h)),
        out_shape=jax.ShapeDtypeStruct((t_q, DA_WIDTH), BF16),
        scratch_shapes=[pltpu.VMEM((2, 1, tq), F32), pltpu.VMEM((2, 1, tq), F32),
                        pltpu.VMEM((2, DA_V_DIM, tq), F32)],
        compiler_params=_cparams(("parallel", "arbitrary")),
        name="diff_attention",
    )(*args)


def _dft_tables_latent(t_len):
    n2 = FFT_N2
    n1 = t_len // n2
    f1 = np.arange(n1)[:, None]
    t1 = np.arange(n1)[None, :]
    ang = 2.0 * np.pi * ((f1 * t1) % n1) / n1
    stage_a = np.concatenate([np.cos(ang), np.sin(ang)], axis=0)
    f2 = np.arange(n2)[None, :, None]
    t2 = np.arange(n2)[None, None, :]
    f1b = np.arange(n1)[:, None, None]
    idx = (t2 * f2 * n1 + t2 * f1b) % t_len
    th = 2.0 * np.pi * idx / t_len
    mr, mi = np.cos(th), -np.sin(th)
    stage_b = np.concatenate([np.concatenate([mr, mi], axis=2), np.concatenate([mi, -mr], axis=2)], axis=1)
    return stage_a.astype(np.float32), stage_b.astype(np.float32)


def _dft_table_channels(t_len, sign):
    c = np.arange(FT_GROUP_DIM)
    ang = 2.0 * np.pi * ((c[:, None] * c[None, :]) % FT_GROUP_DIM) / FT_GROUP_DIM
    scale = 1.0 / math.sqrt(t_len * FT_GROUP_DIM)
    return (np.concatenate([np.cos(ang), sign * np.sin(ang)], axis=0) * scale).astype(np.float32)


FFT_GROUP = 8


def _fft_a_body(cs_ref, z_ref, p_ref, q_ref):
    n1 = z_ref.shape[0]
    width = z_ref.shape[2]
    x = jnp.concatenate([z_ref[:, s, :] for s in range(FFT_GROUP)], axis=1).astype(BF16)
    r = jnp.dot(cs_ref[...], x, preferred_element_type=F32)
    for s in range(FFT_GROUP):
        p_ref[:, s, :] = r[:n1, s * width:(s + 1) * width]
        q_ref[:, s, :] = r[n1:, s * width:(s + 1) * width]


def _fft_b_body(m_ref, p_ref, q_ref, ch_ref, o_ref):
    n2 = FFT_N2
    ch = ch_ref[...]
    for ff in range(FFT_GROUP):
        pq = jnp.concatenate([p_ref[ff], q_ref[ff]], axis=0).astype(BF16)
        r = jnp.dot(m_ref[ff], pq, preferred_element_type=F32)
        ys = []
        for g in range(FT_GROUPS):
            c0 = g * FT_GROUP_DIM
            rg = jnp.concatenate([r[:n2, c0:c0 + FT_GROUP_DIM], r[n2:, c0:c0 + FT_GROUP_DIM]], axis=1)
            ys.append(jnp.dot(rg.astype(BF16), ch, preferred_element_type=F32))
        o_ref[:, ff, :] = jnp.concatenate(ys, axis=1)


def fourier_mix_latent(z):
    t_len, width = z.shape
    n2 = FFT_N2
    n1 = t_len // n2
    ta, tb = _dft_tables_latent(t_len)
    cs = jnp.asarray(ta).astype(BF16)
    mt = jnp.asarray(tb).astype(BF16)
    ch = jnp.asarray(_dft_table_channels(t_len, 1.0)).astype(BF16)
    gsz = FFT_GROUP
    slab = pl.BlockSpec((n1, gsz, width), lambda j: (0, j, 0))
    pv, qv = pl.pallas_call(
        _fft_a_body,
        grid=(n2 // gsz,),
        in_specs=[pl.BlockSpec((2 * n1, n1), lambda j: (0, 0)), slab],
        out_specs=[slab, slab],
        out_shape=[jax.ShapeDtypeStruct((n1, n2, width), F32)] * 2,
        compiler_params=_cparams(("arbitrary",)),
        name="fft_stage_a",
    )(cs, z.reshape(n1, n2, width))
    slabs = pl.BlockSpec((gsz, n2, width), lambda f: (f, 0, 0))
    out = pl.pallas_call(
        _fft_b_body,
        grid=(n1 // gsz,),
        in_specs=[
            pl.BlockSpec((gsz, 2 * n2, 2 * n2), lambda f: (f, 0, 0)),
            slabs, slabs,
            pl.BlockSpec((2 * FT_GROUP_DIM, FT_GROUP_DIM), lambda f: (0, 0)),
        ],
        out_specs=pl.BlockSpec((n2, gsz, width), lambda f: (0, f, 0)),
        out_shape=jax.ShapeDtypeStruct((n2, n1, width), F32),
        compiler_params=_cparams(("arbitrary",)),
        name="fft_stage_b",
    )(mt, pv, qv, ch)
    return out.reshape(t_len, width)


def _fft_ctx_body(cs_ref, z_ref, ch_ref, o_ref):
    t_len = z_ref.shape[0]
    r = jnp.dot(cs_ref[...], z_ref[...], preferred_element_type=F32)
    ch = ch_ref[...]
    for g in range(FT_GROUPS):
        c0 = g * FT_GROUP_DIM
        rg = jnp.concatenate([r[:t_len, c0:c0 + FT_GROUP_DIM], r[t_len:, c0:c0 + FT_GROUP_DIM]], axis=1).astype(BF16)
        o_ref[:, c0:c0 + FT_GROUP_DIM] = jnp.dot(rg, ch, preferred_element_type=F32).astype(o_ref.dtype)


def fourier_mix_short(z):
    t_len, width = z.shape
    f = np.arange(t_len)
    ang = 2.0 * np.pi * ((f[:, None] * f[None, :]) % t_len) / t_len
    cs = jnp.asarray(np.concatenate([np.cos(ang), np.sin(ang)], axis=0).astype(np.float32)).astype(BF16)
    ch = jnp.asarray(_dft_table_channels(t_len, -1.0)).astype(BF16)
    return pl.pallas_call(
        _fft_ctx_body,
        out_shape=jax.ShapeDtypeStruct((t_len, width), BF16),
        compiler_params=pltpu.CompilerParams(vmem_limit_bytes=V7X_VMEM_LIMIT_BYTES),
        name="fft_short",
    )(cs, z, ch)


def _split3(x):
    hi = x.astype(BF16)
    r1 = x - hi.astype(F32)
    mid = r1.astype(BF16)
    lo = (r1 - mid.astype(F32)).astype(BF16)
    return hi, mid, lo


def _gla_body(*refs):
    (qf_ref, kf_ref, vf_ref, gdf_ref, w2f_ref, b2f_ref, qb_ref, kb_ref, vb_ref, gdb_ref, w2b_ref, b2b_ref,
     qc_ref, kc_ref, vc_ref, gdc_ref, of_ref, ob_ref, st_ref) = refs

    @pl.when(pl.program_id(0) == 0)
    def _():
        st_ref[...] = jnp.zeros_like(st_ref)

    ctx_refs = (qc_ref, kc_ref, vc_ref, gdc_ref)
    _gla_step(False, (qf_ref, kf_ref, vf_ref, gdf_ref), ctx_refs, w2f_ref, b2f_ref, of_ref, st_ref.at[0])
    _gla_step(True, (qb_ref, kb_ref, vb_ref, gdb_ref), ctx_refs, w2b_ref, b2b_ref, ob_ref, st_ref.at[1])


def _gla_step(reverse, lat_refs, ctx_refs, w2_ref, b2_ref, o_ref, st_ref):
    rows = GLA_ROWS
    ch = GLA_CHUNK
    n_chunks = rows // ch
    dk, dv = GLA_K_DIM, GLA_V_DIM

    is_ctx = pl.program_id(0) == 0
    q, k, v, gd = (jnp.where(is_ctx, c_ref[...], l_ref[...]) for c_ref, l_ref in zip(ctx_refs, lat_refs))

    logits = jnp.dot(gd, w2_ref[...], preferred_element_type=F32) + b2_ref[...]
    log_a = (jnp.minimum(logits, 0.0) - jnp.log(1.0 + jnp.exp(-jnp.abs(logits)))) / GLA_TAU

    ri = lax.broadcasted_iota(jnp.int32, (rows, rows), 0)
    ci = lax.broadcasted_iota(jnp.int32, (rows, rows), 1)
    same = (ri // ch) == (ci // ch)
    tri = jnp.logical_and(same, (ci >= ri) if reverse else (ci <= ri))
    tri_b = jnp.where(tri, 1.0, 0.0).astype(BF16)
    hi, mid, lo = _split3(log_a)
    cum = (jnp.dot(tri_b, hi, preferred_element_type=F32) + jnp.dot(tri_b, mid, preferred_element_type=F32)
           + jnp.dot(tri_b, lo, preferred_element_type=F32))
    e_pos = jnp.exp(cum)
    e_neg = jnp.exp(-cum)
    order = list(range(n_chunks))[::-1] if reverse else list(range(n_chunks))
    last_row = [(c * ch) if reverse else (c * ch + ch - 1) for c in range(n_chunks)]
    scale = dk ** -0.5

    for h in range(GLA_HEADS):
        ks = slice(h * dk, (h + 1) * dk)
        vs = slice(h * dv, (h + 1) * dv)
        qf = q[:, ks].astype(F32) * scale
        kf = k[:, ks].astype(F32)
        vh = v[:, vs]
        q_in = (qf * e_pos[:, ks]).astype(BF16)
        k_in = (kf * e_neg[:, ks]).astype(BF16)
        a = lax.dot_general(q_in, k_in, (((1,), (1,)), ((), ())), preferred_element_type=F32)
        a = jnp.where(tri, a, 0.0).astype(BF16)
        o_intra = jnp.dot(a, vh, preferred_element_type=F32)
        st = st_ref[h]
        for c in order:
            rs = slice(c * ch, (c + 1) * ch)
            cum_last = cum[last_row[c]:last_row[c] + 1, ks]
            o_inter = lax.dot_general(q_in[rs], st.astype(BF16), (((1,), (1,)), ((), ())),
                                      preferred_element_type=F32)
            o_ref[rs, vs] = o_intra[rs] + o_inter
            k_dec = (kf[rs] * jnp.exp(cum_last - cum[rs, ks])).astype(BF16)
            kv_t = lax.dot_general(vh[rs], k_dec, (((0,), (0,)), ((), ())), preferred_element_type=F32)
            st = jnp.exp(cum_last) * st + kv_t
        st_ref[h] = st


def gla_bidirectional(p, pc, w2pad_f, b2_f, w2pad_b, b2_b):
    t_len = p.shape[0]
    rows = GLA_ROWS
    nb = t_len // rows
    hk = GLA_HEADS * GLA_K_DIM
    lat_f = lambda n: jnp.where(n == 0, 0, n - 1)
    lat_b = lambda n: jnp.where(n == 0, nb - 1, nb - n)

    def direction_specs(lat):
        return [pl.BlockSpec((rows, hk), lambda n: (lat(n), P_GQ // hk)),
                pl.BlockSpec((rows, hk), lambda n: (lat(n), P_GK // hk)),
                pl.BlockSpec((rows, GLA_WIDTH), lambda n: (lat(n), P_GV // GLA_WIDTH)),
                pl.BlockSpec((rows, LANES), lambda n: (lat(n), P_GD // LANES)),
                pl.BlockSpec((LANES, hk), lambda n: (0, 0)),
                pl.BlockSpec((1, hk), lambda n: (0, 0))]

    ctx_specs = [pl.BlockSpec((rows, hk), lambda n: (0, P_GQ // hk)),
                 pl.BlockSpec((rows, hk), lambda n: (0, P_GK // hk)),
                 pl.BlockSpec((rows, GLA_WIDTH), lambda n: (0, P_GV // GLA_WIDTH)),
                 pl.BlockSpec((rows, LANES), lambda n: (0, P_GD // LANES))]
    out_spec = lambda lat: pl.BlockSpec((rows, GLA_WIDTH), lambda n: (jnp.where(n == 0, nb, lat(n)), 0))
    out_sds = jax.ShapeDtypeStruct((t_len + rows, GLA_WIDTH), F32)
    return pl.pallas_call(
        _gla_body,
        grid=(nb + 1,),
        in_specs=direction_specs(lat_f) + direction_specs(lat_b) + ctx_specs,
        out_specs=[out_spec(lat_f), out_spec(lat_b)],
        out_shape=[out_sds, out_sds],
        scratch_shapes=[pltpu.VMEM((2, GLA_HEADS, GLA_V_DIM, GLA_K_DIM), F32)],
        compiler_params=_cparams(("arbitrary",)),
        name="gla",
    )(p, p, p, p, w2pad_f, b2_f.reshape(1, hk), p, p, p, p, w2pad_b, b2_b.reshape(1, hk), pc, pc, pc, pc)


def _gla_out_body(of_ref, ob_ref, rg_ref, g_ref, y_ref):
    g = g_ref[...]
    r_all = rg_ref[...].astype(F32)
    for h in range(GLA_HEADS):
        vs = slice(h * GLA_V_DIM, (h + 1) * GLA_V_DIM)
        o = of_ref[:, vs] + ob_ref[:, vs]
        ms = jnp.mean(o * o, axis=-1, keepdims=True)
        r0 = P_GR - P_GD + h * GLA_V_DIM
        r = r_all[:, r0:r0 + GLA_V_DIM]
        y_ref[:, vs] = (o * lax.rsqrt(ms + EPS) * g * (r * _sigmoid(r))).astype(y_ref.dtype)


def gla_output(o_f, o_b, p, g, row_block0, tm=256):
    n_rows = p.shape[0]
    o_spec = pl.BlockSpec((tm, GLA_WIDTH), lambda i: (row_block0 + i, 0))
    return pl.pallas_call(
        _gla_out_body,
        grid=(n_rows // tm,),
        in_specs=[o_spec, o_spec, pl.BlockSpec((tm, RG_BLOCK), lambda i: (i, P_GD // RG_BLOCK)),
                  pl.BlockSpec((1, GLA_V_DIM), lambda i: (0, 0))],
        out_specs=pl.BlockSpec((tm, GLA_WIDTH), lambda i: (i, 0)),
        out_shape=jax.ShapeDtypeStruct((n_rows, GLA_WIDTH), BF16),
        compiler_params=_cparams(("arbitrary",)),
        name="gla_output",
    )(o_f, o_b, p, g.reshape(1, GLA_V_DIM))


FFN_HALO = 16


def _ffn_up_body(n_row_tiles, a_ref, ap_ref, an_ref, wg_ref, wv_ref, cwg_ref, cwv_ref, cbg_ref, cbv_ref,
                 o_ref, ext_ref):
    i = pl.program_id(0)
    tm = a_ref.shape[0]
    hl = FFN_HALO

    @pl.when(pl.program_id(1) == 0)
    def _():
        ext_ref[hl:hl + tm, :] = a_ref[...]
        zero = jnp.zeros_like(ap_ref)
        ext_ref[0:hl, :] = jnp.where(i > 0, ap_ref[...], zero)
        ext_ref[hl + tm:, :] = jnp.where(i < n_row_tiles - 1, an_ref[...], zero)

    ext = ext_ref[...]

    def conv(w_ref, cw_ref, cb_ref):
        u = jnp.dot(ext, w_ref[...].astype(BF16), preferred_element_type=F32)
        c = (pltpu.roll(u, 1, 0) * cw_ref[0:1, :] + u * cw_ref[1:2, :]
             + pltpu.roll(u, tm + 2 * hl - 1, 0) * cw_ref[2:3, :])
        return c[hl:hl + tm] + cb_ref[...]

    gate = conv(wg_ref, cwg_ref, cbg_ref)
    val = conv(wv_ref, cwv_ref, cbv_ref)
    o_ref[...] = (gate * _sigmoid(gate) * val).astype(o_ref.dtype)


def ffn_up_gated(h, w_up, conv_w, conv_b, layer, *, tm, tn):
    m, k = h.shape
    f = w_up.shape[2] // 2
    tm = min(tm, m)
    nt = m // tm
    nc = f // tn
    hb = tm // FFN_HALO
    last = m // FFN_HALO - 1
    gate_half = lambda rows: pl.BlockSpec((None, rows, tn), lambda i, j: (layer, 0, j))
    val_half = lambda rows: pl.BlockSpec((None, rows, tn), lambda i, j: (layer, 0, j + nc))
    cb = conv_b.reshape(conv_b.shape[0], 1, 2 * f)
    return pl.pallas_call(
        functools.partial(_ffn_up_body, nt),
        grid=(nt, nc),
        in_specs=[
            pl.BlockSpec((tm, k), lambda i, j: (i, 0), pipeline_mode=pl.Buffered(1)),
            pl.BlockSpec((FFN_HALO, k), lambda i, j: (jnp.maximum(i * hb - 1, 0), 0)),
            pl.BlockSpec((FFN_HALO, k), lambda i, j: (jnp.minimum((i + 1) * hb, last), 0)),
            gate_half(k), val_half(k), gate_half(3), val_half(3), gate_half(1), val_half(1),
        ],
        out_specs=pl.BlockSpec((tm, tn), lambda i, j: (i, j)),
        out_shape=jax.ShapeDtypeStruct((m, f), BF16),
        scratch_shapes=[pltpu.VMEM((tm + 2 * FFN_HALO, k), BF16)],
        compiler_params=_cparams(("arbitrary", "arbitrary")),
        name="ffn_up_gated",
    )(h, h, h, w_up, w_up, conv_w, conv_w, cb, cb)


def _rope_tables(t_len):
    rows = t_len // GRID_W
    row = jnp.repeat(jnp.arange(rows), GRID_W).astype(F32)
    col = jnp.tile(jnp.arange(GRID_W), rows).astype(F32)
    half = DA_HEAD_DIM // 2
    freqs = ROPE_BASE ** (-jnp.arange(0, half, 2, dtype=F32) / half)
    ar = row[:, None] * freqs
    ac = col[:, None] * freqs
    ang = jnp.concatenate([ar, ar, ac, ac], axis=-1)
    cos, sin = jnp.cos(ang), jnp.sin(ang)
    hi = (jnp.arange(DA_HEAD_DIM) % 64) >= 32
    return cos, jnp.where(hi, sin, 0.0), jnp.where(hi, 0.0, -sin)


def _ffn(x, h, w_up, conv_w, conv_b, w_down_bf16, l, gate):
    act = ffn_up_gated(h, w_up, conv_w, conv_b, l, tm=1024, tn=256)
    return matmul_ksplit_res(act, w_down_bf16, l, x, gate, tm=1024, tn=512, tk=D_FF // 2, name="ffn_down")


def _in_proj(h, w_in_t, l, suffix):
    return matmul_nt(h, w_in_t, layer=l, tm=2048, tn=512, name="in_proj" + suffix)


def kernel(x, c, ctx, c_ctx, w_ada, b_ada, norm1_g, norm2_g, w_in, q_norm_g, k_norm_g, lambda_q1, lambda_k1,
           lambda_q2, lambda_k2, da_subln_g, gla_gate_w2, gla_gate_b, gla_norm_g, w_out, w_up, conv_w, conv_b,
           w_down):
    depth = w_ada.shape[0]
    t_len = x.shape[1]
    d = D_MODEL
    xl = x[0]
    xc = ctx[0]
    rope = _rope_tables(t_len)
    w_in_t = jnp.swapaxes(w_in, 1, 2)
    w_down_bf16 = w_down.astype(BF16)

    cvec = jnp.zeros((8, d), F32).at[0].set(c[0]).at[1].set(c_ctx)
    mod = adaln(cvec, w_ada, b_ada)

    for l in range(depth):
        need_ctx = l < depth - 1
        lam_init = 0.8 - 0.6 * math.exp(-0.3 * l)
        mods_l = [mod[l, 0, i * d:(i + 1) * d] for i in range(N_MOD)]
        mods_c = [mod[l, 1, i * d:(i + 1) * d] for i in range(N_MOD)]

        h_l = modulate(xl, norm1_g[l], mods_l[0], mods_l[1])
        h_c = modulate(xc, norm1_g[l], mods_c[0], mods_c[1])
        p_l = _in_proj(h_l, w_in_t, l, "")
        p_c = _in_proj(h_c, w_in_t, l, "_ctx")

        qt_l, k_l, vt_l = qk_prep(p_l, q_norm_g[l], k_norm_g[l], rope, ATTN_TK)
        qt_c, k_c, vt_c = qk_prep(p_c, q_norm_g[l], k_norm_g[l], None, xc.shape[0])
        lams = (lambda_q1[l], lambda_k1[l], lambda_q2[l], lambda_k2[l])
        da_l = diff_attention(qt_l, k_l, vt_l, k_c, vt_c, lams, da_subln_g[l], lam_init, tq=1024)

        ft_l = fourier_mix_latent(p_l[:, P_FT:P_FT + FT_WIDTH].astype(F32))

        w2 = gla_gate_w2[l]
        zpad = jnp.zeros((LANES - 2 * GLA_GATE_RANK, w2.shape[-1]), F32)
        zr = jnp.zeros((GLA_GATE_RANK, w2.shape[-1]), F32)
        w2_f = jnp.concatenate([w2[0], zr, zpad], axis=0).astype(BF16)
        w2_b = jnp.concatenate([zr, w2[1], zpad], axis=0).astype(BF16)
        o_f, o_b = gla_bidirectional(p_l, p_c, w2_f, gla_gate_b[l, 0], w2_b, gla_gate_b[l, 1])
        gla_l = gla_output(o_f, o_b, p_l, gla_norm_g[l], 0)

        x_new = out_proj(da_l, ft_l, gla_l, w_out, l, xl, mods_l[2], tm=2048, tn=256)
        h2_l = modulate(x_new, norm2_g[l], mods_l[3], mods_l[4])
        x_new = _ffn(x_new, h2_l, w_up, conv_w, conv_b, w_down_bf16, l, mods_l[5])

        if need_ctx:
            da_c = diff_attention(qt_c, k_c, vt_c, None, None, lams, da_subln_g[l], lam_init, tq=256)
            ft_c = fourier_mix_short(p_c[:, P_FT:P_FT + FT_WIDTH])
            gla_c = gla_output(o_f, o_b, p_c, gla_norm_g[l], t_len // 256)
            xc_new = out_proj(da_c, ft_c, gla_c, w_out, l, xc, mods_c[2], tm=1024, tn=512, name="out_proj_ctx")
            h2_c = modulate(xc_new, norm2_g[l], mods_c[3], mods_c[4])
            xc = _ffn(xc_new, h2_c, w_up, conv_w, conv_b, w_down_bf16, l, mods_c[5])
        xl = x_new

    return xl[None]
```
